```python
import jax
import jax.numpy as jnp
from jax import lax
import numpy as np

D_MODEL = 2048
BATCH = 4
SEQ = 4096
DEPTH = 2

GRID_W = 64
CTX_LEN = 256
HEAD_DIM = 128
A_HEADS = 8
A_KV_HEADS = 2
WINDOW = 128
WBLK = 128
B_HEADS = 8
Q_LORA = 512
KV_LORA = 256
NOPE_DIM = 128
ROPE_DIM = 64
V_DIM = 128
QK_DIM = NOPE_DIM + ROPE_DIM
QBLK = 128
IN_SPLITS = (A_HEADS * HEAD_DIM, A_KV_HEADS * HEAD_DIM, A_KV_HEADS * HEAD_DIM, Q_LORA, KV_LORA, ROPE_DIM)
IN_WIDTH = sum(IN_SPLITS)
MIX_WIDTH = A_HEADS * HEAD_DIM + B_HEADS * V_DIM
F_GROUPS = 8
F_GROUP_DIM = D_MODEL // F_GROUPS
N_EXPERTS = 16
N_GROUPS = 4
EXP_PER_GROUP = N_EXPERTS // N_GROUPS
TOP_K = 2
D_EXPERT = 1024
MOE_BLK = 512
ROPE_BASE = 10000.0
EPS = 1e-6
F32 = jnp.float32

kernel_name = 'hybrid_swa_mla_fourier_moe_dit'


def rmsnorm(x, g):
    xf = x.astype(F32)
    y = xf * lax.rsqrt(jnp.mean(xf * xf, axis=-1, keepdims=True) + EPS)
    return y.astype(x.dtype) * g


def modulate(h, shift, scale):
    return h * (1 + scale) + shift


def rope_half(x, ang):
    cos = jnp.cos(ang)[None, :, None, :]
    sin = jnp.sin(ang)[None, :, None, :]
    x1, x2 = jnp.split(x.astype(F32), 2, axis=-1)
    return jnp.concatenate([x1 * cos - x2 * sin, x1 * sin + x2 * cos], axis=-1)


def axial_rope(x, row, col):
    half = x.shape[-1] // 2
    freqs = ROPE_BASE ** (-jnp.arange(0, half, 2, dtype=F32) / half)
    xr = rope_half(x[..., :half], row[:, None] * freqs[None, :])
    xc = rope_half(x[..., half:], col[:, None] * freqs[None, :])
    return jnp.concatenate([xr, xc], axis=-1).astype(x.dtype)


def dense_attn(q, k, v):
    s = jnp.einsum('bqhd,bkhd->bhqk', q, k).astype(F32) * (q.shape[-1] ** -0.5)
    p = jax.nn.softmax(s, axis=-1).astype(v.dtype)
    return jnp.einsum('bhqk,bkhd->bqhd', p, v)


def window_gqa(q, k, v, kc, vc, sink):
    B, S, Hq, dh = q.shape
    Hkv = k.shape[2]
    G = Hq // Hkv
    nb = S // WBLK
    C = kc.shape[1]
    L = 3 * WBLK
    scale = dh ** -0.5
    qb = q.reshape(B, nb, WBLK, Hkv, G, dh)

    def band(t):
        tp = jnp.pad(t, ((0, 0), (WBLK, WBLK), (0, 0), (0, 0))).reshape(B, nb + 2, WBLK, Hkv, dh)
        return jnp.concatenate([tp[:, :-2], tp[:, 1:-1], tp[:, 2:]], axis=2)

    kw, vw = band(k), band(v)
    s_loc = jnp.einsum('bnqkgd,bnskd->bnkgqs', qb, kw).astype(F32) * scale
    s_ctx = jnp.einsum('bnqkgd,bckd->bnkgqc', qb, kc).astype(F32) * scale
    blk = jnp.arange(nb)[:, None, None] * WBLK
    qpos = blk + jnp.arange(WBLK)[None, :, None]
    kpos = blk - WBLK + jnp.arange(L)[None, None, :]
    valid = (jnp.abs(qpos - kpos) <= WINDOW) & (kpos >= 0) & (kpos < S)
    s_loc = jnp.where(valid[None, :, None, None], s_loc, -jnp.inf)
    s_sink = jnp.broadcast_to(sink.astype(F32).reshape(Hkv, G)[None, None, :, :, None, None], s_loc.shape[:-1] + (1,))
    p = jax.nn.softmax(jnp.concatenate([s_loc, s_ctx, s_sink], axis=-1), axis=-1).astype(v.dtype)
    o = jnp.einsum('bnkgqs,bnskd->bnqkgd', p[..., :L], vw) + jnp.einsum('bnkgqc,bckd->bnqkgd', p[..., L:L + C], vc)
    return o.reshape(B, S, Hq, dh)


def ctx_gqa(qc, kc, vc, sink):
    B, C, Hq, dh = qc.shape
    Hkv = kc.shape[2]
    G = Hq // Hkv
    s = jnp.einsum('bqkgd,bckd->bkgqc', qc.reshape(B, C, Hkv, G, dh), kc).astype(F32) * (dh ** -0.5)
    s_sink = jnp.broadcast_to(sink.astype(F32).reshape(Hkv, G)[None, :, :, None, None], s.shape[:-1] + (1,))
    p = jax.nn.softmax(jnp.concatenate([s, s_sink], axis=-1), axis=-1).astype(vc.dtype)
    return jnp.einsum('bkgqc,bckd->bqkgd', p[..., :C], vc).reshape(B, C, Hq, dh)


def mla_attn(q, k, v, kc, vc):
    B, S, H, dq = q.shape
    nb = S // QBLK
    k_all = jnp.concatenate([kc, k], axis=1)
    v_all = jnp.concatenate([vc, v], axis=1)
    qb = jnp.moveaxis(q.reshape(B, nb, QBLK, H, dq), 1, 0)
    o = lax.map(lambda qblk: dense_attn(qblk, k_all, v_all), qb)
    return jnp.moveaxis(o, 0, 1).reshape(B, S, H, v.shape[-1])


def mixer_ab(h, hc, row, col, w_in, g_aqn, g_akn, g_bq_lat, w_bq_up, g_bkv_lat, w_bkv_up, g_bqn, g_bkn, sink, w_o, ctx_out):
    cuts = np.cumsum(IN_SPLITS)[:-1].tolist()

    def project(t, with_q):
        lead = t.shape[:2]
        aq, ak, av, bqd, bkvd, bkr = jnp.split(t @ w_in, cuts, axis=-1)
        ak = rmsnorm(ak.reshape(lead + (A_KV_HEADS, HEAD_DIM)), g_akn)
        av = av.reshape(lead + (A_KV_HEADS, HEAD_DIM))
        kv = (rmsnorm(bkvd, g_bkv_lat) @ w_bkv_up).reshape(lead + (B_HEADS, NOPE_DIM + V_DIM))
        k_rope = jnp.broadcast_to(bkr[:, :, None, :], lead + (B_HEADS, ROPE_DIM))
        bk = rmsnorm(jnp.concatenate([kv[..., :NOPE_DIM], k_rope], axis=-1), g_bkn)
        bv = kv[..., NOPE_DIM:]
        if not with_q:
            return None, ak, av, None, bk, bv
        aq = rmsnorm(aq.reshape(lead + (A_HEADS, HEAD_DIM)), g_aqn)
        bq = rmsnorm((rmsnorm(bqd, g_bq_lat) @ w_bq_up).reshape(lead + (B_HEADS, QK_DIM)), g_bqn)
        return aq, ak, av, bq, bk, bv

    def rope_tail(t):
        return jnp.concatenate([t[..., :NOPE_DIM], axial_rope(t[..., NOPE_DIM:], row, col)], axis=-1)

    aq, ak, av, bq, bk, bv = project(h, True)
    aq, ak = axial_rope(aq, row, col), axial_rope(ak, row, col)
    bq, bk = rope_tail(bq), rope_tail(bk)
    aqc, akc, avc, bqc, bkc, bvc = project(hc, ctx_out)
    B, S = h.shape[:2]
    ya = window_gqa(aq, ak, av, akc, avc, sink)
    yb = mla_attn(bq, bk, bv, bkc, bvc)
    y = jnp.concatenate([ya.reshape(B, S, -1), yb.reshape(B, S, -1)], axis=-1) @ w_o
    if not ctx_out:
        return y, None
    C = hc.shape[1]
    yac = ctx_gqa(aqc, akc, avc, sink)
    ybc = dense_attn(bqc, bkc, bvc)
    yc = jnp.concatenate([yac.reshape(B, C, -1), ybc.reshape(B, C, -1)], axis=-1) @ w_o
    return y, yc


def fourier_mix(h, w_fo, b_fo):
    B, S, D = h.shape
    hg = h.astype(F32).reshape(B, S, F_GROUPS, F_GROUP_DIM)
    f = jnp.fft.fft2(hg, axes=(1, 3), norm='ortho').real.astype(h.dtype).reshape(B, S, D)
    return f @ w_fo + b_fo


def moe_ffn(h, w_router, b_router, w_gate, w_up, w_down):
    T, D = h.shape
    aff = jax.nn.sigmoid(h.astype(F32) @ w_router.astype(F32))
    sel = aff + b_router.astype(F32)
    grp_score = lax.top_k(sel.reshape(T, N_GROUPS, EXP_PER_GROUP), 2)[0].sum(-1)
    grp = jnp.argmax(grp_score, axis=-1)
    in_grp = (jnp.arange(N_EXPERTS) // EXP_PER_GROUP)[None, :] == grp[:, None]
    _, idx = lax.top_k(jnp.where(in_grp, sel, -jnp.inf), TOP_K)
    wts = jnp.take_along_axis(aff, idx, axis=-1)
    wts = wts / jnp.sum(wts, axis=-1, keepdims=True)
    A = T * TOP_K
    e_flat = idx.reshape(-1)
    tok_flat = jnp.arange(A) // TOP_K
    w_flat = wts.reshape(-1)
    order = jnp.argsort(e_flat)
    e_sorted = e_flat[order]
    counts = jnp.bincount(e_flat, length=N_EXPERTS)
    padded = (counts + MOE_BLK - 1) // MOE_BLK * MOE_BLK
    pad_end = jnp.cumsum(padded)
    pad_start = pad_end - padded
    start = jnp.cumsum(counts) - counts
    dest = pad_start[e_sorted] + jnp.arange(A) - start[e_sorted]
    n_blocks = -(-A // MOE_BLK) + N_EXPERTS
    buf_tok = jnp.full((n_blocks * MOE_BLK,), T, jnp.int32).at[dest].set(tok_flat[order])
    buf_w = jnp.zeros((n_blocks * MOE_BLK,), F32).at[dest].set(w_flat[order])
    blk_e = jnp.minimum(jnp.searchsorted(pad_end, jnp.arange(n_blocks) * MOE_BLK, side='right'), N_EXPERTS - 1)
    h_pad = jnp.concatenate([h, jnp.zeros((1, D), h.dtype)], axis=0)
    xb = h_pad[buf_tok].reshape(n_blocks, MOE_BLK, D)

    def expert(args):
        xe, e = args
        return (jax.nn.silu(xe @ w_gate[e]) * (xe @ w_up[e])) @ w_down[e]

    yb = lax.map(expert, (xb, blk_e)).reshape(-1, D)
    y = jnp.zeros((T + 1, D), h.dtype).at[buf_tok].add(yb * buf_w[:, None].astype(yb.dtype))
    return y[:T]


def setup_inputs(seed: int = 0) -> dict:
    key = jax.random.key(seed)
    ks = jax.random.split(key, 25)
    n_attn = (DEPTH + 1) // 2
    n_four = DEPTH // 2
    D = D_MODEL

    def nrm(k, shape, fan_in, gain=1.0):
        return jax.random.normal(k, shape, F32) * (gain * fan_in ** -0.5)

    def gain(k, shape):
        return 1.0 + 0.05 * jax.random.normal(k, shape, F32)

    return {
        'x': jax.random.normal(ks[0], (BATCH, SEQ, D), F32),
        'c': jax.random.normal(ks[1], (BATCH, D), F32),
        'ctx': jax.random.normal(ks[2], (BATCH, CTX_LEN, D), F32),
        'c_ctx': jax.random.normal(ks[3], (D,), F32),
        'w_ada': nrm(ks[4], (DEPTH, D, 6 * D), D, 0.5),
        'b_ada': 0.02 * jax.random.normal(ks[5], (DEPTH, 6 * D), F32),
        'g_norm': gain(ks[6], (DEPTH, 2, D)),
        'w_in': nrm(ks[7], (n_attn, D, IN_WIDTH), D),
        'g_aqn': gain(ks[8], (n_attn, HEAD_DIM)),
        'g_akn': gain(ks[9], (n_attn, HEAD_DIM)),
        'g_bq_lat': gain(ks[10], (n_attn, Q_LORA)),
        'w_bq_up': nrm(ks[11], (n_attn, Q_LORA, B_HEADS * QK_DIM), Q_LORA),
        'g_bkv_lat': gain(ks[12], (n_attn, KV_LORA)),
        'w_bkv_up': nrm(ks[13], (n_attn, KV_LORA, B_HEADS * (NOPE_DIM + V_DIM)), KV_LORA),
        'g_bqn': gain(ks[14], (n_attn, QK_DIM)),
        'g_bkn': gain(ks[15], (n_attn, QK_DIM)),
        'sink': 0.5 * jax.random.normal(ks[16], (n_attn, A_HEADS), F32),
        'w_o_ab': nrm(ks[17], (n_attn, MIX_WIDTH, D), MIX_WIDTH),
        'w_fo': nrm(ks[18], (n_four, D, D), D),
        'b_fo': 0.02 * jax.random.normal(ks[19], (n_four, D), F32),
        'w_router': nrm(ks[20], (D, N_EXPERTS), D),
        'b_router': 0.01 * jax.random.normal(ks[21], (N_EXPERTS,), F32),
        'w_gate': nrm(ks[22], (DEPTH, N_EXPERTS, D, D_EXPERT), D),
        'w_up': nrm(ks[23], (DEPTH, N_EXPERTS, D, D_EXPERT), D),
        'w_down': nrm(ks[24], (DEPTH, N_EXPERTS, D_EXPERT, D), D_EXPERT),
    }


def reference(x, c, ctx, c_ctx, w_ada, b_ada, g_norm, w_in, g_aqn, g_akn, g_bq_lat, w_bq_up, g_bkv_lat, w_bkv_up, g_bqn, g_bkn, sink, w_o_ab, w_fo, b_fo, w_router, b_router, w_gate, w_up, w_down):
    B, S, D = x.shape
    rows = S // GRID_W
    row = jnp.repeat(jnp.arange(rows, dtype=F32), GRID_W)
    col = jnp.tile(jnp.arange(GRID_W, dtype=F32), rows)
    xc = ctx
    for l in range(DEPTH):
        is_attn = l % 2 == 0
        i = l // 2
        ctx_out = any(j % 2 == 0 for j in range(l + 1, DEPTH))
        mod = jnp.split(jax.nn.silu(c) @ w_ada[l] + b_ada[l], 6, axis=-1)
        sh_a, sc_a, gt_a, sh_m, sc_m, gt_m = [m[:, None, :] for m in mod]
        h = modulate(rmsnorm(x, g_norm[l, 0]), sh_a, sc_a)
        if is_attn or ctx_out:
            sh_ac, sc_ac, gt_ac, sh_mc, sc_mc, gt_mc = jnp.split(jax.nn.silu(c_ctx) @ w_ada[l] + b_ada[l], 6, axis=-1)
            hc = modulate(rmsnorm(xc, g_norm[l, 0]), sh_ac, sc_ac)
        if is_attn:
            y, yc = mixer_ab(h, hc, row, col, w_in[i], g_aqn[i], g_akn[i], g_bq_lat[i], w_bq_up[i], g_bkv_lat[i], w_bkv_up[i], g_bqn[i], g_bkn[i], sink[i], w_o_ab[i], ctx_out)
        else:
            y = fourier_mix(h, w_fo[i], b_fo[i])
            yc = fourier_mix(hc, w_fo[i], b_fo[i]) if ctx_out else None
        x = x + gt_a * y
        h2 = modulate(rmsnorm(x, g_norm[l, 1]), sh_m, sc_m)
        x = x + gt_m * moe_ffn(h2.reshape(B * S, D), w_router, b_router, w_gate[l], w_up[l], w_down[l]).reshape(B, S, D)
        if ctx_out:
            xc = xc + gt_ac * yc
            h2c = modulate(rmsnorm(xc, g_norm[l, 1]), sh_mc, sc_mc)
            xc = xc + gt_mc * moe_ffn(h2c.reshape(-1, D), w_router, b_router, w_gate[l], w_up[l], w_down[l]).reshape(xc.shape)
    return x
```

```python
import functools
import math

import numpy as np
import jax
import jax.numpy as jnp
from jax import lax
from jax.experimental import pallas as pl
from jax.experimental.pallas import tpu as pltpu

F32 = jnp.float32
BF16 = jnp.bfloat16
I32 = jnp.int32
HIGHEST = lax.Precision.HIGHEST

D_MODEL = 2048
DEPTH = 2
GRID_W = 64
HEAD_DIM = 128
A_HEADS = 8
A_KV_HEADS = 2
A_GROUP = A_HEADS // A_KV_HEADS
WINDOW = 128
WBLK = 128
B_HEADS = 8
Q_LORA = 512
KV_LORA = 256
NOPE_DIM = 128
ROPE_DIM = 64
V_DIM = 128
QK_DIM = NOPE_DIM + ROPE_DIM
QK_PAD = 256
IN_SPLITS = (A_HEADS * HEAD_DIM, A_KV_HEADS * HEAD_DIM, A_KV_HEADS * HEAD_DIM, Q_LORA, KV_LORA, ROPE_DIM)
IN_WIDTH = sum(IN_SPLITS)
IN_PAD = 2432
F_GROUPS = 8
F_GROUP_DIM = D_MODEL // F_GROUPS
N_EXPERTS = 16
N_GROUPS = 4
EXP_PER_GROUP = N_EXPERTS // N_GROUPS
TOP_K = 2
D_EXPERT = 1024
ROPE_BASE = 10000.0
EPS = 1e-6
LOG2E = math.log2(math.e)
LANES = 128
HALF_D = D_MODEL // 2

MOE_BLK = 512
DFT_N1 = 16
DFT_N2 = 256
VMEM_LIMIT = 56 * 1024 * 1024


def _cparams(sem, **kw):
    return pltpu.CompilerParams(dimension_semantics=sem, vmem_limit_bytes=VMEM_LIMIT, **kw)


def _resident(shape):
    nd = len(shape)
    return pl.BlockSpec(shape, lambda *_: (0,) * nd, pipeline_mode=pl.Buffered(1))


def _rms(t, width=None):
    n = t.shape[-1] if width is None else width
    ss = jnp.sum(t * t, axis=-1, keepdims=True)
    return t * lax.rsqrt(ss * (1.0 / n) + EPS)


def _pack_halves(t):
    k = t.shape[-1] // 2
    lo = lax.bitcast_convert_type(t[:, :k].astype(BF16).astype(F32), I32)
    hi = lax.bitcast_convert_type(t[:, k:].astype(BF16).astype(F32), I32)
    return hi | lax.shift_right_logical(lo, 16)


def _unpack_lo(p):
    return lax.bitcast_convert_type(lax.shift_left(p, 16), F32)


def _unpack_hi(p):
    return lax.bitcast_convert_type(p & jnp.int32(-65536), F32)


def _rope(t, cos, sneg, spos, dist):
    n = t.shape[-1]
    return t * cos + pltpu.roll(t, n - dist, 1) * sneg + pltpu.roll(t, dist, 1) * spos


def _ada_kernel(c_ref, w_ref, b_ref, o_ref):
    c = c_ref[...]
    s = c * jax.nn.sigmoid(c)
    o_ref[0] = jnp.dot(s, w_ref[0], precision=HIGHEST, preferred_element_type=F32) + b_ref[0]


def _ada(crows, w_ada, b_ada):
    depth, d, n = w_ada.shape
    tn = 1024
    return pl.pallas_call(
        _ada_kernel,
        grid=(depth, n // tn),
        in_specs=[
            pl.BlockSpec((8, d), lambda l, j: (0, 0)),
            pl.BlockSpec((1, d, tn), lambda l, j: (l, 0, j)),
            pl.BlockSpec((1, 1, tn), lambda l, j: (l, 0, j)),
        ],
        out_specs=pl.BlockSpec((1, 8, tn), lambda l, j: (l, 0, j)),
        out_shape=jax.ShapeDtypeStruct((depth, 8, n), F32),
        compiler_params=_cparams(("arbitrary", "arbitrary")),
        name="ada",
    )(crows, w_ada, b_ada.reshape(depth, 1, n))


def _proj_kernel(x_ref, mod_ref, gn_ref, win_ref, wbq_ref, wbkv_ref, gaq_ref, gak_ref, gbql_ref, gbkvl_ref,
                 gbq_ref, gbk_ref, ca_ref, sna_ref, spa_ref, cb_ref, snb_ref, spb_ref,
                 aq_ref, ak_ref, av_ref, bq_ref, bk_ref, bv_ref):
    x = x_ref[0]
    shift = mod_ref[0, 0:1, :]
    scale = mod_ref[0, 1:2, :]
    h = (_rms(x) * gn_ref[...]) * (1.0 + scale) + shift
    p = jnp.dot(h.astype(BF16), win_ref[...], preferred_element_type=F32)

    ca, sna, spa = ca_ref[...], sna_ref[...], spa_ref[...]
    cb, snb, spb = cb_ref[...], snb_ref[...], spb_ref[...]
    a_scale = HEAD_DIM ** -0.5 * LOG2E
    b_scale = QK_DIM ** -0.5 * LOG2E

    for hd in range(A_HEADS):
        t = _rms(p[:, hd * HEAD_DIM:(hd + 1) * HEAD_DIM]) * gaq_ref[...]
        aq_ref[0, :, hd * HEAD_DIM:(hd + 1) * HEAD_DIM] = (_rope(t, ca, sna, spa, 32) * a_scale).astype(BF16)
    off = A_HEADS * HEAD_DIM
    for kh in range(A_KV_HEADS):
        t = _rms(p[:, off + kh * HEAD_DIM:off + (kh + 1) * HEAD_DIM]) * gak_ref[...]
        ak_ref[0, :, kh * HEAD_DIM:(kh + 1) * HEAD_DIM] = _rope(t, ca, sna, spa, 32).astype(BF16)
    off += A_KV_HEADS * HEAD_DIM
    av_ref[0] = p[:, off:off + A_KV_HEADS * HEAD_DIM].astype(BF16)
    off += A_KV_HEADS * HEAD_DIM

    ql = (_rms(p[:, off:off + Q_LORA]) * gbql_ref[...]).astype(BF16)
    off += Q_LORA
    qu = jnp.dot(ql, wbq_ref[...], preferred_element_type=F32)
    for hd in range(B_HEADS):
        t = _rms(qu[:, hd * QK_PAD:(hd + 1) * QK_PAD], QK_DIM) * gbq_ref[...]
        bq_ref[0, :, hd * QK_PAD:hd * QK_PAD + NOPE_DIM] = (t[:, :NOPE_DIM] * b_scale).astype(BF16)
        bq_ref[0, :, hd * QK_PAD + NOPE_DIM:(hd + 1) * QK_PAD] = (
            _rope(t[:, NOPE_DIM:], cb, snb, spb, 16) * b_scale).astype(BF16)

    kvl = (_rms(p[:, off:off + KV_LORA]) * gbkvl_ref[...]).astype(BF16)
    off += KV_LORA
    kvu = jnp.dot(kvl, wbkv_ref[...], preferred_element_type=F32)
    kr = p[:, off:off + LANES]
    kr_ss = jnp.sum(kr * kr, axis=-1, keepdims=True)
    kr_rot = _rope(kr * gbk_ref[:, NOPE_DIM:], cb, snb, spb, 16)
    for hd in range(B_HEADS):
        kn = kvu[:, hd * NOPE_DIM:(hd + 1) * NOPE_DIM]
        ss = jnp.sum(kn * kn, axis=-1, keepdims=True) + kr_ss
        r = lax.rsqrt(ss * (1.0 / QK_DIM) + EPS)
        bk_ref[0, :, hd * QK_PAD:hd * QK_PAD + NOPE_DIM] = (kn * r * gbk_ref[:, :NOPE_DIM]).astype(BF16)
        bk_ref[0, :, hd * QK_PAD + NOPE_DIM:(hd + 1) * QK_PAD] = (kr_rot * r).astype(BF16)
    bv_ref[0] = kvu[:, B_HEADS * NOPE_DIM:].astype(BF16)


def _proj(x, mod, gn, win, wbq, wbkv, gaq, gak, gbql, gbkvl, gbq, gbk, tabs, tm):
    b, s, d = x.shape
    row = lambda w: pl.BlockSpec((1, tm, w), lambda i, j: (i, j, 0))
    tab = pl.BlockSpec((tm, LANES), lambda i, j: (j, 0))
    widths = (A_HEADS * HEAD_DIM, A_KV_HEADS * HEAD_DIM, A_KV_HEADS * HEAD_DIM,
              B_HEADS * QK_PAD, B_HEADS * QK_PAD, B_HEADS * V_DIM)
    return pl.pallas_call(
        _proj_kernel,
        grid=(b, s // tm),
        in_specs=[row(d), pl.BlockSpec((1, 6, d), lambda i, j: (i, 0, 0)), _resident(gn.shape),
                  _resident(win.shape), _resident(wbq.shape), _resident(wbkv.shape),
                  _resident(gaq.shape), _resident(gak.shape), _resident(gbql.shape), _resident(gbkvl.shape),
                  _resident(gbq.shape), _resident(gbk.shape)] + [tab] * 6,
        out_specs=[row(w) for w in widths],
        out_shape=[jax.ShapeDtypeStruct((b, s, w), BF16) for w in widths],
        compiler_params=_cparams(("arbitrary", "arbitrary")),
        name="proj",
    )(x, mod, gn, win, wbq, wbkv, gaq, gak, gbql, gbkvl, gbq, gbk, *tabs)


def _win_kernel(sink_ref, q_ref, kp_ref, kc_ref, kn_ref, vp_ref, vc_ref, vn_ref, kx_ref, vx_ref, o_ref, *, seq):
    n = pl.program_id(1)
    rows = A_GROUP * WBLK
    band = 3 * WBLK
    ctx = kx_ref.shape[1]
    r_iota = lax.broadcasted_iota(I32, (rows, band), 0)
    c_iota = lax.broadcasted_iota(I32, (rows, band), 1)
    qpos = n * WBLK + (r_iota & (WBLK - 1))
    kpos = (n - 1) * WBLK + c_iota
    valid = (jnp.abs(qpos - kpos) <= WINDOW) & (kpos >= 0) & (kpos < seq)
    head_of_row = lax.broadcasted_iota(I32, (rows, 1), 0) // WBLK
    dn = (((1,), (1,)), ((), ()))
    for kh in range(A_KV_HEADS):
        cs = slice(kh * HEAD_DIM, (kh + 1) * HEAD_DIM)
        q = jnp.concatenate([q_ref[0, :, (kh * A_GROUP + g) * HEAD_DIM:(kh * A_GROUP + g + 1) * HEAD_DIM]
                             for g in range(A_GROUP)], axis=0)
        kb = jnp.concatenate([kp_ref[0, :, cs], kc_ref[0, :, cs], kn_ref[0, :, cs]], axis=0)
        vb = jnp.concatenate([vp_ref[0, :, cs], vc_ref[0, :, cs], vn_ref[0, :, cs]], axis=0)
        s_loc = jnp.where(valid, lax.dot_general(q, kb, dn, preferred_element_type=F32), -jnp.inf)
        s_ctx = lax.dot_general(q, kx_ref[0, :, cs], dn, preferred_element_type=F32)
        sink = jnp.zeros((rows, 1), F32)
        for g in range(A_GROUP):
            sink = jnp.where(head_of_row == g, sink_ref[kh * A_GROUP + g] * LOG2E, sink)
        m = jnp.maximum(jnp.maximum(jnp.max(s_loc, axis=-1, keepdims=True),
                                    jnp.max(s_ctx, axis=-1, keepdims=True)), sink)
        p_loc = jnp.exp2(s_loc - m)
        p_ctx = jnp.exp2(s_ctx - m)
        den = (jnp.sum(p_loc, axis=-1, keepdims=True) + jnp.sum(p_ctx, axis=-1, keepdims=True)
               + jnp.exp2(sink - m))
        o = (jnp.dot(p_loc.astype(BF16), vb, preferred_element_type=F32)
             + jnp.dot(p_ctx.astype(BF16), vx_ref[0, :, cs], preferred_element_type=F32)) / den
        for g in range(A_GROUP):
            hd = kh * A_GROUP + g
            o_ref[0, :, hd * HEAD_DIM:(hd + 1) * HEAD_DIM] = o[g * WBLK:(g + 1) * WBLK].astype(BF16)


def _win_attn(sink, aq, ak, av, akc, avc):
    b, s, _ = aq.shape
    nb = s // WBLK
    c = akc.shape[1]
    kvw = A_KV_HEADS * HEAD_DIM
    prev = pl.BlockSpec((1, WBLK, kvw), lambda i, j: (i, jnp.maximum(j - 1, 0), 0))
    cur = pl.BlockSpec((1, WBLK, kvw), lambda i, j: (i, j, 0))
    nxt = pl.BlockSpec((1, WBLK, kvw), lambda i, j: (i, jnp.minimum(j + 1, nb - 1), 0))
    cx = pl.BlockSpec((1, c, kvw), lambda i, j: (i, 0, 0))
    qo = pl.BlockSpec((1, WBLK, A_HEADS * HEAD_DIM), lambda i, j: (i, j, 0))
    return pl.pallas_call(
        functools.partial(_win_kernel, seq=s),
        grid=(b, nb),
        in_specs=[pl.BlockSpec(memory_space=pltpu.SMEM), qo, prev, cur, nxt, prev, cur, nxt, cx, cx],
        out_specs=qo,
        out_shape=jax.ShapeDtypeStruct(aq.shape, BF16),
        compiler_params=_cparams(("arbitrary", "arbitrary")),
        name="win_attn",
    )(sink, aq, ak, ak, ak, av, av, av, akc, avc)


def _mla_kernel(q_ref, k_ref, v_ref, kx_ref, vx_ref, o_ref, *, tk):
    q = q_ref[0]
    dn = (((1,), (1,)), ((), ()))
    s0 = lax.dot_general(q, kx_ref[0], dn, preferred_element_type=F32)
    m0 = jnp.max(s0, axis=-1, keepdims=True)
    p0 = jnp.exp2(s0 - m0)
    l0 = jnp.sum(p0, axis=-1, keepdims=True)
    acc0 = jnp.dot(p0.astype(BF16), vx_ref[0], preferred_element_type=F32)

    def step(c, carry):
        m, l, acc = carry
        ks = pl.ds(pl.multiple_of(c * tk, tk), tk)
        s = lax.dot_general(q, k_ref[0, ks, :], dn, preferred_element_type=F32)
        m_new = jnp.maximum(m, jnp.max(s, axis=-1, keepdims=True))
        alpha = jnp.exp2(m - m_new)
        p = jnp.exp2(s - m_new)
        l = alpha * l + jnp.sum(p, axis=-1, keepdims=True)
        acc = alpha * acc + jnp.dot(p.astype(BF16), v_ref[0, ks, :], preferred_element_type=F32)
        return m_new, l, acc

    _, l, acc = lax.fori_loop(0, k_ref.shape[1] // tk, step, (m0, l0, acc0))
    o_ref[0] = (acc / l).astype(BF16)


def _mla_attn(bq, bk, bv, bkc, bvc, tq, tk):
    b, s, _ = bq.shape
    c = bkc.shape[1]
    return pl.pallas_call(
        functools.partial(_mla_kernel, tk=tk),
        grid=(b, B_HEADS, s // tq),
        in_specs=[
            pl.BlockSpec((1, tq, QK_PAD), lambda i, h, j: (i, j, h)),
            pl.BlockSpec((1, s, QK_PAD), lambda i, h, j: (i, 0, h)),
            pl.BlockSpec((1, s, V_DIM), lambda i, h, j: (i, 0, h)),
            pl.BlockSpec((1, c, QK_PAD), lambda i, h, j: (i, 0, h)),
            pl.BlockSpec((1, c, V_DIM), lambda i, h, j: (i, 0, h)),
        ],
        out_specs=pl.BlockSpec((1, tq, V_DIM), lambda i, h, j: (i, j, h)),
        out_shape=jax.ShapeDtypeStruct((b, s, B_HEADS * V_DIM), BF16),
        compiler_params=_cparams(("arbitrary", "arbitrary", "arbitrary")),
        name="mla_attn",
    )(bq, bk, bv, bkc, bvc)


def _route(sel, aff):
    scores = []
    for g in range(N_GROUPS):
        r = sel[g * EXP_PER_GROUP:(g + 1) * EXP_PER_GROUP]
        best = None
        for a in range(EXP_PER_GROUP):
            for b in range(a + 1, EXP_PER_GROUP):
                pair = r[a] + r[b]
                best = pair if best is None else jnp.maximum(best, pair)
        scores.append(best)
    top, grp = scores[0], jnp.zeros_like(scores[0], dtype=I32)
    for g in range(1, N_GROUPS):
        take = scores[g] > top
        grp = jnp.where(take, g, grp)
        top = jnp.where(take, scores[g], top)
    masked = [jnp.where(grp == e // EXP_PER_GROUP, sel[e], -jnp.inf) for e in range(N_EXPERTS)]

    def argmax_first(vals, skip=None):
        bv = jnp.full_like(vals[0], -jnp.inf)
        bi = jnp.full_like(grp, -1)
        for e in range(N_EXPERTS):
            take = vals[e] > bv
            if skip is not None:
                take = take & (skip != e)
            bi = jnp.where(take, e, bi)
            bv = jnp.where(take, vals[e], bv)
        return bi

    i0 = argmax_first(masked)
    i1 = argmax_first(masked, skip=i0)
    a0 = jnp.zeros_like(aff[0])
    a1 = jnp.zeros_like(aff[0])
    for e in range(N_EXPERTS):
        a0 = jnp.where(i0 == e, aff[e], a0)
        a1 = jnp.where(i1 == e, aff[e], a1)
    tot = a0 + a1
    return i0, i1, a0 / tot, a1 / tot


def _out_kernel(*refs, n_lhs):
    lhs = refs[:n_lhs]
    ws = refs[n_lhs:2 * n_lhs]
    bias_ref, x_ref, mod_ref, gn_ref, wr_ref, br_ref, x1_ref, hp_ref, idx_ref, wts_ref = refs[2 * n_lhs:]
    y = bias_ref[...]
    for a, w in zip(lhs, ws):
        y = y + jnp.dot(a[0], w[...], preferred_element_type=F32)
    x1 = x_ref[0] + mod_ref[0, 2:3, :] * y
    x1_ref[0] = x1
    h2 = (_rms(x1) * gn_ref[...]) * (1.0 + mod_ref[0, 4:5, :]) + mod_ref[0, 3:4, :]
    hp_ref[0] = _pack_halves(h2)
    logits = jnp.dot(h2, wr_ref[...], precision=HIGHEST, preferred_element_type=F32)
    lt = logits.T[:N_EXPERTS]
    aff_t = jax.nn.sigmoid(lt)
    sel_t = aff_t + br_ref[...]
    sel = [sel_t[e:e + 1] for e in range(N_EXPERTS)]
    aff = [aff_t[e:e + 1] for e in range(N_EXPERTS)]
    i0, i1, w0, w1 = _route(sel, aff)
    idx_ref[0] = jnp.concatenate([i0, i1], axis=0)
    wts_ref[0] = jnp.concatenate([w0, w1], axis=0)


def _out_proj(lhs, ws, bias, x, mod, gn, wr, br, tm, permuted_x):
    b, s, d = x.shape
    n_lhs = len(lhs)
    nt = s // tm
    if permuted_x:
        assert tm == s // DFT_N1
        x_in = x.reshape(b, tm, DFT_N1 * d)
        x_spec = pl.BlockSpec((1, tm, d), lambda i, j: (i, 0, j))
    else:
        x_in = x
        x_spec = pl.BlockSpec((1, tm, d), lambda i, j: (i, j, 0))
    in_specs = ([pl.BlockSpec((1, tm, a.shape[-1]), lambda i, j: (i, j, 0)) for a in lhs]
                + [_resident(w.shape) for w in ws]
                + [_resident(bias.shape), x_spec, pl.BlockSpec((1, 6, d), lambda i, j: (i, 0, 0)),
                   _resident(gn.shape), _resident(wr.shape), _resident(br.shape)])
    return pl.pallas_call(
        functools.partial(_out_kernel, n_lhs=n_lhs),
        grid=(b, nt),
        in_specs=in_specs,
        out_specs=[pl.BlockSpec((1, tm, d), lambda i, j: (i, j, 0)),
                   pl.BlockSpec((1, tm, HALF_D), lambda i, j: (i, j, 0)),
                   pl.BlockSpec((1, TOP_K, tm), lambda i, j: (i, 0, j)),
                   pl.BlockSpec((1, TOP_K, tm), lambda i, j: (i, 0, j))],
        out_shape=[jax.ShapeDtypeStruct((b, s, d), F32), jax.ShapeDtypeStruct((b, s, HALF_D), I32),
                   jax.ShapeDtypeStruct((b, TOP_K, s), I32), jax.ShapeDtypeStruct((b, TOP_K, s), F32)],
        compiler_params=_cparams(("arbitrary", "arbitrary")),
        name="out_proj",
    )(*lhs, *ws, bias, x_in, mod, gn, wr, br)


def _dispatch(idx, blk):
    b, _, s = idx.shape
    t = b * s
    e = jnp.transpose(idx, (0, 2, 1)).reshape(t * TOP_K)
    onehot = (e[:, None] == jnp.arange(N_EXPERTS, dtype=I32)[None, :]).astype(I32)
    csum = jnp.cumsum(onehot, axis=0)
    counts = csum[-1]
    rank = jnp.sum((csum - onehot) * onehot, axis=1)
    padded = (counts + blk - 1) // blk * blk
    pad_end = jnp.cumsum(padded)
    pad_start = pad_end - padded
    dest = pad_start[e] + rank
    n_blocks = -(-t * TOP_K // blk) + N_EXPERTS
    tok = jnp.arange(t * TOP_K, dtype=I32) // TOP_K
    src_tok = jnp.zeros((n_blocks * blk,), I32).at[dest].set(tok)
    n_used = pad_end[-1] // blk
    blk_ids = jnp.minimum(jnp.arange(n_blocks, dtype=I32), n_used - 1)
    blk_e = jnp.sum((blk_ids[:, None] * blk >= pad_end[None, :]).astype(I32), axis=1)
    blk_e = jnp.minimum(blk_e, N_EXPERTS - 1)
    return dest.reshape(t, TOP_K).astype(I32), src_tok, blk_e, n_used.reshape(1).astype(I32)


def _moe_kernel(blk_e_ref, n_used_ref, tok_ref, hp_ref, wg_ref, wu_ref, wd_ref, o_ref, xbuf, sem, *, blk):
    i = pl.program_id(0)
    n_used = n_used_ref[0]
    slot = i % 2

    def gather(block, slot_):
        def body(r, _):
            t = tok_ref[block * blk + r]
            pltpu.make_async_copy(hp_ref.at[pl.ds(t, 1)], xbuf.at[slot_, pl.ds(r, 1)], sem.at[slot_]).start()
            return 0
        lax.fori_loop(0, blk, body, 0, unroll=8)

    @pl.when(i == 0)
    def _():
        gather(0, 0)

    @pl.when(i + 1 < n_used)
    def _():
        gather(i + 1, 1 - slot)

    @pl.when(i < n_used)
    def _():
        pltpu.make_async_copy(xbuf.at[slot], xbuf.at[slot], sem.at[slot]).wait()
        xp = xbuf[slot]
        xl = _unpack_lo(xp).astype(BF16)
        xh = _unpack_hi(xp).astype(BF16)
        g = (jnp.dot(xl, wg_ref[0, :HALF_D, :], preferred_element_type=F32)
             + jnp.dot(xh, wg_ref[0, HALF_D:, :], preferred_element_type=F32))
        u = (jnp.dot(xl, wu_ref[0, :HALF_D, :], preferred_element_type=F32)
             + jnp.dot(xh, wu_ref[0, HALF_D:, :], preferred_element_type=F32))
        a = (g * jax.nn.sigmoid(g) * u).astype(BF16)
        o_ref[...] = _pack_halves(jnp.dot(a, wd_ref[0], preferred_element_type=F32))

    @pl.when(i >= n_used)
    def _():
        o_ref[...] = jnp.zeros_like(o_ref)


def _moe(hp, src_tok, blk_e, n_used, wg, wu, wd, blk):
    t = hp.shape[0]
    n_blocks = blk_e.shape[0]
    _, d, de = wg.shape
    grid_spec = pltpu.PrefetchScalarGridSpec(
        num_scalar_prefetch=3,
        grid=(n_blocks,),
        in_specs=[
            pl.BlockSpec(memory_space=pl.ANY),
            pl.BlockSpec((1, d, de), lambda i, be, nu, tk: (be[i], 0, 0)),
            pl.BlockSpec((1, d, de), lambda i, be, nu, tk: (be[i], 0, 0)),
            pl.BlockSpec((1, de, d), lambda i, be, nu, tk: (be[i], 0, 0)),
        ],
        out_specs=pl.BlockSpec((blk, HALF_D), lambda i, be, nu, tk: (i, 0)),
        scratch_shapes=[pltpu.VMEM((2, blk, HALF_D), I32), pltpu.SemaphoreType.DMA((2,))],
    )
    return pl.pallas_call(
        functools.partial(_moe_kernel, blk=blk),
        grid_spec=grid_spec,
        out_shape=jax.ShapeDtypeStruct((n_blocks * blk, HALF_D), I32),
        compiler_params=_cparams(("arbitrary",)),
        name="moe_ffn",
    )(blk_e, n_used, src_tok, hp, wg, wu, wd)


def _combine_kernel(dest_ref, yp_ref, x_ref, wt_ref, mod_ref, *rest, tm, nt, with_norm):
    if with_norm:
        gn_ref, modn_ref, o_ref, hn_ref, buf, sem = rest
    else:
        o_ref, buf, sem = rest
    i = pl.program_id(0)
    j = pl.program_id(1)
    step = i * nt + j
    slot = step % 2

    def gather(step_, slot_):
        def body(r, _):
            base = (step_ * tm + r) * TOP_K
            for k in range(TOP_K):
                pltpu.make_async_copy(yp_ref.at[pl.ds(dest_ref[base + k], 1)],
                                      buf.at[slot_, k, pl.ds(r, 1)], sem.at[slot_]).start()
            return 0
        lax.fori_loop(0, tm, body, 0, unroll=8)

    @pl.when(step == 0)
    def _():
        gather(0, 0)

    @pl.when(step + 1 < pl.num_programs(0) * nt)
    def _():
        gather(step + 1, 1 - slot)

    pltpu.make_async_copy(buf.at[slot], buf.at[slot], sem.at[slot]).wait()
    w = wt_ref[0]
    y_lo = _unpack_lo(buf[slot, 0]) * w[:, 0:1] + _unpack_lo(buf[slot, 1]) * w[:, 1:2]
    y_hi = _unpack_hi(buf[slot, 0]) * w[:, 0:1] + _unpack_hi(buf[slot, 1]) * w[:, 1:2]
    y = jnp.concatenate([y_lo, y_hi], axis=-1)
    out = x_ref[0] + mod_ref[0, 5:6, :] * y
    o_ref[0] = out
    if with_norm:
        hn_ref[0] = ((_rms(out) * gn_ref[...]) * (1.0 + modn_ref[0, 1:2, :]) + modn_ref[0, 0:1, :]).astype(BF16)


def _combine(dest, yp, x1, wts, mod, tm, norm=None, permuted_out=False):
    b, s, d = x1.shape
    nt = s // tm
    with_norm = norm is not None
    row = pl.BlockSpec((1, tm, d), lambda i, j, ds: (i, j, 0))
    modspec = pl.BlockSpec((1, 6, d), lambda i, j, ds: (i, 0, 0))
    in_specs = [pl.BlockSpec(memory_space=pl.ANY), row,
                pl.BlockSpec((1, tm, TOP_K), lambda i, j, ds: (i, j, 0)), modspec]
    args = [yp, x1, wts, mod]
    out_shape = [jax.ShapeDtypeStruct((b, s, d), F32)]
    if permuted_out:
        assert tm == s // DFT_N1
        out_shape = [jax.ShapeDtypeStruct((b, tm, DFT_N1 * d), F32)]
        out_specs = [pl.BlockSpec((1, tm, d), lambda i, j, ds: (i, 0, j))]
    else:
        out_specs = [row]
    if with_norm:
        gn, modn = norm
        in_specs += [pl.BlockSpec(gn.shape, lambda i, j, ds: (0, 0)), modspec]
        args += [gn, modn]
        out_shape.append(jax.ShapeDtypeStruct((b, s, d), BF16))
        out_specs.append(row)
    grid_spec = pltpu.PrefetchScalarGridSpec(
        num_scalar_prefetch=1, grid=(b, nt), in_specs=in_specs, out_specs=out_specs,
        scratch_shapes=[pltpu.VMEM((2, TOP_K, tm, HALF_D), I32), pltpu.SemaphoreType.DMA((2,))])
    res = list(pl.pallas_call(
        functools.partial(_combine_kernel, tm=tm, nt=nt, with_norm=with_norm),
        grid_spec=grid_spec,
        out_shape=out_shape,
        compiler_params=_cparams(("arbitrary", "arbitrary")),
        name="combine",
    )(dest.reshape(-1), *args))
    if permuted_out:
        res[0] = res[0].reshape(b, s, d)
    return res


def _dft_constants(seq):
    n1, n2 = DFT_N1, DFT_N2
    assert seq == n1 * n2 and n2 == n1 * n1
    c = np.arange(F_GROUP_DIM)
    ang = 2 * np.pi * np.outer(c, c) / F_GROUP_DIM
    fc = np.concatenate([np.cos(ang), -np.sin(ang)], axis=1)
    hi, k1, nn, lo = np.meshgrid(np.arange(n1), np.arange(n1), np.arange(n1), np.arange(n1), indexing="ij")
    ang1 = -2 * np.pi * (k1 * (n2 * nn + n1 * hi + lo) % seq) / seq
    eye = np.eye(n1)
    are = np.einsum("hkna,ab->hkanb", np.cos(ang1), eye).reshape(n1, n2, n2)
    aim = np.einsum("hkna,ab->hkanb", np.sin(ang1), eye).reshape(n1, n2, n2)
    m1 = np.concatenate([np.concatenate([are, -aim], axis=2), np.concatenate([aim, are], axis=2)], axis=1)
    k = np.arange(n2)
    ang2 = 2 * np.pi * np.outer(k, k) / n2
    m2 = np.concatenate([np.cos(ang2), np.sin(ang2)], axis=1) / math.sqrt(seq * F_GROUP_DIM)
    return tuple(jnp.asarray(m, F32).astype(BF16) for m in (fc, m1, m2))


def _fourier_kernel(h_ref, fc_ref, m1_ref, m2_ref, o_ref, zre, zim, yre, yim):
    n1, n2, gd = DFT_N1, DFT_N2, F_GROUP_DIM
    rows = 512
    for r in range(0, h_ref.shape[1], rows):
        z = jnp.dot(h_ref[0, r:r + rows, :], fc_ref[...], preferred_element_type=F32)
        zre[r:r + rows, :] = z[:, :gd].astype(BF16)
        zim[r:r + rows, :] = z[:, gd:].astype(BF16)
    for hi in range(n1):
        rhs = jnp.concatenate([zre[a * n2 + hi * n1:a * n2 + (hi + 1) * n1, :] for a in range(n1)]
                              + [zim[a * n2 + hi * n1:a * n2 + (hi + 1) * n1, :] for a in range(n1)], axis=0)
        y = jnp.dot(m1_ref[hi], rhs, preferred_element_type=F32).astype(BF16)
        for k1 in range(n1):
            yre[k1, hi * n1:(hi + 1) * n1, :] = y[k1 * n1:(k1 + 1) * n1]
            yim[k1, hi * n1:(hi + 1) * n1, :] = y[n2 + k1 * n1:n2 + (k1 + 1) * n1]
    for k1 in range(n1):
        rhs = jnp.concatenate([yre[k1], yim[k1]], axis=0)
        o_ref[0, k1 * n2:(k1 + 1) * n2, :] = jnp.dot(m2_ref[...], rhs, preferred_element_type=F32).astype(BF16)


def _fourier(hn, consts):
    b, s, d = hn.shape
    fc, m1, m2 = consts
    gd = F_GROUP_DIM
    return pl.pallas_call(
        _fourier_kernel,
        grid=(b, F_GROUPS),
        in_specs=[pl.BlockSpec((1, s, gd), lambda i, g: (i, 0, g)),
                  _resident(fc.shape), _resident(m1.shape), _resident(m2.shape)],
        out_specs=pl.BlockSpec((1, s, gd), lambda i, g: (i, 0, g)),
        out_shape=jax.ShapeDtypeStruct((b, s, d), BF16),
        scratch_shapes=[pltpu.VMEM((s, gd), BF16), pltpu.VMEM((s, gd), BF16),
                        pltpu.VMEM((DFT_N1, DFT_N2, gd), BF16), pltpu.VMEM((DFT_N1, DFT_N2, gd), BF16)],
        compiler_params=_cparams(("arbitrary", "arbitrary")),
        name="fourier",
    )(hn, fc, m1, m2)


def _rope_tables(seq):
    pos = np.arange(seq)
    row, col = (pos // GRID_W).astype(np.float64), (pos % GRID_W).astype(np.float64)

    def tables(width):
        half = width // 2
        quarter = half // 2
        freqs = ROPE_BASE ** (-np.arange(0, half, 2, dtype=np.float64) / half)
        lane = np.arange(LANES)
        ang = np.where((lane < half)[None, :], row[:, None], col[:, None]) * freqs[lane % quarter][None, :]
        live = (lane < width)[None, :]
        first = ((lane % half) < quarter)[None, :]
        cos = np.where(live, np.cos(ang), 1.0)
        sneg = np.where(live & first, -np.sin(ang), 0.0)
        spos = np.where(live & ~first, np.sin(ang), 0.0)
        return [jnp.asarray(t, F32) for t in (cos, sneg, spos)]

    return tables(HEAD_DIM) + tables(ROPE_DIM)


def _identity_tables(n):
    one, zero = jnp.ones((n, LANES), F32), jnp.zeros((n, LANES), F32)
    return [one, zero, zero, one, zero, zero]


def _pad_heads(w, lead):
    w = w.reshape(lead, B_HEADS, QK_DIM)
    return jnp.pad(w, ((0, 0), (0, 0), (0, QK_PAD - QK_DIM))).reshape(lead, B_HEADS * QK_PAD)


def kernel(x, c, ctx, c_ctx, w_ada, b_ada, g_norm, w_in, g_aqn, g_akn, g_bq_lat, w_bq_up, g_bkv_lat, w_bkv_up,
           g_bqn, g_bkn, sink, w_o_ab, w_fo, b_fo, w_router, b_router, w_gate, w_up, w_down):
    b, s, d = x.shape
    n_ctx = ctx.shape[1]

    crows = jnp.concatenate([c, c_ctx[None, :], jnp.zeros((8 - b - 1, d), F32)], axis=0)
    mods = _ada(crows, w_ada, b_ada).reshape(DEPTH, 8, 6, d)
    mod_lat = [mods[l, :b] for l in range(DEPTH)]
    mod_ctx = jnp.broadcast_to(mods[0, b][None], (b, 6, d))

    wr = jnp.pad(w_router, ((0, 0), (0, LANES - N_EXPERTS)))
    br = b_router.reshape(N_EXPERTS, 1)

    win = jnp.pad(w_in[0], ((0, 0), (0, IN_PAD - IN_WIDTH))).astype(BF16)
    wbq = _pad_heads(w_bq_up[0], Q_LORA).astype(BF16)
    wkv = w_bkv_up[0].reshape(KV_LORA, B_HEADS, NOPE_DIM + V_DIM)
    wbkv = jnp.concatenate([wkv[:, :, :NOPE_DIM].reshape(KV_LORA, -1), wkv[:, :, NOPE_DIM:].reshape(KV_LORA, -1)],
                           axis=1).astype(BF16)
    gains = (g_aqn[0][None], g_akn[0][None], g_bq_lat[0][None], g_bkv_lat[0][None],
             jnp.pad(g_bqn[0][None], ((0, 0), (0, QK_PAD - QK_DIM))),
             jnp.pad(g_bkn[0][None], ((0, 0), (0, QK_PAD - QK_DIM))))
    gn0 = g_norm[0, 0][None]
    aq, ak, av, bq, bk, bv = _proj(x, mod_lat[0], gn0, win, wbq, wbkv, *gains, _rope_tables(s), 256)
    _, akc, avc, _, bkc, bvc = _proj(ctx, mod_ctx, gn0, win, wbq, wbkv, *gains, _identity_tables(n_ctx), n_ctx)

    ya = _win_attn(sink[0], aq, ak, av, akc, avc)
    yb = _mla_attn(bq, bk, bv, bkc, bvc, 512, 512)

    wo = w_o_ab[0].astype(BF16)
    n_a = A_HEADS * HEAD_DIM
    x1, hp, idx, wts = _out_proj([ya, yb], [wo[:n_a], wo[n_a:]], jnp.zeros((1, d), F32), x, mod_lat[0],
                                 g_norm[0, 1][None], wr, br, 512, False)
    dest, src_tok, blk_e, n_used = _dispatch(idx, MOE_BLK)
    yp = _moe(hp.reshape(b * s, HALF_D), src_tok, blk_e, n_used,
              w_gate[0].astype(BF16), w_up[0].astype(BF16), w_down[0].astype(BF16), MOE_BLK)
    x2, hn = _combine(dest, yp, x1, jnp.transpose(wts, (0, 2, 1)), mod_lat[0], 256,
                      norm=(g_norm[1, 0][None], mod_lat[1]))

    f = _fourier(hn, _dft_constants(s))
    tm1 = s // DFT_N1
    x3, hp, idx, wts = _out_proj([f], [w_fo[0].astype(BF16)], b_fo[0][None], x2, mod_lat[1],
                                 g_norm[1, 1][None], wr, br, tm1, True)
    dest, src_tok, blk_e, n_used = _dispatch(idx, MOE_BLK)
    yp = _moe(hp.reshape(b * s, HALF_D), src_tok, blk_e, n_used,
              w_gate[1].astype(BF16), w_up[1].astype(BF16), w_down[1].astype(BF16), MOE_BLK)
    (x4,) = _combine(dest, yp, x3, jnp.transpose(wts, (0, 2, 1)), mod_lat[1], tm1, permuted_out=True)
    return x4
```

```python
import functools
import math

import numpy as np
import jax
import jax.numpy as jnp
from jax import lax
from jax.experimental import pallas as pl
from jax.experimental.pallas import tpu as pltpu

F32 = jnp.float32
BF16 = jnp.bfloat16
I32 = jnp.int32
HIGHEST = lax.Precision.HIGHEST

D_MODEL = 2048
DEPTH = 2
GRID_W = 64
HEAD_DIM = 128
A_HEADS = 8
A_KV_HEADS = 2
A_GROUP = A_HEADS // A_KV_HEADS
WINDOW = 128
WBLK = 128
B_HEADS = 8
Q_LORA = 512
KV_LORA = 256
NOPE_DIM = 128
ROPE_DIM = 64
V_DIM = 128
QK_DIM = NOPE_DIM + ROPE_DIM
QK_PAD = 256
V_PAD = 256
IN_SPLITS = (A_HEADS * HEAD_DIM, A_KV_HEADS * HEAD_DIM, A_KV_HEADS * HEAD_DIM, Q_LORA, KV_LORA, ROPE_DIM)
IN_WIDTH = sum(IN_SPLITS)
IN_PAD = 2432
F_GROUPS = 8
F_GROUP_DIM = D_MODEL // F_GROUPS
N_EXPERTS = 16
N_GROUPS = 4
EXP_PER_GROUP = N_EXPERTS // N_GROUPS
TOP_K = 2
D_EXPERT = 1024
ROPE_BASE = 10000.0
EPS = 1e-6
LOG2E = math.log2(math.e)
LANES = 128

MOE_BLK = 512
DFT_N1 = 16
DFT_N2 = 256
VMEM_LIMIT = 56 * 1024 * 1024


def _cparams(sem, **kw):
    return pltpu.CompilerParams(dimension_semantics=sem, vmem_limit_bytes=VMEM_LIMIT, **kw)


def _resident(shape):
    nd = len(shape)
    return pl.BlockSpec(shape, lambda *_: (0,) * nd, pipeline_mode=pl.Buffered(1))


def _rms(t, width=None):
    n = t.shape[-1] if width is None else width
    ss = jnp.sum(t * t, axis=-1, keepdims=True)
    return t * lax.rsqrt(ss * (1.0 / n) + EPS)


def _rope(t, cos, sneg, spos, dist):
    n = t.shape[-1]
    return t * cos + pltpu.roll(t, n - dist, 1) * sneg + pltpu.roll(t, dist, 1) * spos


def _ada_kernel(c_ref, w_ref, b_ref, o_ref):
    c = c_ref[...]
    s = c * jax.nn.sigmoid(c)
    o_ref[0] = jnp.dot(s, w_ref[0], precision=HIGHEST, preferred_element_type=F32) + b_ref[0]


def _ada(crows, w_ada, b_ada):
    depth, d, n = w_ada.shape
    tn = 1024
    return pl.pallas_call(
        _ada_kernel,
        grid=(depth, n // tn),
        in_specs=[
            pl.BlockSpec((8, d), lambda l, j: (0, 0)),
            pl.BlockSpec((1, d, tn), lambda l, j: (l, 0, j)),
            pl.BlockSpec((1, 1, tn), lambda l, j: (l, 0, j)),
        ],
        out_specs=pl.BlockSpec((1, 8, tn), lambda l, j: (l, 0, j)),
        out_shape=jax.ShapeDtypeStruct((depth, 8, n), F32),
        compiler_params=_cparams(("arbitrary", "arbitrary")),
        name="ada",
    )(crows, w_ada, b_ada.reshape(depth, 1, n))


def _proj_kernel(x_ref, mod_ref, gn_ref, win_ref, wbq_ref, wbkv_ref, gaq_ref, gak_ref, gbql_ref, gbkvl_ref,
                 gbq_ref, gbk_ref, ca_ref, sna_ref, spa_ref, cb_ref, snb_ref, spb_ref,
                 aq_ref, ak_ref, av_ref, bq_ref, bk_ref, bv_ref):
    x = x_ref[0]
    shift = mod_ref[0, 0:1, :]
    scale = mod_ref[0, 1:2, :]
    h = (_rms(x) * gn_ref[...]) * (1.0 + scale) + shift
    p = jnp.dot(h.astype(BF16), win_ref[...], preferred_element_type=F32)

    ca, sna, spa = ca_ref[...], sna_ref[...], spa_ref[...]
    cb, snb, spb = cb_ref[...], snb_ref[...], spb_ref[...]
    a_scale = HEAD_DIM ** -0.5 * LOG2E
    b_scale = QK_DIM ** -0.5 * LOG2E

    for hd in range(A_HEADS):
        t = _rms(p[:, hd * HEAD_DIM:(hd + 1) * HEAD_DIM]) * gaq_ref[...]
        aq_ref[0, :, hd * HEAD_DIM:(hd + 1) * HEAD_DIM] = (_rope(t, ca, sna, spa, 32) * a_scale).astype(BF16)
    off = A_HEADS * HEAD_DIM
    for kh in range(A_KV_HEADS):
        t = _rms(p[:, off + kh * HEAD_DIM:off + (kh + 1) * HEAD_DIM]) * gak_ref[...]
        ak_ref[0, :, kh * HEAD_DIM:(kh + 1) * HEAD_DIM] = _rope(t, ca, sna, spa, 32).astype(BF16)
    off += A_KV_HEADS * HEAD_DIM
    av_ref[0] = p[:, off:off + A_KV_HEADS * HEAD_DIM].astype(BF16)
    off += A_KV_HEADS * HEAD_DIM

    ql = (_rms(p[:, off:off + Q_LORA]) * gbql_ref[...]).astype(BF16)
    off += Q_LORA
    qu = jnp.dot(ql, wbq_ref[...], preferred_element_type=F32)
    for hd in range(B_HEADS):
        t = _rms(qu[:, hd * QK_PAD:(hd + 1) * QK_PAD], QK_DIM) * gbq_ref[...]
        bq_ref[0, :, hd * QK_PAD:hd * QK_PAD + NOPE_DIM] = (t[:, :NOPE_DIM] * b_scale).astype(BF16)
        bq_ref[0, :, hd * QK_PAD + NOPE_DIM:(hd + 1) * QK_PAD] = (
            _rope(t[:, NOPE_DIM:], cb, snb, spb, 16) * b_scale).astype(BF16)

    kvl = (_rms(p[:, off:off + KV_LORA]) * gbkvl_ref[...]).astype(BF16)
    off += KV_LORA
    kvu = jnp.dot(kvl, wbkv_ref[...], preferred_element_type=F32)
    kr = p[:, off:off + LANES]
    kr_ss = jnp.sum(kr * kr, axis=-1, keepdims=True)
    kr_rot = _rope(kr * gbk_ref[:, NOPE_DIM:], cb, snb, spb, 16)
    for hd in range(B_HEADS):
        kn = kvu[:, hd * NOPE_DIM:(hd + 1) * NOPE_DIM]
        ss = jnp.sum(kn * kn, axis=-1, keepdims=True) + kr_ss
        r = lax.rsqrt(ss * (1.0 / QK_DIM) + EPS)
        bk_ref[0, :, hd * QK_PAD:hd * QK_PAD + NOPE_DIM] = (kn * r * gbk_ref[:, :NOPE_DIM]).astype(BF16)
        bk_ref[0, :, hd * QK_PAD + NOPE_DIM:(hd + 1) * QK_PAD] = (kr_rot * r).astype(BF16)
    ones_col = (lax.broadcasted_iota(I32, (kvu.shape[0], V_PAD - V_DIM), 1) == 0).astype(BF16)
    for hd in range(B_HEADS):
        v = kvu[:, B_HEADS * NOPE_DIM + hd * V_DIM:B_HEADS * NOPE_DIM + (hd + 1) * V_DIM]
        bv_ref[0, :, hd * V_PAD:hd * V_PAD + V_DIM] = v.astype(BF16)
        bv_ref[0, :, hd * V_PAD + V_DIM:(hd + 1) * V_PAD] = ones_col


def _proj(x, mod, gn, win, wbq, wbkv, gaq, gak, gbql, gbkvl, gbq, gbk, tabs, tm):
    b, s, d = x.shape
    row = lambda w: pl.BlockSpec((1, tm, w), lambda i, j: (i, j, 0))
    tab = pl.BlockSpec((tm, LANES), lambda i, j: (j, 0))
    widths = (A_HEADS * HEAD_DIM, A_KV_HEADS * HEAD_DIM, A_KV_HEADS * HEAD_DIM,
              B_HEADS * QK_PAD, B_HEADS * QK_PAD, B_HEADS * V_PAD)
    return pl.pallas_call(
        _proj_kernel,
        grid=(b, s // tm),
        in_specs=[row(d), pl.BlockSpec((1, 6, d), lambda i, j: (i, 0, 0)), _resident(gn.shape),
                  _resident(win.shape), _resident(wbq.shape), _resident(wbkv.shape),
                  _resident(gaq.shape), _resident(gak.shape), _resident(gbql.shape), _resident(gbkvl.shape),
                  _resident(gbq.shape), _resident(gbk.shape)] + [tab] * 6,
        out_specs=[row(w) for w in widths],
        out_shape=[jax.ShapeDtypeStruct((b, s, w), BF16) for w in widths],
        compiler_params=_cparams(("arbitrary", "arbitrary")),
        name="proj",
    )(x, mod, gn, win, wbq, wbkv, gaq, gak, gbql, gbkvl, gbq, gbk, *tabs)


def _win_kernel(sink_ref, q_ref, kp_ref, kc_ref, kn_ref, vp_ref, vc_ref, vn_ref, kx_ref, vx_ref, o_ref, *, seq):
    n = pl.program_id(1)
    rows = A_GROUP * WBLK
    band = 3 * WBLK
    ctx = kx_ref.shape[1]
    r_iota = lax.broadcasted_iota(I32, (rows, band), 0)
    c_iota = lax.broadcasted_iota(I32, (rows, band), 1)
    qpos = n * WBLK + (r_iota & (WBLK - 1))
    kpos = (n - 1) * WBLK + c_iota
    valid = (jnp.abs(qpos - kpos) <= WINDOW) & (kpos >= 0) & (kpos < seq)
    head_of_row = lax.broadcasted_iota(I32, (rows, 1), 0) // WBLK
    dn = (((1,), (1,)), ((), ()))
    for kh in range(A_KV_HEADS):
        cs = slice(kh * HEAD_DIM, (kh + 1) * HEAD_DIM)
        q = jnp.concatenate([q_ref[0, :, (kh * A_GROUP + g) * HEAD_DIM:(kh * A_GROUP + g + 1) * HEAD_DIM]
                             for g in range(A_GROUP)], axis=0)
        kb = jnp.concatenate([kp_ref[0, :, cs], kc_ref[0, :, cs], kn_ref[0, :, cs]], axis=0)
        vb = jnp.concatenate([vp_ref[0, :, cs], vc_ref[0, :, cs], vn_ref[0, :, cs]], axis=0)
        s_loc = jnp.where(valid, lax.dot_general(q, kb, dn, preferred_element_type=F32), -jnp.inf)
        s_ctx = lax.dot_general(q, kx_ref[0, :, cs], dn, preferred_element_type=F32)
        sink = jnp.zeros((rows, 1), F32)
        for g in range(A_GROUP):
            sink = jnp.where(head_of_row == g, sink_ref[kh * A_GROUP + g] * LOG2E, sink)
        m = jnp.maximum(jnp.maximum(jnp.max(s_loc, axis=-1, keepdims=True),
                                    jnp.max(s_ctx, axis=-1, keepdims=True)), sink)
        p_loc = jnp.exp2(s_loc - m)
        p_ctx = jnp.exp2(s_ctx - m)
        den = (jnp.sum(p_loc, axis=-1, keepdims=True) + jnp.sum(p_ctx, axis=-1, keepdims=True)
               + jnp.exp2(sink - m))
        o = (jnp.dot(p_loc.astype(BF16), vb, preferred_element_type=F32)
             + jnp.dot(p_ctx.astype(BF16), vx_ref[0, :, cs], preferred_element_type=F32)) / den
        for g in range(A_GROUP):
            hd = kh * A_GROUP + g
            o_ref[0, :, hd * HEAD_DIM:(hd + 1) * HEAD_DIM] = o[g * WBLK:(g + 1) * WBLK].astype(BF16)


def _win_attn(sink, aq, ak, av, akc, avc):
    b, s, _ = aq.shape
    nb = s // WBLK
    c = akc.shape[1]
    kvw = A_KV_HEADS * HEAD_DIM
    prev = pl.BlockSpec((1, WBLK, kvw), lambda i, j: (i, jnp.maximum(j - 1, 0), 0))
    cur = pl.BlockSpec((1, WBLK, kvw), lambda i, j: (i, j, 0))
    nxt = pl.BlockSpec((1, WBLK, kvw), lambda i, j: (i, jnp.minimum(j + 1, nb - 1), 0))
    cx = pl.BlockSpec((1, c, kvw), lambda i, j: (i, 0, 0))
    qo = pl.BlockSpec((1, WBLK, A_HEADS * HEAD_DIM), lambda i, j: (i, j, 0))
    return pl.pallas_call(
        functools.partial(_win_kernel, seq=s),
        grid=(b, nb),
        in_specs=[pl.BlockSpec(memory_space=pltpu.SMEM), qo, prev, cur, nxt, prev, cur, nxt, cx, cx],
        out_specs=qo,
        out_shape=jax.ShapeDtypeStruct(aq.shape, BF16),
        compiler_params=_cparams(("arbitrary", "arbitrary")),
        name="win_attn",
    )(sink, aq, ak, ak, ak, av, av, av, akc, avc)


def _mla_kernel(q_ref, k_ref, v_ref, kx_ref, vx_ref, o_ref, s_a, s_b, acc_ref, *, tk):
    q = q_ref[0]
    dn = (((1,), (1,)), ((), ()))
    n_chunks = k_ref.shape[1] // tk
    s_bufs = (s_a, s_b)

    def scores_into(buf, c):
        s = lax.dot_general(q, k_ref[0, c * tk:(c + 1) * tk, :], dn, preferred_element_type=F32)
        buf[...] = s
        return jnp.max(s, axis=-1, keepdims=True)

    s0 = lax.dot_general(q, kx_ref[0], dn, preferred_element_type=F32)
    m = jnp.max(s0, axis=-1, keepdims=True)
    acc_ref[...] = jnp.dot(jnp.exp2(s0 - m).astype(BF16), vx_ref[0], preferred_element_type=F32)
    mx = scores_into(s_bufs[0], 0)
    for c in range(n_chunks):
        if c + 1 < n_chunks:
            mx_next = scores_into(s_bufs[(c + 1) % 2], c + 1)
        m_new = jnp.maximum(m, mx)
        p = jnp.exp2(s_bufs[c % 2][...] - m_new).astype(BF16)
        acc_ref[...] = (jnp.exp2(m - m_new) * acc_ref[...]
                        + jnp.dot(p, v_ref[0, c * tk:(c + 1) * tk, :], preferred_element_type=F32))
        m, mx = m_new, mx_next
    acc = acc_ref[...]
    o_ref[0] = (acc[:, :V_DIM] / acc[:, V_DIM:V_DIM + 1]).astype(BF16)


def _mla_attn(bq, bk, bv, bkc, bvc, tq, tk):
    b, s, _ = bq.shape
    c = bkc.shape[1]
    return pl.pallas_call(
        functools.partial(_mla_kernel, tk=tk),
        grid=(b, B_HEADS, s // tq),
        in_specs=[
            pl.BlockSpec((1, tq, QK_PAD), lambda i, h, j: (i, j, h)),
            pl.BlockSpec((1, s, QK_PAD), lambda i, h, j: (i, 0, h)),
            pl.BlockSpec((1, s, V_PAD), lambda i, h, j: (i, 0, h)),
            pl.BlockSpec((1, c, QK_PAD), lambda i, h, j: (i, 0, h)),
            pl.BlockSpec((1, c, V_PAD), lambda i, h, j: (i, 0, h)),
        ],
        out_specs=pl.BlockSpec((1, tq, V_DIM), lambda i, h, j: (i, j, h)),
        out_shape=jax.ShapeDtypeStruct((b, s, B_HEADS * V_DIM), BF16),
        scratch_shapes=[pltpu.VMEM((tq, tk), F32), pltpu.VMEM((tq, tk), F32), pltpu.VMEM((tq, V_PAD), F32)],
        compiler_params=_cparams(("arbitrary", "arbitrary", "arbitrary")),
        name="mla_attn",
    )(bq, bk, bv, bkc, bvc)


def _route(sel, aff):
    scores = []
    for g in range(N_GROUPS):
        r = sel[g * EXP_PER_GROUP:(g + 1) * EXP_PER_GROUP]
        best = None
        for a in range(EXP_PER_GROUP):
            for b in range(a + 1, EXP_PER_GROUP):
                pair = r[a] + r[b]
                best = pair if best is None else jnp.maximum(best, pair)
        scores.append(best)
    top, grp = scores[0], jnp.zeros_like(scores[0], dtype=I32)
    for g in range(1, N_GROUPS):
        take = scores[g] > top
        grp = jnp.where(take, g, grp)
        top = jnp.where(take, scores[g], top)
    masked = [jnp.where(grp == e // EXP_PER_GROUP, sel[e], -jnp.inf) for e in range(N_EXPERTS)]

    def argmax_first(vals, skip=None):
        bv = jnp.full_like(vals[0], -jnp.inf)
        bi = jnp.full_like(grp, -1)
        for e in range(N_EXPERTS):
            take = vals[e] > bv
            if skip is not None:
                take = take & (skip != e)
            bi = jnp.where(take, e, bi)
            bv = jnp.where(take, vals[e], bv)
        return bi

    i0 = argmax_first(masked)
    i1 = argmax_first(masked, skip=i0)
    a0 = jnp.zeros_like(aff[0])
    a1 = jnp.zeros_like(aff[0])
    for e in range(N_EXPERTS):
        a0 = jnp.where(i0 == e, aff[e], a0)
        a1 = jnp.where(i1 == e, aff[e], a1)
    tot = a0 + a1
    return i0, i1, a0 / tot, a1 / tot


def _out_kernel(*refs, n_lhs):
    lhs = refs[:n_lhs]
    ws = refs[n_lhs:2 * n_lhs]
    bias_ref, x_ref, mod_ref, gn_ref, wr_ref, br_ref, x1_ref, hp_ref, idx_ref, wts_ref = refs[2 * n_lhs:]
    y = bias_ref[...]
    for a, w in zip(lhs, ws):
        y = y + jnp.dot(a[0], w[...], preferred_element_type=F32)
    x1 = x_ref[0] + mod_ref[0, 2:3, :] * y
    x1_ref[0] = x1
    h2 = (_rms(x1) * gn_ref[...]) * (1.0 + mod_ref[0, 4:5, :]) + mod_ref[0, 3:4, :]
    hp_ref[0] = h2
    h_head = h2.astype(BF16)
    h_tail = (h2 - h_head.astype(F32)).astype(BF16)
    t = jnp.dot(h_head, wr_ref[...], preferred_element_type=F32)
    logits = (t[:, :LANES] + t[:, LANES:]) + jnp.dot(h_tail, wr_ref[:, :LANES], preferred_element_type=F32)
    lt = logits.T[:N_EXPERTS]
    aff_t = jax.nn.sigmoid(lt)
    sel_t = aff_t + br_ref[...]
    sel = [sel_t[e:e + 1] for e in range(N_EXPERTS)]
    aff = [aff_t[e:e + 1] for e in range(N_EXPERTS)]
    i0, i1, w0, w1 = _route(sel, aff)
    idx_ref[0] = jnp.concatenate([i0, i1], axis=0)
    wts_ref[0] = jnp.concatenate([w0, w1], axis=0)


def _out_proj(lhs, ws, bias, x, mod, gn, wr, br, tm):
    b, s, d = x.shape
    n_lhs = len(lhs)
    row = pl.BlockSpec((1, tm, d), lambda i, j: (i, j, 0))
    in_specs = ([pl.BlockSpec((1, tm, a.shape[-1]), lambda i, j: (i, j, 0)) for a in lhs]
                + [_resident(w.shape) for w in ws]
                + [_resident(bias.shape), row, pl.BlockSpec((1, 6, d), lambda i, j: (i, 0, 0)),
                   _resident(gn.shape), _resident(wr.shape), _resident(br.shape)])
    return pl.pallas_call(
        functools.partial(_out_kernel, n_lhs=n_lhs),
        grid=(b, s // tm),
        in_specs=in_specs,
        out_specs=[row, row,
                   pl.BlockSpec((1, TOP_K, tm), lambda i, j: (i, 0, j)),
                   pl.BlockSpec((1, TOP_K, tm), lambda i, j: (i, 0, j))],
        out_shape=[jax.ShapeDtypeStruct((b, s, d), F32), jax.ShapeDtypeStruct((b, s, d), F32),
                   jax.ShapeDtypeStruct((b, TOP_K, s), I32), jax.ShapeDtypeStruct((b, TOP_K, s), F32)],
        compiler_params=_cparams(("arbitrary", "arbitrary")),
        name="out_proj",
    )(*lhs, *ws, bias, x, mod, gn, wr, br)


def _dispatch(idx, blk):
    b, _, s = idx.shape
    t = b * s
    e = jnp.transpose(idx, (0, 2, 1)).reshape(t * TOP_K)
    onehot = (e[:, None] == jnp.arange(N_EXPERTS, dtype=I32)[None, :]).astype(I32)
    csum = jnp.cumsum(onehot, axis=0)
    counts = csum[-1]
    rank = jnp.sum((csum - onehot) * onehot, axis=1)
    padded = (counts + blk - 1) // blk * blk
    pad_end = jnp.cumsum(padded)
    pad_start = pad_end - padded
    dest = pad_start[e] + rank
    n_blocks = -(-t * TOP_K // blk) + N_EXPERTS
    tok = jnp.arange(t * TOP_K, dtype=I32) // TOP_K
    src_tok = jnp.zeros((n_blocks * blk,), I32).at[dest].set(tok)
    n_used = pad_end[-1] // blk
    blk_ids = jnp.minimum(jnp.arange(n_blocks, dtype=I32), n_used - 1)
    blk_e = jnp.sum((blk_ids[:, None] * blk >= pad_end[None, :]).astype(I32), axis=1)
    blk_e = jnp.minimum(blk_e, N_EXPERTS - 1)
    return dest.reshape(t, TOP_K).astype(I32), src_tok, blk_e, n_used.reshape(1).astype(I32)


def _moe_kernel(blk_e_ref, n_used_ref, tok_ref, hp_ref, wg_ref, wu_ref, wd_ref, o_ref, xbuf, sem, *, blk):
    i = pl.program_id(0)
    n_used = n_used_ref[0]
    slot = i % 2

    def gather(block, slot_):
        def body(r, _):
            t = tok_ref[block * blk + r]
            pltpu.make_async_copy(hp_ref.at[pl.ds(t, 1)], xbuf.at[slot_, pl.ds(r, 1)], sem.at[slot_]).start()
            return 0
        lax.fori_loop(0, blk, body, 0, unroll=8)

    @pl.when(i == 0)
    def _():
        gather(0, 0)

    @pl.when(i + 1 < n_used)
    def _():
        gather(i + 1, 1 - slot)

    @pl.when(i < n_used)
    def _():
        pltpu.make_async_copy(xbuf.at[slot], xbuf.at[slot], sem.at[slot]).wait()
        xb = xbuf[slot].astype(BF16)
        g = jnp.dot(xb, wg_ref[0, 0], preferred_element_type=F32)
        u = jnp.dot(xb, wu_ref[0, 0], preferred_element_type=F32)
        a = (g * jax.nn.sigmoid(g) * u).astype(BF16)
        o_ref[...] = jnp.dot(a, wd_ref[0, 0], preferred_element_type=F32)

    @pl.when(i >= n_used)
    def _():
        o_ref[...] = jnp.zeros_like(o_ref)


def _moe(h2, src_tok, blk_e, n_used, wg, wu, wd, layer, blk):
    n_blocks = blk_e.shape[0]
    _, _, d, de = wg.shape
    grid_spec = pltpu.PrefetchScalarGridSpec(
        num_scalar_prefetch=3,
        grid=(n_blocks,),
        in_specs=[
            pl.BlockSpec(memory_space=pl.ANY),
            pl.BlockSpec((1, 1, d, de), lambda i, be, nu, tk: (layer, be[i], 0, 0)),
            pl.BlockSpec((1, 1, d, de), lambda i, be, nu, tk: (layer, be[i], 0, 0)),
            pl.BlockSpec((1, 1, de, d), lambda i, be, nu, tk: (layer, be[i], 0, 0)),
        ],
        out_specs=pl.BlockSpec((blk, d), lambda i, be, nu, tk: (i, 0)),
        scratch_shapes=[pltpu.VMEM((2, blk, d), F32), pltpu.SemaphoreType.DMA((2,))],
    )
    return pl.pallas_call(
        functools.partial(_moe_kernel, blk=blk),
        grid_spec=grid_spec,
        out_shape=jax.ShapeDtypeStruct((n_blocks * blk, d), F32),
        compiler_params=_cparams(("arbitrary",)),
        name="moe_ffn",
    )(blk_e, n_used, src_tok, h2, wg, wu, wd)


def _combine_kernel(dest_ref, yp_ref, x_ref, wt_ref, mod_ref, *rest, tm, nt, with_norm):
    if with_norm:
        gn_ref, modn_ref, o_ref, hn_ref, buf, sem = rest
    else:
        o_ref, buf, sem = rest
    i = pl.program_id(0)
    j = pl.program_id(1)
    step = i * nt + j
    slot = step % 2

    def gather(step_, slot_):
        def body(r, _):
            base = (step_ * tm + r) * TOP_K
            for k in range(TOP_K):
                pltpu.make_async_copy(yp_ref.at[pl.ds(dest_ref[base + k], 1)],
                                      buf.at[slot_, k, pl.ds(r, 1)], sem.at[slot_]).start()
            return 0
        lax.fori_loop(0, tm, body, 0, unroll=8)

    @pl.when(step == 0)
    def _():
        gather(0, 0)

    @pl.when(step + 1 < pl.num_programs(0) * nt)
    def _():
        gather(step + 1, 1 - slot)

    pltpu.make_async_copy(buf.at[slot], buf.at[slot], sem.at[slot]).wait()
    w = wt_ref[0]
    y = buf[slot, 0] * w[:, 0:1] + buf[slot, 1] * w[:, 1:2]
    out = x_ref[0] + mod_ref[0, 5:6, :] * y
    o_ref[0] = out
    if with_norm:
        hn_ref[0] = ((_rms(out) * gn_ref[...]) * (1.0 + modn_ref[0, 1:2, :]) + modn_ref[0, 0:1, :]).astype(BF16)


def _combine(dest, yp, x1, wts, mod, tm, norm=None):
    b, s, d = x1.shape
    nt = s // tm
    with_norm = norm is not None
    row = pl.BlockSpec((1, tm, d), lambda i, j, ds: (i, j, 0))
    modspec = pl.BlockSpec((1, 6, d), lambda i, j, ds: (i, 0, 0))
    in_specs = [pl.BlockSpec(memory_space=pl.ANY), row,
                pl.BlockSpec((1, tm, TOP_K), lambda i, j, ds: (i, j, 0)), modspec]
    args = [yp, x1, wts, mod]
    out_shape = [jax.ShapeDtypeStruct((b, s, d), F32)]
    out_specs = [row]
    if with_norm:
        gn, modn = norm
        in_specs += [pl.BlockSpec(gn.shape, lambda i, j, ds: (0, 0)), modspec]
        args += [gn, modn]
        out_shape.append(jax.ShapeDtypeStruct((b, s, d), BF16))
        out_specs.append(row)
    grid_spec = pltpu.PrefetchScalarGridSpec(
        num_scalar_prefetch=1, grid=(b, nt), in_specs=in_specs, out_specs=out_specs,
        scratch_shapes=[pltpu.VMEM((2, TOP_K, tm, d), F32), pltpu.SemaphoreType.DMA((2,))])
    return pl.pallas_call(
        functools.partial(_combine_kernel, tm=tm, nt=nt, with_norm=with_norm),
        grid_spec=grid_spec,
        out_shape=out_shape,
        compiler_params=_cparams(("arbitrary", "arbitrary")),
        name="combine",
    )(dest.reshape(-1), *args)


def _dft_constants(seq):
    n1, n2 = DFT_N1, DFT_N2
    assert seq == n1 * n2 and n2 == n1 * n1
    c = np.arange(F_GROUP_DIM)
    ang = 2 * np.pi * np.outer(c, c) / F_GROUP_DIM
    fc = np.concatenate([np.cos(ang), -np.sin(ang)], axis=1)
    a, k2, m = np.meshgrid(np.arange(n1), np.arange(n2), np.arange(n2), indexing="ij")
    ang_a = -2 * np.pi * (k2 * (a + n1 * m) % seq) / seq
    tre, tim = np.cos(ang_a), np.sin(ang_a)
    ma = np.concatenate([np.concatenate([tre, -tim], axis=2), np.concatenate([tim, tre], axis=2)], axis=1)
    ang_b = -2 * np.pi * np.outer(np.arange(n1), np.arange(n1)) / n1
    eye = np.eye(n1)
    mb = np.concatenate([np.kron(np.cos(ang_b), eye), -np.kron(np.sin(ang_b), eye)], axis=1)
    mb = mb / math.sqrt(seq * F_GROUP_DIM)
    return tuple(jnp.asarray(t, F32).astype(BF16) for t in (fc, ma, mb))


def _fourier_kernel(h_ref, fc_ref, ma_ref, mb_ref, o_ref, z_ref, yre, yim):
    n1, n2, gd = DFT_N1, DFT_N2, F_GROUP_DIM
    rows = 512
    n_tiles = 2 * gd // LANES
    for r in range(0, h_ref.shape[1], rows):
        z = jnp.dot(h_ref[0, r:r + rows, :], fc_ref[...], preferred_element_type=F32)
        for t in range(n_tiles):
            z_ref[t, r:r + rows, :] = z[:, t * LANES:(t + 1) * LANES]
    for a in range(n1):
        zs = [z_ref[t, pl.ds(a, n2, stride=n1), :].astype(BF16) for t in range(n_tiles)]
        rhs = jnp.concatenate([jnp.concatenate(zs[:n_tiles // 2], axis=1),
                               jnp.concatenate(zs[n_tiles // 2:], axis=1)], axis=0)
        y = jnp.dot(ma_ref[a], rhs, preferred_element_type=F32).astype(BF16)
        yre[a] = y[:n2]
        yim[a] = y[n2:]
    for hi in range(n1):
        rhs = jnp.concatenate([yre[a, hi * n1:(hi + 1) * n1, :] for a in range(n1)]
                              + [yim[a, hi * n1:(hi + 1) * n1, :] for a in range(n1)], axis=0)
        out = jnp.dot(mb_ref[...], rhs, preferred_element_type=F32).astype(BF16)
        for k1 in range(n1):
            o_ref[0, k1 * n2 + hi * n1:k1 * n2 + (hi + 1) * n1, :] = out[k1 * n1:(k1 + 1) * n1]


def _fourier(hn, consts):
    b, s, d = hn.shape
    fc, ma, mb = consts
    gd = F_GROUP_DIM
    return pl.pallas_call(
        _fourier_kernel,
        grid=(b, F_GROUPS),
        in_specs=[pl.BlockSpec((1, s, gd), lambda i, g: (i, 0, g)),
                  _resident(fc.shape), _resident(ma.shape), _resident(mb.shape)],
        out_specs=pl.BlockSpec((1, s, gd), lambda i, g: (i, 0, g)),
        out_shape=jax.ShapeDtypeStruct((b, s, d), BF16),
        scratch_shapes=[pltpu.VMEM((2 * gd // LANES, s, LANES), F32),
                        pltpu.VMEM((DFT_N1, DFT_N2, gd), BF16), pltpu.VMEM((DFT_N1, DFT_N2, gd), BF16)],
        compiler_params=_cparams(("arbitrary", "arbitrary")),
        name="fourier",
    )(hn, fc, ma, mb)


def _rope_tables(seq):
    pos = np.arange(seq)
    row, col = (pos // GRID_W).astype(np.float64), (pos % GRID_W).astype(np.float64)

    def tables(width):
        half = width // 2
        quarter = half // 2
        freqs = ROPE_BASE ** (-np.arange(0, half, 2, dtype=np.float64) / half)
        lane = np.arange(LANES)
        ang = np.where((lane < half)[None, :], row[:, None], col[:, None]) * freqs[lane % quarter][None, :]
        live = (lane < width)[None, :]
        first = ((lane % half) < quarter)[None, :]
        cos = np.where(live, np.cos(ang), 1.0)
        sneg = np.where(live & first, -np.sin(ang), 0.0)
        spos = np.where(live & ~first, np.sin(ang), 0.0)
        return [jnp.asarray(t, F32) for t in (cos, sneg, spos)]

    return tables(HEAD_DIM) + tables(ROPE_DIM)


def _identity_tables(n):
    one, zero = jnp.ones((n, LANES), F32), jnp.zeros((n, LANES), F32)
    return [one, zero, zero, one, zero, zero]


def _pad_heads(w, lead):
    w = w.reshape(lead, B_HEADS, QK_DIM)
    return jnp.pad(w, ((0, 0), (0, 0), (0, QK_PAD - QK_DIM))).reshape(lead, B_HEADS * QK_PAD)


def kernel(x, c, ctx, c_ctx, w_ada, b_ada, g_norm, w_in, g_aqn, g_akn, g_bq_lat, w_bq_up, g_bkv_lat, w_bkv_up,
           g_bqn, g_bkn, sink, w_o_ab, w_fo, b_fo, w_router, b_router, w_gate, w_up, w_down):
    b, s, d = x.shape
    n_ctx = ctx.shape[1]

    crows = jnp.concatenate([c, c_ctx[None, :], jnp.zeros((8 - b - 1, d), F32)], axis=0)
    mods = _ada(crows, w_ada, b_ada).reshape(DEPTH, 8, 6, d)
    mod_lat = [mods[l, :b] for l in range(DEPTH)]
    mod_ctx = jnp.broadcast_to(mods[0, b][None], (b, 6, d))

    wr = jnp.pad(w_router, ((0, 0), (0, LANES - N_EXPERTS)))
    wr_head = wr.astype(BF16)
    wr = jnp.concatenate([wr_head, (wr - wr_head.astype(F32)).astype(BF16)], axis=1)
    br = b_router.reshape(N_EXPERTS, 1)
    wg, wu, wd = w_gate.astype(BF16), w_up.astype(BF16), w_down.astype(BF16)

    win = jnp.pad(w_in[0], ((0, 0), (0, IN_PAD - IN_WIDTH))).astype(BF16)
    wbq = _pad_heads(w_bq_up[0], Q_LORA).astype(BF16)
    wkv = w_bkv_up[0].reshape(KV_LORA, B_HEADS, NOPE_DIM + V_DIM)
    wbkv = jnp.concatenate([wkv[:, :, :NOPE_DIM].reshape(KV_LORA, -1), wkv[:, :, NOPE_DIM:].reshape(KV_LORA, -1)],
                           axis=1).astype(BF16)
    gains = (g_aqn[0][None], g_akn[0][None], g_bq_lat[0][None], g_bkv_lat[0][None],
             jnp.pad(g_bqn[0][None], ((0, 0), (0, QK_PAD - QK_DIM))),
             jnp.pad(g_bkn[0][None], ((0, 0), (0, QK_PAD - QK_DIM))))
    gn0 = g_norm[0, 0][None]
    aq, ak, av, bq, bk, bv = _proj(x, mod_lat[0], gn0, win, wbq, wbkv, *gains, _rope_tables(s), 256)
    _, akc, avc, _, bkc, bvc = _proj(ctx, mod_ctx, gn0, win, wbq, wbkv, *gains, _identity_tables(n_ctx), n_ctx)

    ya = _win_attn(sink[0], aq, ak, av, akc, avc)
    yb = _mla_attn(bq, bk, bv, bkc, bvc, 512, 512)

    wo = w_o_ab[0].astype(BF16)
    n_a = A_HEADS * HEAD_DIM
    x1, h2, idx, wts = _out_proj([ya, yb], [wo[:n_a], wo[n_a:]], jnp.zeros((1, d), F32), x, mod_lat[0],
                                 g_norm[0, 1][None], wr, br, 512)
    dest, src_tok, blk_e, n_used = _dispatch(idx, MOE_BLK)
    yp = _moe(h2.reshape(b * s, d), src_tok, blk_e, n_used, wg, wu, wd, 0, MOE_BLK)
    x2, hn = _combine(dest, yp, x1, jnp.transpose(wts, (0, 2, 1)), mod_lat[0], 256,
                      norm=(g_norm[1, 0][None], mod_lat[1]))

    f = _fourier(hn, _dft_constants(s))
    x3, h2, idx, wts = _out_proj([f], [w_fo[0].astype(BF16)], b_fo[0][None], x2, mod_lat[1],
                                 g_norm[1, 1][None], wr, br, 512)
    dest, src_tok, blk_e, n_used = _dispatch(idx, MOE_BLK)
    yp = _moe(h2.reshape(b * s, d), src_tok, blk_e, n_used, wg, wu, wd, 1, MOE_BLK)
    (x4,) = _combine(dest, yp, x3, jnp.transpose(wts, (0, 2, 1)), mod_lat[1], 256)
    return x4
```

```python
import functools
import math

import numpy as np
import jax
import jax.numpy as jnp
from jax import lax
from jax.experimental import pallas as pl
from jax.experimental.pallas import tpu as pltpu

F32 = jnp.float32
BF16 = jnp.bfloat16
I32 = jnp.int32
HIGHEST = lax.Precision.HIGHEST

D_MODEL = 2048
DEPTH = 2
GRID_W = 64
HEAD_DIM = 128
A_HEADS = 8
A_KV_HEADS = 2
A_GROUP = A_HEADS // A_KV_HEADS
WINDOW = 128
WBLK = 128
B_HEADS = 8
Q_LORA = 512
KV_LORA = 256
NOPE_DIM = 128
ROPE_DIM = 64
V_DIM = 128
QK_DIM = NOPE_DIM + ROPE_DIM
QK_PAD = 256
V_PAD = 256
IN_SPLITS = (A_HEADS * HEAD_DIM, A_KV_HEADS * HEAD_DIM, A_KV_HEADS * HEAD_DIM, Q_LORA, KV_LORA, ROPE_DIM)
IN_WIDTH = sum(IN_SPLITS)
IN_PAD = 2432
F_GROUPS = 8
F_GROUP_DIM = D_MODEL // F_GROUPS
N_EXPERTS = 16
N_GROUPS = 4
EXP_PER_GROUP = N_EXPERTS // N_GROUPS
TOP_K = 2
D_EXPERT = 1024
ROPE_BASE = 10000.0
EPS = 1e-6
LOG2E = math.log2(math.e)
LANES = 128

MOE_BLK = 512
DFT_N1 = 16
DFT_N2 = 256
VMEM_LIMIT = 56 * 1024 * 1024


def _cparams(sem, **kw):
    return pltpu.CompilerParams(dimension_semantics=sem, vmem_limit_bytes=VMEM_LIMIT, **kw)


def _resident(shape):
    nd = len(shape)
    return pl.BlockSpec(shape, lambda *_: (0,) * nd, pipeline_mode=pl.Buffered(1))


def _rms(t, width=None):
    n = t.shape[-1] if width is None else width
    ss = jnp.sum(t * t, axis=-1, keepdims=True)
    return t * lax.rsqrt(ss * (1.0 / n) + EPS)


def _rope(t, cos, sneg, spos, dist):
    n = t.shape[-1]
    return t * cos + pltpu.roll(t, n - dist, 1) * sneg + pltpu.roll(t, dist, 1) * spos


def _ada_kernel(c_ref, w_ref, b_ref, o_ref):
    c = c_ref[...]
    s = c * jax.nn.sigmoid(c)
    o_ref[0] = jnp.dot(s, w_ref[0], precision=HIGHEST, preferred_element_type=F32) + b_ref[0]


def _ada(crows, w_ada, b_ada):
    depth, d, n = w_ada.shape
    tn = 1024
    return pl.pallas_call(
        _ada_kernel,
        grid=(depth, n // tn),
        in_specs=[
            pl.BlockSpec((8, d), lambda l, j: (0, 0)),
            pl.BlockSpec((1, d, tn), lambda l, j: (l, 0, j)),
            pl.BlockSpec((1, 1, tn), lambda l, j: (l, 0, j)),
        ],
        out_specs=pl.BlockSpec((1, 8, tn), lambda l, j: (l, 0, j)),
        out_shape=jax.ShapeDtypeStruct((depth, 8, n), F32),
        compiler_params=_cparams(("arbitrary", "arbitrary")),
        name="ada",
    )(crows, w_ada, b_ada.reshape(depth, 1, n))


def _proj_kernel(x_ref, mod_ref, gn_ref, win_ref, wbq_ref, wbkv_ref, gaq_ref, gak_ref, gbql_ref, gbkvl_ref,
                 gbq_ref, gbk_ref, ca_ref, sna_ref, spa_ref, cb_ref, snb_ref, spb_ref,
                 aq_ref, ak_ref, av_ref, bq_ref, bk_ref, bv_ref):
    x = x_ref[0]
    shift = mod_ref[0, 0:1, :]
    scale = mod_ref[0, 1:2, :]
    hb = ((_rms(x) * gn_ref[...]) * (1.0 + scale) + shift).astype(BF16)

    def cols(w_ref, lhs, lo, width):
        return jnp.dot(lhs, w_ref[:, lo:lo + width], preferred_element_type=F32)

    ca, sna, spa = ca_ref[...], sna_ref[...], spa_ref[...]
    cb, snb, spb = cb_ref[...], snb_ref[...], spb_ref[...]
    a_scale = HEAD_DIM ** -0.5 * LOG2E
    b_scale = QK_DIM ** -0.5 * LOG2E
    pair = 2 * HEAD_DIM

    for hp in range(A_HEADS // 2):
        pp = cols(win_ref, hb, hp * pair, pair)
        for j in range(2):
            hd = 2 * hp + j
            t = _rms(pp[:, j * HEAD_DIM:(j + 1) * HEAD_DIM]) * gaq_ref[...]
            aq_ref[0, :, hd * HEAD_DIM:(hd + 1) * HEAD_DIM] = (_rope(t, ca, sna, spa, 32) * a_scale).astype(BF16)
    off = A_HEADS * HEAD_DIM
    pp = cols(win_ref, hb, off, A_KV_HEADS * HEAD_DIM)
    for kh in range(A_KV_HEADS):
        t = _rms(pp[:, kh * HEAD_DIM:(kh + 1) * HEAD_DIM]) * gak_ref[...]
        ak_ref[0, :, kh * HEAD_DIM:(kh + 1) * HEAD_DIM] = _rope(t, ca, sna, spa, 32).astype(BF16)
    off += A_KV_HEADS * HEAD_DIM
    av_ref[0] = cols(win_ref, hb, off, A_KV_HEADS * HEAD_DIM).astype(BF16)
    off += A_KV_HEADS * HEAD_DIM

    ql = (_rms(cols(win_ref, hb, off, Q_LORA)) * gbql_ref[...]).astype(BF16)
    off += Q_LORA
    for hd in range(B_HEADS):
        t = _rms(cols(wbq_ref, ql, hd * QK_PAD, QK_PAD), QK_DIM) * gbq_ref[...]
        bq_ref[0, :, hd * QK_PAD:hd * QK_PAD + NOPE_DIM] = (t[:, :NOPE_DIM] * b_scale).astype(BF16)
        bq_ref[0, :, hd * QK_PAD + NOPE_DIM:(hd + 1) * QK_PAD] = (
            _rope(t[:, NOPE_DIM:], cb, snb, spb, 16) * b_scale).astype(BF16)

    kvl = (_rms(cols(win_ref, hb, off, KV_LORA)) * gbkvl_ref[...]).astype(BF16)
    off += KV_LORA
    kr = cols(win_ref, hb, off, LANES)
    kr_ss = jnp.sum(kr * kr, axis=-1, keepdims=True)
    kr_rot = _rope(kr * gbk_ref[:, NOPE_DIM:], cb, snb, spb, 16)
    ones_col = (lax.broadcasted_iota(I32, (kr.shape[0], V_PAD - V_DIM), 1) == 0).astype(BF16)
    for hp in range(B_HEADS // 2):
        kn2 = cols(wbkv_ref, kvl, hp * pair, pair)
        v2 = cols(wbkv_ref, kvl, B_HEADS * NOPE_DIM + hp * pair, pair)
        for j in range(2):
            hd = 2 * hp + j
            kn = kn2[:, j * NOPE_DIM:(j + 1) * NOPE_DIM]
            ss = jnp.sum(kn * kn, axis=-1, keepdims=True) + kr_ss
            r = lax.rsqrt(ss * (1.0 / QK_DIM) + EPS)
            bk_ref[0, :, hd * QK_PAD:hd * QK_PAD + NOPE_DIM] = (kn * r * gbk_ref[:, :NOPE_DIM]).astype(BF16)
            bk_ref[0, :, hd * QK_PAD + NOPE_DIM:(hd + 1) * QK_PAD] = (kr_rot * r).astype(BF16)
            bv_ref[0, :, hd * V_PAD:hd * V_PAD + V_DIM] = v2[:, j * V_DIM:(j + 1) * V_DIM].astype(BF16)
            bv_ref[0, :, hd * V_PAD + V_DIM:(hd + 1) * V_PAD] = ones_col


def _proj(x, mod, gn, win, wbq, wbkv, gaq, gak, gbql, gbkvl, gbq, gbk, tabs, tm):
    b, s, d = x.shape
    row = lambda w: pl.BlockSpec((1, tm, w), lambda i, j: (i, j, 0))
    tab = pl.BlockSpec((tm, LANES), lambda i, j: (j, 0))
    widths = (A_HEADS * HEAD_DIM, A_KV_HEADS * HEAD_DIM, A_KV_HEADS * HEAD_DIM,
              B_HEADS * QK_PAD, B_HEADS * QK_PAD, B_HEADS * V_PAD)
    return pl.pallas_call(
        _proj_kernel,
        grid=(b, s // tm),
        in_specs=[row(d), pl.BlockSpec((1, 6, d), lambda i, j: (i, 0, 0)), _resident(gn.shape),
                  _resident(win.shape), _resident(wbq.shape), _resident(wbkv.shape),
                  _resident(gaq.shape), _resident(gak.shape), _resident(gbql.shape), _resident(gbkvl.shape),
                  _resident(gbq.shape), _resident(gbk.shape)] + [tab] * 6,
        out_specs=[row(w) for w in widths],
        out_shape=[jax.ShapeDtypeStruct((b, s, w), BF16) for w in widths],
        compiler_params=_cparams(("arbitrary", "arbitrary")),
        name="proj",
    )(x, mod, gn, win, wbq, wbkv, gaq, gak, gbql, gbkvl, gbq, gbk, *tabs)


def _win_kernel(sink_ref, q_ref, kp_ref, kc_ref, kn_ref, vp_ref, vc_ref, vn_ref, kx_ref, vx_ref, o_ref, *, seq):
    n = pl.program_id(1)
    rows = A_GROUP * WBLK
    band = 3 * WBLK
    ctx = kx_ref.shape[1]
    r_iota = lax.broadcasted_iota(I32, (rows, band), 0)
    c_iota = lax.broadcasted_iota(I32, (rows, band), 1)
    qpos = n * WBLK + (r_iota & (WBLK - 1))
    kpos = (n - 1) * WBLK + c_iota
    valid = (jnp.abs(qpos - kpos) <= WINDOW) & (kpos >= 0) & (kpos < seq)
    head_of_row = lax.broadcasted_iota(I32, (rows, 1), 0) // WBLK
    dn = (((1,), (1,)), ((), ()))
    for kh in range(A_KV_HEADS):
        cs = slice(kh * HEAD_DIM, (kh + 1) * HEAD_DIM)
        q = jnp.concatenate([q_ref[0, :, (kh * A_GROUP + g) * HEAD_DIM:(kh * A_GROUP + g + 1) * HEAD_DIM]
                             for g in range(A_GROUP)], axis=0)
        kb = jnp.concatenate([kp_ref[0, :, cs], kc_ref[0, :, cs], kn_ref[0, :, cs]], axis=0)
        vb = jnp.concatenate([vp_ref[0, :, cs], vc_ref[0, :, cs], vn_ref[0, :, cs]], axis=0)
        s_loc = jnp.where(valid, lax.dot_general(q, kb, dn, preferred_element_type=F32), -jnp.inf)
        s_ctx = lax.dot_general(q, kx_ref[0, :, cs], dn, preferred_element_type=F32)
        sink = jnp.zeros((rows, 1), F32)
        for g in range(A_GROUP):
            sink = jnp.where(head_of_row == g, sink_ref[kh * A_GROUP + g] * LOG2E, sink)
        m = jnp.maximum(jnp.maximum(jnp.max(s_loc, axis=-1, keepdims=True),
                                    jnp.max(s_ctx, axis=-1, keepdims=True)), sink)
        p_loc = jnp.exp2(s_loc - m)
        p_ctx = jnp.exp2(s_ctx - m)
        den = (jnp.sum(p_loc, axis=-1, keepdims=True) + jnp.sum(p_ctx, axis=-1, keepdims=True)
               + jnp.exp2(sink - m))
        o = (jnp.dot(p_loc.astype(BF16), vb, preferred_element_type=F32)
             + jnp.dot(p_ctx.astype(BF16), vx_ref[0, :, cs], preferred_element_type=F32)) / den
        for g in range(A_GROUP):
            hd = kh * A_GROUP + g
            o_ref[0, :, hd * HEAD_DIM:(hd + 1) * HEAD_DIM] = o[g * WBLK:(g + 1) * WBLK].astype(BF16)


def _win_attn(sink, aq, ak, av, akc, avc):
    b, s, _ = aq.shape
    nb = s // WBLK
    c = akc.shape[1]
    kvw = A_KV_HEADS * HEAD_DIM
    prev = pl.BlockSpec((1, WBLK, kvw), lambda i, j: (i, jnp.maximum(j - 1, 0), 0))
    cur = pl.BlockSpec((1, WBLK, kvw), lambda i, j: (i, j, 0))
    nxt = pl.BlockSpec((1, WBLK, kvw), lambda i, j: (i, jnp.minimum(j + 1, nb - 1), 0))
    cx = pl.BlockSpec((1, c, kvw), lambda i, j: (i, 0, 0))
    qo = pl.BlockSpec((1, WBLK, A_HEADS * HEAD_DIM), lambda i, j: (i, j, 0))
    return pl.pallas_call(
        functools.partial(_win_kernel, seq=s),
        grid=(b, nb),
        in_specs=[pl.BlockSpec(memory_space=pltpu.SMEM), qo, prev, cur, nxt, prev, cur, nxt, cx, cx],
        out_specs=qo,
        out_shape=jax.ShapeDtypeStruct(aq.shape, BF16),
        compiler_params=_cparams(("arbitrary", "arbitrary")),
        name="win_attn",
    )(sink, aq, ak, ak, ak, av, av, av, akc, avc)


def _mla_kernel(q_ref, k_ref, v_ref, kx_ref, vx_ref, o_ref, s_a, s_b, acc_ref, *, tk):
    q = q_ref[0]
    dn = (((1,), (1,)), ((), ()))
    n_chunks = k_ref.shape[1] // tk
    s_bufs = (s_a, s_b)

    def scores_into(buf, c):
        s = lax.dot_general(q, k_ref[0, c * tk:(c + 1) * tk, :], dn, preferred_element_type=F32)
        buf[...] = s
        return jnp.max(s, axis=-1, keepdims=True)

    s0 = lax.dot_general(q, kx_ref[0], dn, preferred_element_type=F32)
    m = jnp.max(s0, axis=-1, keepdims=True)
    acc_ref[...] = jnp.dot(jnp.exp2(s0 - m).astype(BF16), vx_ref[0], preferred_element_type=F32)
    mx = scores_into(s_bufs[0], 0)
    for c in range(n_chunks):
        if c + 1 < n_chunks:
            mx_next = scores_into(s_bufs[(c + 1) % 2], c + 1)
        m_new = jnp.maximum(m, mx)
        p = jnp.exp2(s_bufs[c % 2][...] - m_new).astype(BF16)
        acc_ref[...] = (jnp.exp2(m - m_new) * acc_ref[...]
                        + jnp.dot(p, v_ref[0, c * tk:(c + 1) * tk, :], preferred_element_type=F32))
        m, mx = m_new, mx_next
    acc = acc_ref[...]
    o_ref[0] = (acc[:, :V_DIM] / acc[:, V_DIM:V_DIM + 1]).astype(BF16)


def _mla_attn(bq, bk, bv, bkc, bvc, tq, tk):
    b, s, _ = bq.shape
    c = bkc.shape[1]
    return pl.pallas_call(
        functools.partial(_mla_kernel, tk=tk),
        grid=(b, B_HEADS, s // tq),
        in_specs=[
            pl.BlockSpec((1, tq, QK_PAD), lambda i, h, j: (i, j, h)),
            pl.BlockSpec((1, s, QK_PAD), lambda i, h, j: (i, 0, h)),
            pl.BlockSpec((1, s, V_PAD), lambda i, h, j: (i, 0, h)),
            pl.BlockSpec((1, c, QK_PAD), lambda i, h, j: (i, 0, h)),
            pl.BlockSpec((1, c, V_PAD), lambda i, h, j: (i, 0, h)),
        ],
        out_specs=pl.BlockSpec((1, tq, V_DIM), lambda i, h, j: (i, j, h)),
        out_shape=jax.ShapeDtypeStruct((b, s, B_HEADS * V_DIM), BF16),
        scratch_shapes=[pltpu.VMEM((tq, tk), F32), pltpu.VMEM((tq, tk), F32), pltpu.VMEM((tq, V_PAD), F32)],
        compiler_params=_cparams(("arbitrary", "arbitrary", "arbitrary")),
        name="mla_attn",
    )(bq, bk, bv, bkc, bvc)


def _route(sel, aff):
    scores = []
    for g in range(N_GROUPS):
        r = sel[g * EXP_PER_GROUP:(g + 1) * EXP_PER_GROUP]
        best = None
        for a in range(EXP_PER_GROUP):
            for b in range(a + 1, EXP_PER_GROUP):
                pair = r[a] + r[b]
                best = pair if best is None else jnp.maximum(best, pair)
        scores.append(best)
    top, grp = scores[0], jnp.zeros_like(scores[0], dtype=I32)
    for g in range(1, N_GROUPS):
        take = scores[g] > top
        grp = jnp.where(take, g, grp)
        top = jnp.where(take, scores[g], top)
    masked = [jnp.where(grp == e // EXP_PER_GROUP, sel[e], -jnp.inf) for e in range(N_EXPERTS)]

    def argmax_first(vals, skip=None):
        bv = jnp.full_like(vals[0], -jnp.inf)
        bi = jnp.full_like(grp, -1)
        for e in range(N_EXPERTS):
            take = vals[e] > bv
            if skip is not None:
                take = take & (skip != e)
            bi = jnp.where(take, e, bi)
            bv = jnp.where(take, vals[e], bv)
        return bi

    i0 = argmax_first(masked)
    i1 = argmax_first(masked, skip=i0)
    a0 = jnp.zeros_like(aff[0])
    a1 = jnp.zeros_like(aff[0])
    for e in range(N_EXPERTS):
        a0 = jnp.where(i0 == e, aff[e], a0)
        a1 = jnp.where(i1 == e, aff[e], a1)
    tot = a0 + a1
    return i0, i1, a0 / tot, a1 / tot


def _out_kernel(*refs, n_lhs):
    lhs = refs[:n_lhs]
    ws = refs[n_lhs:2 * n_lhs]
    bias_ref, x_ref, mod_ref, gn_ref, wr_ref, br_ref, x1_ref, hp_ref, idx_ref, wts_ref = refs[2 * n_lhs:]
    y = bias_ref[...]
    for a, w in zip(lhs, ws):
        y = y + jnp.dot(a[0], w[...], preferred_element_type=F32)
    x1 = x_ref[0] + mod_ref[0, 2:3, :] * y
    x1_ref[0] = x1
    h2 = (_rms(x1) * gn_ref[...]) * (1.0 + mod_ref[0, 4:5, :]) + mod_ref[0, 3:4, :]
    hp_ref[0] = h2
    h_head = h2.astype(BF16)
    h_tail = (h2 - h_head.astype(F32)).astype(BF16)
    t = jnp.dot(h_head, wr_ref[...], preferred_element_type=F32)
    logits = (t[:, :LANES] + t[:, LANES:]) + jnp.dot(h_tail, wr_ref[:, :LANES], preferred_element_type=F32)
    lt = logits.T[:N_EXPERTS]
    aff_t = jax.nn.sigmoid(lt)
    sel_t = aff_t + br_ref[...]
    sel = [sel_t[e:e + 1] for e in range(N_EXPERTS)]
    aff = [aff_t[e:e + 1] for e in range(N_EXPERTS)]
    i0, i1, w0, w1 = _route(sel, aff)
    idx_ref[0] = jnp.concatenate([i0, i1], axis=0)
    wts_ref[0] = jnp.concatenate([w0, w1], axis=0)


def _out_proj(lhs, ws, bias, x, mod, gn, wr, br, tm):
    b, s, d = x.shape
    n_lhs = len(lhs)
    row = pl.BlockSpec((1, tm, d), lambda i, j: (i, j, 0))
    in_specs = ([pl.BlockSpec((1, tm, a.shape[-1]), lambda i, j: (i, j, 0)) for a in lhs]
                + [_resident(w.shape) for w in ws]
                + [_resident(bias.shape), row, pl.BlockSpec((1, 6, d), lambda i, j: (i, 0, 0)),
                   _resident(gn.shape), _resident(wr.shape), _resident(br.shape)])
    return pl.pallas_call(
        functools.partial(_out_kernel, n_lhs=n_lhs),
        grid=(b, s // tm),
        in_specs=in_specs,
        out_specs=[row, row,
                   pl.BlockSpec((1, TOP_K, tm), lambda i, j: (i, 0, j)),
                   pl.BlockSpec((1, TOP_K, tm), lambda i, j: (i, 0, j))],
        out_shape=[jax.ShapeDtypeStruct((b, s, d), F32), jax.ShapeDtypeStruct((b, s, d), F32),
                   jax.ShapeDtypeStruct((b, TOP_K, s), I32), jax.ShapeDtypeStruct((b, TOP_K, s), F32)],
        compiler_params=_cparams(("arbitrary", "arbitrary")),
        name="out_proj",
    )(*lhs, *ws, bias, x, mod, gn, wr, br)


def _dispatch(idx, blk):
    b, _, s = idx.shape
    t = b * s
    n_asg = t * TOP_K
    e = jnp.transpose(idx, (0, 2, 1)).reshape(n_asg)
    onehot = (e[:, None] == jnp.arange(N_EXPERTS, dtype=I32)[None, :]).astype(I32)
    csum = jnp.cumsum(onehot, axis=0)
    counts = csum[-1]
    rank = jnp.sum((csum - onehot) * onehot, axis=1)
    padded = (counts + blk - 1) // blk * blk
    pad_end = jnp.cumsum(padded)
    pad_start = pad_end - padded
    start = jnp.cumsum(counts) - counts
    dest = pad_start[e] + rank
    src_tok = (jnp.sort(e * n_asg + jnp.arange(n_asg, dtype=I32)) % n_asg) // TOP_K
    src_tok = jnp.concatenate([src_tok, jnp.zeros((blk,), src_tok.dtype)])
    n_blocks = -(-n_asg // blk) + N_EXPERTS
    n_used = pad_end[-1] // blk
    blk_ids = jnp.minimum(jnp.arange(n_blocks, dtype=I32), n_used - 1)
    blk_e = jnp.sum((blk_ids[:, None] * blk >= pad_end[None, :]).astype(I32), axis=1)
    blk_e = jnp.minimum(blk_e, N_EXPERTS - 1)
    blk_lo = start[blk_e] + blk_ids * blk - pad_start[blk_e]
    as_i32 = lambda a: a.astype(I32)
    return (as_i32(dest.reshape(t, TOP_K)), as_i32(src_tok), as_i32(blk_e), as_i32(blk_lo),
            as_i32(n_used.reshape(1)))


def _moe_kernel(blk_e_ref, lo_ref, n_used_ref, tok_ref, h_ref, wg_ref, wu_ref, wd_ref, o_ref, xbuf, sem, *, blk):
    i = pl.program_id(0)
    n_used = n_used_ref[0]
    slot = i % 2

    def gather(block, slot_, unrolled):
        lo = lo_ref[block]

        def row_copy(r):
            t = tok_ref[lo + r]
            pltpu.make_async_copy(h_ref.at[pl.ds(t, 1)], xbuf.at[slot_, pl.ds(r, 1)], sem.at[slot_]).start()

        if unrolled:
            for r in range(blk):
                row_copy(r)
        else:
            def body(r, _):
                row_copy(r)
                return 0
            lax.fori_loop(0, blk, body, 0, unroll=8)

    @pl.when(i == 0)
    def _():
        gather(0, 0, False)

    @pl.when(i + 1 < n_used)
    def _():
        gather(i + 1, 1 - slot, True)

    @pl.when(i < n_used)
    def _():
        pltpu.make_async_copy(xbuf.at[slot], xbuf.at[slot], sem.at[slot]).wait()
        xb = xbuf[slot].astype(BF16)
        g = jnp.dot(xb, wg_ref[0, 0], preferred_element_type=F32)
        u = jnp.dot(xb, wu_ref[0, 0], preferred_element_type=F32)
        a = (g * jax.nn.sigmoid(g) * u).astype(BF16)
        o_ref[...] = jnp.dot(a, wd_ref[0, 0], preferred_element_type=F32)

    @pl.when(i >= n_used)
    def _():
        o_ref[...] = jnp.zeros_like(o_ref)


def _moe(h2, src_tok, blk_e, blk_lo, n_used, wg, wu, wd, layer, blk):
    n_blocks = blk_e.shape[0]
    _, _, d, de = wg.shape
    grid_spec = pltpu.PrefetchScalarGridSpec(
        num_scalar_prefetch=4,
        grid=(n_blocks,),
        in_specs=[
            pl.BlockSpec(memory_space=pl.ANY),
            pl.BlockSpec((1, 1, d, de), lambda i, be, *_: (layer, be[i], 0, 0)),
            pl.BlockSpec((1, 1, d, de), lambda i, be, *_: (layer, be[i], 0, 0)),
            pl.BlockSpec((1, 1, de, d), lambda i, be, *_: (layer, be[i], 0, 0)),
        ],
        out_specs=pl.BlockSpec((blk, d), lambda i, *_: (i, 0)),
        scratch_shapes=[pltpu.VMEM((2, blk, d), F32), pltpu.SemaphoreType.DMA((2,))],
    )
    return pl.pallas_call(
        functools.partial(_moe_kernel, blk=blk),
        grid_spec=grid_spec,
        out_shape=jax.ShapeDtypeStruct((n_blocks * blk, d), F32),
        compiler_params=_cparams(("arbitrary",)),
        name="moe_ffn",
    )(blk_e, blk_lo, n_used, src_tok, h2, wg, wu, wd)


def _combine_kernel(dest_ref, yp_ref, x_ref, wt_ref, mod_ref, *rest, tm, nt, with_norm):
    if with_norm:
        gn_ref, modn_ref, o_ref, hn_ref, buf, sem = rest
    else:
        o_ref, buf, sem = rest
    i = pl.program_id(0)
    j = pl.program_id(1)
    step = i * nt + j
    slot = step % 2

    def gather(step_, slot_, unrolled):
        base = step_ * (tm * TOP_K)

        def row_copies(r):
            for k in range(TOP_K):
                pltpu.make_async_copy(yp_ref.at[pl.ds(dest_ref[base + r * TOP_K + k], 1)],
                                      buf.at[slot_, k, pl.ds(r, 1)], sem.at[slot_]).start()

        if unrolled:
            for r in range(tm):
                row_copies(r)
        else:
            def body(r, _):
                row_copies(r)
                return 0
            lax.fori_loop(0, tm, body, 0, unroll=8)

    @pl.when(step == 0)
    def _():
        gather(0, 0, False)

    @pl.when(step + 1 < pl.num_programs(0) * nt)
    def _():
        gather(step + 1, 1 - slot, True)

    pltpu.make_async_copy(buf.at[slot], buf.at[slot], sem.at[slot]).wait()
    w = wt_ref[0]
    y = buf[slot, 0] * w[:, 0:1] + buf[slot, 1] * w[:, 1:2]
    out = x_ref[0] + mod_ref[0, 5:6, :] * y
    o_ref[0] = out
    if with_norm:
        hn_ref[0] = ((_rms(out) * gn_ref[...]) * (1.0 + modn_ref[0, 1:2, :]) + modn_ref[0, 0:1, :]).astype(BF16)


def _combine(dest, yp, x1, wts, mod, tm, norm=None):
    b, s, d = x1.shape
    nt = s // tm
    with_norm = norm is not None
    row = pl.BlockSpec((1, tm, d), lambda i, j, ds: (i, j, 0))
    modspec = pl.BlockSpec((1, 6, d), lambda i, j, ds: (i, 0, 0))
    in_specs = [pl.BlockSpec(memory_space=pl.ANY), row,
                pl.BlockSpec((1, tm, TOP_K), lambda i, j, ds: (i, j, 0)), modspec]
    args = [yp, x1, wts, mod]
    out_shape = [jax.ShapeDtypeStruct((b, s, d), F32)]
    out_specs = [row]
    if with_norm:
        gn, modn = norm
        in_specs += [pl.BlockSpec(gn.shape, lambda i, j, ds: (0, 0)), modspec]
        args += [gn, modn]
        out_shape.append(jax.ShapeDtypeStruct((b, s, d), BF16))
        out_specs.append(row)
    grid_spec = pltpu.PrefetchScalarGridSpec(
        num_scalar_prefetch=1, grid=(b, nt), in_specs=in_specs, out_specs=out_specs,
        scratch_shapes=[pltpu.VMEM((2, TOP_K, tm, d), F32), pltpu.SemaphoreType.DMA((2,))])
    return pl.pallas_call(
        functools.partial(_combine_kernel, tm=tm, nt=nt, with_norm=with_norm),
        grid_spec=grid_spec,
        out_shape=out_shape,
        compiler_params=_cparams(("arbitrary", "arbitrary")),
        name="combine",
    )(dest.reshape(-1), *args)


def _dft_constants(seq):
    n1, n2 = DFT_N1, DFT_N2
    assert seq == n1 * n2 and n2 == n1 * n1
    c = np.arange(F_GROUP_DIM)
    ang = 2 * np.pi * np.outer(c, c) / F_GROUP_DIM
    fc = np.concatenate([np.cos(ang), -np.sin(ang)], axis=1)
    a, k2, m = np.meshgrid(np.arange(n1), np.arange(n2), np.arange(n2), indexing="ij")
    ang_a = -2 * np.pi * (k2 * (a + n1 * m) % seq) / seq
    tre, tim = np.cos(ang_a), np.sin(ang_a)
    ma = np.concatenate([np.concatenate([tre, -tim], axis=2), np.concatenate([tim, tre], axis=2)], axis=1)
    ang_b = -2 * np.pi * np.outer(np.arange(n1), np.arange(n1)) / n1
    eye = np.eye(n1)
    mb = np.concatenate([np.kron(np.cos(ang_b), eye), -np.kron(np.sin(ang_b), eye)], axis=1)
    mb = mb / math.sqrt(seq * F_GROUP_DIM)
    return tuple(jnp.asarray(t, F32).astype(BF16) for t in (fc, ma, mb))


def _fourier_kernel(h_ref, fc_ref, ma_ref, mb_ref, o_ref, z_ref, yre, yim):
    n1, n2, gd = DFT_N1, DFT_N2, F_GROUP_DIM
    rows = 512
    n_tiles = 2 * gd // LANES
    for r in range(0, h_ref.shape[1], rows):
        z = jnp.dot(h_ref[0, r:r + rows, :], fc_ref[...], preferred_element_type=F32)
        for t in range(n_tiles):
            z_ref[t, r:r + rows, :] = z[:, t * LANES:(t + 1) * LANES]
    for a in range(n1):
        zs = [z_ref[t, pl.ds(a, n2, stride=n1), :].astype(BF16) for t in range(n_tiles)]
        rhs = jnp.concatenate([jnp.concatenate(zs[:n_tiles // 2], axis=1),
                               jnp.concatenate(zs[n_tiles // 2:], axis=1)], axis=0)
        y = jnp.dot(ma_ref[a], rhs, preferred_element_type=F32).astype(BF16)
        yre[a] = y[:n2]
        yim[a] = y[n2:]
    for hi in range(n1):
        rhs = jnp.concatenate([yre[a, hi * n1:(hi + 1) * n1, :] for a in range(n1)]
                              + [yim[a, hi * n1:(hi + 1) * n1, :] for a in range(n1)], axis=0)
        out = jnp.dot(mb_ref[...], rhs, preferred_element_type=F32).astype(BF16)
        for k1 in range(n1):
            o_ref[0, k1 * n2 + hi * n1:k1 * n2 + (hi + 1) * n1, :] = out[k1 * n1:(k1 + 1) * n1]


def _fourier(hn, consts):
    b, s, d = hn.shape
    fc, ma, mb = consts
    gd = F_GROUP_DIM
    return pl.pallas_call(
        _fourier_kernel,
        grid=(b, F_GROUPS),
        in_specs=[pl.BlockSpec((1, s, gd), lambda i, g: (i, 0, g)),
                  _resident(fc.shape), _resident(ma.shape), _resident(mb.shape)],
        out_specs=pl.BlockSpec((1, s, gd), lambda i, g: (i, 0, g)),
        out_shape=jax.ShapeDtypeStruct((b, s, d), BF16),
        scratch_shapes=[pltpu.VMEM((2 * gd // LANES, s, LANES), F32),
                        pltpu.VMEM((DFT_N1, DFT_N2, gd), BF16), pltpu.VMEM((DFT_N1, DFT_N2, gd), BF16)],
        compiler_params=_cparams(("arbitrary", "arbitrary")),
        name="fourier",
    )(hn, fc, ma, mb)


def _rope_tables(seq):
    pos = np.arange(seq)
    row, col = (pos // GRID_W).astype(np.float64), (pos % GRID_W).astype(np.float64)

    def tables(width):
        half = width // 2
        quarter = half // 2
        freqs = ROPE_BASE ** (-np.arange(0, half, 2, dtype=np.float64) / half)
        lane = np.arange(LANES)
        ang = np.where((lane < half)[None, :], row[:, None], col[:, None]) * freqs[lane % quarter][None, :]
        live = (lane < width)[None, :]
        first = ((lane % half) < quarter)[None, :]
        cos = np.where(live, np.cos(ang), 1.0)
        sneg = np.where(live & first, -np.sin(ang), 0.0)
        spos = np.where(live & ~first, np.sin(ang), 0.0)
        return [jnp.asarray(t, F32) for t in (cos, sneg, spos)]

    return tables(HEAD_DIM) + tables(ROPE_DIM)


def _identity_tables(n):
    one, zero = jnp.ones((n, LANES), F32), jnp.zeros((n, LANES), F32)
    return [one, zero, zero, one, zero, zero]


def _pad_heads(w, lead):
    w = w.reshape(lead, B_HEADS, QK_DIM)
    return jnp.pad(w, ((0, 0), (0, 0), (0, QK_PAD - QK_DIM))).reshape(lead, B_HEADS * QK_PAD)


def kernel(x, c, ctx, c_ctx, w_ada, b_ada, g_norm, w_in, g_aqn, g_akn, g_bq_lat, w_bq_up, g_bkv_lat, w_bkv_up,
           g_bqn, g_bkn, sink, w_o_ab, w_fo, b_fo, w_router, b_router, w_gate, w_up, w_down):
    b, s, d = x.shape
    n_ctx = ctx.shape[1]

    crows = jnp.concatenate([c, c_ctx[None, :], jnp.zeros((8 - b - 1, d), F32)], axis=0)
    mods = _ada(crows, w_ada, b_ada).reshape(DEPTH, 8, 6, d)
    mod_lat = [mods[l, :b] for l in range(DEPTH)]
    mod_ctx = jnp.broadcast_to(mods[0, b][None], (b, 6, d))

    wr = jnp.pad(w_router, ((0, 0), (0, LANES - N_EXPERTS)))
    wr_head = wr.astype(BF16)
    wr = jnp.concatenate([wr_head, (wr - wr_head.astype(F32)).astype(BF16)], axis=1)
    br = b_router.reshape(N_EXPERTS, 1)
    wg, wu, wd = w_gate.astype(BF16), w_up.astype(BF16), w_down.astype(BF16)

    win = jnp.pad(w_in[0], ((0, 0), (0, IN_PAD - IN_WIDTH))).astype(BF16)
    wbq = _pad_heads(w_bq_up[0], Q_LORA).astype(BF16)
    wkv = w_bkv_up[0].reshape(KV_LORA, B_HEADS, NOPE_DIM + V_DIM)
    wbkv = jnp.concatenate([wkv[:, :, :NOPE_DIM].reshape(KV_LORA, -1), wkv[:, :, NOPE_DIM:].reshape(KV_LORA, -1)],
                           axis=1).astype(BF16)
    gains = (g_aqn[0][None], g_akn[0][None], g_bq_lat[0][None], g_bkv_lat[0][None],
             jnp.pad(g_bqn[0][None], ((0, 0), (0, QK_PAD - QK_DIM))),
             jnp.pad(g_bkn[0][None], ((0, 0), (0, QK_PAD - QK_DIM))))
    gn0 = g_norm[0, 0][None]
    aq, ak, av, bq, bk, bv = _proj(x, mod_lat[0], gn0, win, wbq, wbkv, *gains, _rope_tables(s), 256)
    _, akc, avc, _, bkc, bvc = _proj(ctx, mod_ctx, gn0, win, wbq, wbkv, *gains, _identity_tables(n_ctx), n_ctx)

    ya = _win_attn(sink[0], aq, ak, av, akc, avc)
    yb = _mla_attn(bq, bk, bv, bkc, bvc, 1024, 1024)

    wo = w_o_ab[0].astype(BF16)
    n_a = A_HEADS * HEAD_DIM
    x1, h2, idx, wts = _out_proj([ya, yb], [wo[:n_a], wo[n_a:]], jnp.zeros((1, d), F32), x, mod_lat[0],
                                 g_norm[0, 1][None], wr, br, 512)
    dest, *plan = _dispatch(idx, MOE_BLK)
    yp = _moe(h2.reshape(b * s, d), *plan, wg, wu, wd, 0, MOE_BLK)
    x2, hn = _combine(dest, yp, x1, jnp.transpose(wts, (0, 2, 1)), mod_lat[0], 256,
                      norm=(g_norm[1, 0][None], mod_lat[1]))

    f = _fourier(hn, _dft_constants(s))
    x3, h2, idx, wts = _out_proj([f], [w_fo[0].astype(BF16)], b_fo[0][None], x2, mod_lat[1],
                                 g_norm[1, 1][None], wr, br, 512)
    dest, *plan = _dispatch(idx, MOE_BLK)
    yp = _moe(h2.reshape(b * s, d), *plan, wg, wu, wd, 1, MOE_BLK)
    (x4,) = _combine(dest, yp, x3, jnp.transpose(wts, (0, 2, 1)), mod_lat[1], 256)
    return x4
```

```python
import functools
import math

import numpy as np
import jax
import jax.numpy as jnp
from jax import lax
from jax.experimental import pallas as pl
from jax.experimental.pallas import tpu as pltpu

F32 = jnp.float32
BF16 = jnp.bfloat16
I32 = jnp.int32
HIGHEST = lax.Precision.HIGHEST

D_MODEL = 2048
DEPTH = 2
GRID_W = 64
HEAD_DIM = 128
A_HEADS = 8
A_KV_HEADS = 2
A_GROUP = A_HEADS // A_KV_HEADS
WINDOW = 128
WBLK = 128
B_HEADS = 8
Q_LORA = 512
KV_LORA = 256
NOPE_DIM = 128
ROPE_DIM = 64
V_DIM = 128
QK_DIM = NOPE_DIM + ROPE_DIM
QK_PAD = 256
V_PAD = 256
IN_SPLITS = (A_HEADS * HEAD_DIM, A_KV_HEADS * HEAD_DIM, A_KV_HEADS * HEAD_DIM, Q_LORA, KV_LORA, ROPE_DIM)
IN_WIDTH = sum(IN_SPLITS)
IN_PAD = 2432
F_GROUPS = 8
F_GROUP_DIM = D_MODEL // F_GROUPS
N_EXPERTS = 16
N_GROUPS = 4
EXP_PER_GROUP = N_EXPERTS // N_GROUPS
TOP_K = 2
D_EXPERT = 1024
ROPE_BASE = 10000.0
EPS = 1e-6
LOG2E = math.log2(math.e)
LANES = 128

MOE_BLK = 512
WIN_Q = 4
DFT_N1 = 16
DFT_N2 = 256
VMEM_LIMIT = 56 * 1024 * 1024


def _cparams(sem, **kw):
    return pltpu.CompilerParams(dimension_semantics=sem, vmem_limit_bytes=VMEM_LIMIT, **kw)


def _resident(shape):
    nd = len(shape)
    return pl.BlockSpec(shape, lambda *_: (0,) * nd, pipeline_mode=pl.Buffered(1))


def _rms(t, width=None):
    n = t.shape[-1] if width is None else width
    ss = jnp.sum(t * t, axis=-1, keepdims=True)
    return t * lax.rsqrt(ss * (1.0 / n) + EPS)


def _rope(t, cos, sneg, spos, dist):
    n = t.shape[-1]
    return t * cos + pltpu.roll(t, n - dist, 1) * sneg + pltpu.roll(t, dist, 1) * spos


def _ada_kernel(c_ref, w_ref, b_ref, o_ref):
    c = c_ref[...]
    s = c * jax.nn.sigmoid(c)
    o_ref[0] = jnp.dot(s, w_ref[0], precision=HIGHEST, preferred_element_type=F32) + b_ref[0]


def _ada(crows, w_ada, b_ada):
    depth, d, n = w_ada.shape
    tn = 1024
    return pl.pallas_call(
        _ada_kernel,
        grid=(depth, n // tn),
        in_specs=[
            pl.BlockSpec((8, d), lambda l, j: (0, 0)),
            pl.BlockSpec((1, d, tn), lambda l, j: (l, 0, j)),
            pl.BlockSpec((1, 1, tn), lambda l, j: (l, 0, j)),
        ],
        out_specs=pl.BlockSpec((1, 8, tn), lambda l, j: (l, 0, j)),
        out_shape=jax.ShapeDtypeStruct((depth, 8, n), F32),
        compiler_params=_cparams(("arbitrary", "arbitrary")),
        name="ada",
    )(crows, w_ada, b_ada.reshape(depth, 1, n))


def _proj_kernel(x_ref, mod_ref, gn_ref, win_ref, wbq_ref, wbkv_ref, gaq_ref, gak_ref, gbql_ref, gbkvl_ref,
                 gbq_ref, gbk_ref, ca_ref, sna_ref, spa_ref, cb_ref, snb_ref, spb_ref,
                 aq_ref, ak_ref, av_ref, bq_ref, bk_ref, bv_ref):
    x = x_ref[0]
    shift = mod_ref[0, 0:1, :]
    scale = mod_ref[0, 1:2, :]
    hb = ((_rms(x) * gn_ref[...]) * (1.0 + scale) + shift).astype(BF16)

    def cols(w_ref, lhs, lo, width):
        return jnp.dot(lhs, w_ref[:, lo:lo + width], preferred_element_type=F32)

    ca, sna, spa = ca_ref[...], sna_ref[...], spa_ref[...]
    cb, snb, spb = cb_ref[...], snb_ref[...], spb_ref[...]
    a_scale = HEAD_DIM ** -0.5 * LOG2E
    b_scale = QK_DIM ** -0.5 * LOG2E
    pair = 2 * HEAD_DIM

    for hp in range(A_HEADS // 2):
        pp = cols(win_ref, hb, hp * pair, pair)
        for j in range(2):
            hd = 2 * hp + j
            t = _rms(pp[:, j * HEAD_DIM:(j + 1) * HEAD_DIM]) * gaq_ref[...]
            aq_ref[0, :, hd * HEAD_DIM:(hd + 1) * HEAD_DIM] = (_rope(t, ca, sna, spa, 32) * a_scale).astype(BF16)
    off = A_HEADS * HEAD_DIM
    pp = cols(win_ref, hb, off, A_KV_HEADS * HEAD_DIM)
    for kh in range(A_KV_HEADS):
        t = _rms(pp[:, kh * HEAD_DIM:(kh + 1) * HEAD_DIM]) * gak_ref[...]
        ak_ref[0, :, kh * HEAD_DIM:(kh + 1) * HEAD_DIM] = _rope(t, ca, sna, spa, 32).astype(BF16)
    off += A_KV_HEADS * HEAD_DIM
    ones_col = (lax.broadcasted_iota(I32, (hb.shape[0], V_PAD - V_DIM), 1) == 0).astype(BF16)
    pp = cols(win_ref, hb, off, A_KV_HEADS * HEAD_DIM)
    for kh in range(A_KV_HEADS):
        av_ref[0, :, kh * V_PAD:kh * V_PAD + HEAD_DIM] = pp[:, kh * HEAD_DIM:(kh + 1) * HEAD_DIM].astype(BF16)
        av_ref[0, :, kh * V_PAD + HEAD_DIM:(kh + 1) * V_PAD] = ones_col
    off += A_KV_HEADS * HEAD_DIM

    ql = (_rms(cols(win_ref, hb, off, Q_LORA)) * gbql_ref[...]).astype(BF16)
    off += Q_LORA
    for hd in range(B_HEADS):
        t = _rms(cols(wbq_ref, ql, hd * QK_PAD, QK_PAD), QK_DIM) * gbq_ref[...]
        bq_ref[0, :, hd * QK_PAD:hd * QK_PAD + NOPE_DIM] = (t[:, :NOPE_DIM] * b_scale).astype(BF16)
        bq_ref[0, :, hd * QK_PAD + NOPE_DIM:(hd + 1) * QK_PAD] = (
            _rope(t[:, NOPE_DIM:], cb, snb, spb, 16) * b_scale).astype(BF16)

    kvl = (_rms(cols(win_ref, hb, off, KV_LORA)) * gbkvl_ref[...]).astype(BF16)
    off += KV_LORA
    kr = cols(win_ref, hb, off, LANES)
    kr_ss = jnp.sum(kr * kr, axis=-1, keepdims=True)
    kr_rot = _rope(kr * gbk_ref[:, NOPE_DIM:], cb, snb, spb, 16)
    for hp in range(B_HEADS // 2):
        kn2 = cols(wbkv_ref, kvl, hp * pair, pair)
        v2 = cols(wbkv_ref, kvl, B_HEADS * NOPE_DIM + hp * pair, pair)
        for j in range(2):
            hd = 2 * hp + j
            kn = kn2[:, j * NOPE_DIM:(j + 1) * NOPE_DIM]
            ss = jnp.sum(kn * kn, axis=-1, keepdims=True) + kr_ss
            r = lax.rsqrt(ss * (1.0 / QK_DIM) + EPS)
            bk_ref[0, :, hd * QK_PAD:hd * QK_PAD + NOPE_DIM] = (kn * r * gbk_ref[:, :NOPE_DIM]).astype(BF16)
            bk_ref[0, :, hd * QK_PAD + NOPE_DIM:(hd + 1) * QK_PAD] = (kr_rot * r).astype(BF16)
            bv_ref[0, :, hd * V_PAD:hd * V_PAD + V_DIM] = v2[:, j * V_DIM:(j + 1) * V_DIM].astype(BF16)
            bv_ref[0, :, hd * V_PAD + V_DIM:(hd + 1) * V_PAD] = ones_col


def _proj(x, mod, gn, win, wbq, wbkv, gaq, gak, gbql, gbkvl, gbq, gbk, tabs, tm):
    b, s, d = x.shape
    row = lambda w: pl.BlockSpec((1, tm, w), lambda i, j: (i, j, 0))
    tab = pl.BlockSpec((tm, LANES), lambda i, j: (j, 0))
    widths = (A_HEADS * HEAD_DIM, A_KV_HEADS * HEAD_DIM, A_KV_HEADS * V_PAD,
              B_HEADS * QK_PAD, B_HEADS * QK_PAD, B_HEADS * V_PAD)
    return pl.pallas_call(
        _proj_kernel,
        grid=(b, s // tm),
        in_specs=[row(d), pl.BlockSpec((1, 6, d), lambda i, j: (i, 0, 0)), _resident(gn.shape),
                  _resident(win.shape), _resident(wbq.shape), _resident(wbkv.shape),
                  _resident(gaq.shape), _resident(gak.shape), _resident(gbql.shape), _resident(gbkvl.shape),
                  _resident(gbq.shape), _resident(gbk.shape)] + [tab] * 6,
        out_specs=[row(w) for w in widths],
        out_shape=[jax.ShapeDtypeStruct((b, s, w), BF16) for w in widths],
        compiler_params=_cparams(("arbitrary", "arbitrary")),
        name="proj",
    )(x, mod, gn, win, wbq, wbkv, gaq, gak, gbql, gbkvl, gbq, gbk, *tabs)


def _win_kernel(sink_ref, q_ref, kp_ref, kc_ref, kn_ref, vp_ref, vc_ref, vn_ref, kx_ref, vx_ref, o_ref, *, seq):
    j = pl.program_id(1)
    rows = A_GROUP * WBLK
    band = 3 * WBLK
    keys = band + kx_ref.shape[1]
    r_iota = lax.broadcasted_iota(I32, (rows, keys), 0)
    c_iota = lax.broadcasted_iota(I32, (rows, keys), 1)
    head_of_row = lax.broadcasted_iota(I32, (rows, 1), 0) // WBLK
    dn = (((1,), (1,)), ((), ()))

    def key_block(refs, t, cols):
        p_ref, c_ref, n_ref = refs
        if t == 0:
            return p_ref[0, :, cols]
        if t == WIN_Q + 1:
            return n_ref[0, :, cols]
        return c_ref[0, (t - 1) * WBLK:t * WBLK, cols]

    for sub in range(WIN_Q):
        n = j * WIN_Q + sub
        qpos = n * WBLK + (r_iota & (WBLK - 1))
        kpos = (n - 1) * WBLK + c_iota
        valid = ((jnp.abs(qpos - kpos) <= WINDOW) & (kpos >= 0) & (kpos < seq)) | (c_iota >= band)
        for kh in range(A_KV_HEADS):
            cs = slice(kh * HEAD_DIM, (kh + 1) * HEAD_DIM)
            vs = slice(kh * V_PAD, (kh + 1) * V_PAD)
            q = jnp.concatenate(
                [q_ref[0, sub * WBLK:(sub + 1) * WBLK, (kh * A_GROUP + g) * HEAD_DIM:(kh * A_GROUP + g + 1) * HEAD_DIM]
                 for g in range(A_GROUP)], axis=0)
            kb = jnp.concatenate([key_block((kp_ref, kc_ref, kn_ref), sub + t, cs) for t in range(3)]
                                 + [kx_ref[0, :, cs]], axis=0)
            vb = jnp.concatenate([key_block((vp_ref, vc_ref, vn_ref), sub + t, vs) for t in range(3)]
                                 + [vx_ref[0, :, vs]], axis=0)
            s = jnp.where(valid, lax.dot_general(q, kb, dn, preferred_element_type=F32), -jnp.inf)
            sink = jnp.zeros((rows, 1), F32)
            for g in range(A_GROUP):
                sink = jnp.where(head_of_row == g, sink_ref[kh * A_GROUP + g] * LOG2E, sink)
            m = jnp.maximum(jnp.max(s, axis=-1, keepdims=True), sink)
            acc = jnp.dot(jnp.exp2(s - m).astype(BF16), vb, preferred_element_type=F32)
            o = acc[:, :HEAD_DIM] / (acc[:, HEAD_DIM:HEAD_DIM + 1] + jnp.exp2(sink - m))
            for g in range(A_GROUP):
                hd = kh * A_GROUP + g
                o_ref[0, sub * WBLK:(sub + 1) * WBLK, hd * HEAD_DIM:(hd + 1) * HEAD_DIM] = (
                    o[g * WBLK:(g + 1) * WBLK].astype(BF16))


def _win_attn(sink, aq, ak, av, akc, avc):
    b, s, _ = aq.shape
    nb = s // WBLK
    c = akc.shape[1]
    tq = WIN_Q * WBLK
    prev = lambda w: pl.BlockSpec((1, WBLK, w), lambda i, j: (i, jnp.maximum(j * WIN_Q - 1, 0), 0))
    cur = lambda w: pl.BlockSpec((1, tq, w), lambda i, j: (i, j, 0))
    nxt = lambda w: pl.BlockSpec((1, WBLK, w), lambda i, j: (i, jnp.minimum((j + 1) * WIN_Q, nb - 1), 0))
    cx = lambda w: pl.BlockSpec((1, c, w), lambda i, j: (i, 0, 0))
    kw, vw = A_KV_HEADS * HEAD_DIM, A_KV_HEADS * V_PAD
    qo = pl.BlockSpec((1, tq, A_HEADS * HEAD_DIM), lambda i, j: (i, j, 0))
    return pl.pallas_call(
        functools.partial(_win_kernel, seq=s),
        grid=(b, nb // WIN_Q),
        in_specs=[pl.BlockSpec(memory_space=pltpu.SMEM), qo, prev(kw), cur(kw), nxt(kw), prev(vw), cur(vw), nxt(vw),
                  cx(kw), cx(vw)],
        out_specs=qo,
        out_shape=jax.ShapeDtypeStruct(aq.shape, BF16),
        compiler_params=_cparams(("arbitrary", "arbitrary")),
        name="win_attn",
    )(sink, aq, ak, ak, ak, av, av, av, akc, avc)


def _mla_kernel(q_ref, k_ref, v_ref, kx_ref, vx_ref, *rest, tk, n_side):
    side_in, o_ref, side_out = rest[:n_side], rest[n_side], rest[n_side + 1:2 * n_side + 1]
    s_a, s_b, acc_ref = rest[2 * n_side + 1:]
    for w_in, w_out in zip(side_in, side_out):
        w_out[...] = w_in[...].astype(BF16)
    q = q_ref[0]
    dn = (((1,), (1,)), ((), ()))
    n_chunks = k_ref.shape[1] // tk
    s_bufs = (s_a, s_b)

    def scores_into(buf, c):
        s = lax.dot_general(q, k_ref[0, c * tk:(c + 1) * tk, :], dn, preferred_element_type=F32)
        buf[...] = s
        return jnp.max(s, axis=-1, keepdims=True)

    s0 = lax.dot_general(q, kx_ref[0], dn, preferred_element_type=F32)
    m = jnp.max(s0, axis=-1, keepdims=True)
    acc_ref[...] = jnp.dot(jnp.exp2(s0 - m).astype(BF16), vx_ref[0], preferred_element_type=F32)
    mx = scores_into(s_bufs[0], 0)
    for c in range(n_chunks):
        if c + 1 < n_chunks:
            mx_next = scores_into(s_bufs[(c + 1) % 2], c + 1)
        m_new = jnp.maximum(m, mx)
        p = jnp.exp2(s_bufs[c % 2][...] - m_new).astype(BF16)
        acc_ref[...] = (jnp.exp2(m - m_new) * acc_ref[...]
                        + jnp.dot(p, v_ref[0, c * tk:(c + 1) * tk, :], preferred_element_type=F32))
        m, mx = m_new, mx_next
    acc = acc_ref[...]
    o_ref[0] = (acc[:, :V_DIM] / acc[:, V_DIM:V_DIM + 1]).astype(BF16)


def _mla_attn(bq, bk, bv, bkc, bvc, side, tq, tk):
    b, s, _ = bq.shape
    c = bkc.shape[1]
    nq = s // tq
    steps = b * B_HEADS * nq
    side2d = [w.reshape(-1, w.shape[-1]) for w in side]
    slab = lambda i, h, j: ((i * B_HEADS + h) * nq + j, 0)
    side_specs = [pl.BlockSpec((w.shape[0] // steps, w.shape[1]), slab) for w in side2d]
    res = pl.pallas_call(
        functools.partial(_mla_kernel, tk=tk, n_side=len(side)),
        grid=(b, B_HEADS, nq),
        in_specs=[
            pl.BlockSpec((1, tq, QK_PAD), lambda i, h, j: (i, j, h)),
            pl.BlockSpec((1, s, QK_PAD), lambda i, h, j: (i, 0, h)),
            pl.BlockSpec((1, s, V_PAD), lambda i, h, j: (i, 0, h)),
            pl.BlockSpec((1, c, QK_PAD), lambda i, h, j: (i, 0, h)),
            pl.BlockSpec((1, c, V_PAD), lambda i, h, j: (i, 0, h)),
        ] + side_specs,
        out_specs=[pl.BlockSpec((1, tq, V_DIM), lambda i, h, j: (i, j, h))] + side_specs,
        out_shape=[jax.ShapeDtypeStruct((b, s, B_HEADS * V_DIM), BF16)]
        + [jax.ShapeDtypeStruct(w.shape, BF16) for w in side2d],
        scratch_shapes=[pltpu.VMEM((tq, tk), F32), pltpu.VMEM((tq, tk), F32), pltpu.VMEM((tq, V_PAD), F32)],
        compiler_params=_cparams(("arbitrary", "arbitrary", "arbitrary")),
        name="mla_attn",
    )(bq, bk, bv, bkc, bvc, *side2d)
    return res[0], [o.reshape(w.shape) for o, w in zip(res[1:], side)]


def _route(sel, aff):
    scores = []
    for g in range(N_GROUPS):
        r = sel[g * EXP_PER_GROUP:(g + 1) * EXP_PER_GROUP]
        best = None
        for a in range(EXP_PER_GROUP):
            for b in range(a + 1, EXP_PER_GROUP):
                pair = r[a] + r[b]
                best = pair if best is None else jnp.maximum(best, pair)
        scores.append(best)
    top, grp = scores[0], jnp.zeros_like(scores[0], dtype=I32)
    for g in range(1, N_GROUPS):
        take = scores[g] > top
        grp = jnp.where(take, g, grp)
        top = jnp.where(take, scores[g], top)
    masked = [jnp.where(grp == e // EXP_PER_GROUP, sel[e], -jnp.inf) for e in range(N_EXPERTS)]

    def argmax_first(vals, skip=None):
        bv = jnp.full_like(vals[0], -jnp.inf)
        bi = jnp.full_like(grp, -1)
        for e in range(N_EXPERTS):
            take = vals[e] > bv
            if skip is not None:
                take = take & (skip != e)
            bi = jnp.where(take, e, bi)
            bv = jnp.where(take, vals[e], bv)
        return bi

    i0 = argmax_first(masked)
    i1 = argmax_first(masked, skip=i0)
    a0 = jnp.zeros_like(aff[0])
    a1 = jnp.zeros_like(aff[0])
    for e in range(N_EXPERTS):
        a0 = jnp.where(i0 == e, aff[e], a0)
        a1 = jnp.where(i1 == e, aff[e], a1)
    tot = a0 + a1
    return i0, i1, a0 / tot, a1 / tot


def _out_kernel(*refs, n_lhs):
    lhs = refs[:n_lhs]
    ws = refs[n_lhs:2 * n_lhs]
    bias_ref, x_ref, mod_ref, gn_ref, wr_ref, br_ref, x1_ref, hp_ref, idx_ref, wts_ref = refs[2 * n_lhs:]
    y = bias_ref[...]
    for a, w in zip(lhs, ws):
        y = y + jnp.dot(a[0], w[...], preferred_element_type=F32)
    x1 = x_ref[0] + mod_ref[0, 2:3, :] * y
    x1_ref[0] = x1
    h2 = (_rms(x1) * gn_ref[...]) * (1.0 + mod_ref[0, 4:5, :]) + mod_ref[0, 3:4, :]
    hp_ref[0] = h2
    w = wr_ref[...]
    w_head = w.astype(BF16)
    w_tail = (w - w_head.astype(F32)).astype(BF16)
    h_head = h2.astype(BF16)
    h_tail = (h2 - h_head.astype(F32)).astype(BF16)
    t = jnp.dot(h_head, jnp.concatenate([w_head, w_tail], axis=1), preferred_element_type=F32)
    logits = (t[:, :LANES] + t[:, LANES:]) + jnp.dot(h_tail, w_head, preferred_element_type=F32)
    lt = logits.T[:N_EXPERTS]
    aff_t = jax.nn.sigmoid(lt)
    sel_t = aff_t + br_ref[...]
    sel = [sel_t[e:e + 1] for e in range(N_EXPERTS)]
    aff = [aff_t[e:e + 1] for e in range(N_EXPERTS)]
    i0, i1, w0, w1 = _route(sel, aff)
    idx_ref[0] = jnp.concatenate([i0, i1], axis=0)
    wts_ref[0] = jnp.concatenate([w0, w1], axis=0)


def _out_proj(lhs, ws, bias, x, mod, gn, wr, br, tm):
    b, s, d = x.shape
    n_lhs = len(lhs)
    row = pl.BlockSpec((1, tm, d), lambda i, j: (i, j, 0))
    in_specs = ([pl.BlockSpec((1, tm, a.shape[-1]), lambda i, j: (i, j, 0)) for a in lhs]
                + [_resident(w.shape) for w in ws]
                + [_resident(bias.shape), row, pl.BlockSpec((1, 6, d), lambda i, j: (i, 0, 0)),
                   _resident(gn.shape), _resident(wr.shape), _resident(br.shape)])
    return pl.pallas_call(
        functools.partial(_out_kernel, n_lhs=n_lhs),
        grid=(b, s // tm),
        in_specs=in_specs,
        out_specs=[row, row,
                   pl.BlockSpec((1, TOP_K, tm), lambda i, j: (i, 0, j)),
                   pl.BlockSpec((1, TOP_K, tm), lambda i, j: (i, 0, j))],
        out_shape=[jax.ShapeDtypeStruct((b, s, d), F32), jax.ShapeDtypeStruct((b, s, d), F32),
                   jax.ShapeDtypeStruct((b, TOP_K, s), I32), jax.ShapeDtypeStruct((b, TOP_K, s), F32)],
        compiler_params=_cparams(("arbitrary", "arbitrary")),
        name="out_proj",
    )(*lhs, *ws, bias, x, mod, gn, wr, br)


def _dispatch(idx, blk):
    b, _, s = idx.shape
    t = b * s
    n_asg = t * TOP_K
    e = jnp.transpose(idx, (0, 2, 1)).reshape(n_asg)
    onehot = (e[:, None] == jnp.arange(N_EXPERTS, dtype=I32)[None, :]).astype(I32)
    csum = jnp.cumsum(onehot, axis=0)
    counts = csum[-1]
    rank = jnp.sum((csum - onehot) * onehot, axis=1)
    padded = (counts + blk - 1) // blk * blk
    pad_end = jnp.cumsum(padded)
    pad_start = pad_end - padded
    start = jnp.cumsum(counts) - counts
    dest = pad_start[e] + rank
    src_tok = (jnp.sort(e * n_asg + jnp.arange(n_asg, dtype=I32)) % n_asg) // TOP_K
    src_tok = jnp.concatenate([src_tok, jnp.zeros((blk,), src_tok.dtype)])
    n_blocks = -(-n_asg // blk) + N_EXPERTS
    n_used = pad_end[-1] // blk
    blk_ids = jnp.minimum(jnp.arange(n_blocks, dtype=I32), n_used - 1)
    blk_e = jnp.sum((blk_ids[:, None] * blk >= pad_end[None, :]).astype(I32), axis=1)
    blk_e = jnp.minimum(blk_e, N_EXPERTS - 1)
    blk_lo = start[blk_e] + blk_ids * blk - pad_start[blk_e]
    as_i32 = lambda a: a.astype(I32)
    return (as_i32(dest.reshape(t, TOP_K)), as_i32(src_tok), as_i32(blk_e), as_i32(blk_lo),
            as_i32(n_used.reshape(1)))


def _moe_kernel(blk_e_ref, lo_ref, n_used_ref, tok_ref, h_ref, wg_ref, wu_ref, wd_ref, o_ref, xbuf, sem, *, blk):
    i = pl.program_id(0)
    n_used = n_used_ref[0]
    slot = i % 2

    def gather(block, slot_, unrolled):
        lo = lo_ref[block]

        def row_copy(r):
            t = tok_ref[lo + r]
            pltpu.make_async_copy(h_ref.at[pl.ds(t, 1)], xbuf.at[slot_, pl.ds(r, 1)], sem.at[slot_]).start()

        if unrolled:
            for r in range(blk):
                row_copy(r)
        else:
            def body(r, _):
                row_copy(r)
                return 0
            lax.fori_loop(0, blk, body, 0, unroll=8)

    @pl.when(i == 0)
    def _():
        gather(0, 0, False)

    @pl.when(i + 1 < n_used)
    def _():
        gather(i + 1, 1 - slot, True)

    @pl.when(i < n_used)
    def _():
        pltpu.make_async_copy(xbuf.at[slot], xbuf.at[slot], sem.at[slot]).wait()
        xb = xbuf[slot].astype(BF16)
        g = jnp.dot(xb, wg_ref[0, 0], preferred_element_type=F32)
        u = jnp.dot(xb, wu_ref[0, 0], preferred_element_type=F32)
        a = (g * jax.nn.sigmoid(g) * u).astype(BF16)
        o_ref[...] = jnp.dot(a, wd_ref[0, 0], preferred_element_type=F32)

    @pl.when(i >= n_used)
    def _():
        o_ref[...] = jnp.zeros_like(o_ref)


def _moe(h2, src_tok, blk_e, blk_lo, n_used, wg, wu, wd, layer, blk):
    n_blocks = blk_e.shape[0]
    _, _, d, de = wg.shape
    grid_spec = pltpu.PrefetchScalarGridSpec(
        num_scalar_prefetch=4,
        grid=(n_blocks,),
        in_specs=[
            pl.BlockSpec(memory_space=pl.ANY),
            pl.BlockSpec((1, 1, d, de), lambda i, be, *_: (layer, be[i], 0, 0)),
            pl.BlockSpec((1, 1, d, de), lambda i, be, *_: (layer, be[i], 0, 0)),
            pl.BlockSpec((1, 1, de, d), lambda i, be, *_: (layer, be[i], 0, 0)),
        ],
        out_specs=pl.BlockSpec((blk, d), lambda i, *_: (i, 0)),
        scratch_shapes=[pltpu.VMEM((2, blk, d), F32), pltpu.SemaphoreType.DMA((2,))],
    )
    return pl.pallas_call(
        functools.partial(_moe_kernel, blk=blk),
        grid_spec=grid_spec,
        out_shape=jax.ShapeDtypeStruct((n_blocks * blk, d), F32),
        compiler_params=_cparams(("arbitrary",)),
        name="moe_ffn",
    )(blk_e, blk_lo, n_used, src_tok, h2, wg, wu, wd)


def _combine_kernel(dest_ref, yp_ref, x_ref, wt_ref, mod_ref, *rest, tm, nt, with_norm):
    if with_norm:
        gn_ref, modn_ref, o_ref, hn_ref, buf, sem = rest
    else:
        o_ref, buf, sem = rest
    i = pl.program_id(0)
    j = pl.program_id(1)
    step = i * nt + j
    slot = step % 2

    def gather(step_, slot_, unrolled):
        base = step_ * (tm * TOP_K)

        def row_copies(r):
            for k in range(TOP_K):
                pltpu.make_async_copy(yp_ref.at[pl.ds(dest_ref[base + r * TOP_K + k], 1)],
                                      buf.at[slot_, k, pl.ds(r, 1)], sem.at[slot_]).start()

        if unrolled:
            for r in range(tm):
                row_copies(r)
        else:
            def body(r, _):
                row_copies(r)
                return 0
            lax.fori_loop(0, tm, body, 0, unroll=8)

    @pl.when(step == 0)
    def _():
        gather(0, 0, False)

    @pl.when(step + 1 < pl.num_programs(0) * nt)
    def _():
        gather(step + 1, 1 - slot, True)

    pltpu.make_async_copy(buf.at[slot], buf.at[slot], sem.at[slot]).wait()
    w = wt_ref[0]
    y = buf[slot, 0] * w[:, 0:1] + buf[slot, 1] * w[:, 1:2]
    out = x_ref[0] + mod_ref[0, 5:6, :] * y
    o_ref[0] = out
    if with_norm:
        hn_ref[0] = ((_rms(out) * gn_ref[...]) * (1.0 + modn_ref[0, 1:2, :]) + modn_ref[0, 0:1, :]).astype(BF16)


def _combine(dest, yp, x1, wts, mod, tm, norm=None):
    b, s, d = x1.shape
    nt = s // tm
    with_norm = norm is not None
    row = pl.BlockSpec((1, tm, d), lambda i, j, ds: (i, j, 0))
    modspec = pl.BlockSpec((1, 6, d), lambda i, j, ds: (i, 0, 0))
    in_specs = [pl.BlockSpec(memory_space=pl.ANY), row,
                pl.BlockSpec((1, tm, TOP_K), lambda i, j, ds: (i, j, 0)), modspec]
    args = [yp, x1, wts, mod]
    out_shape = [jax.ShapeDtypeStruct((b, s, d), F32)]
    out_specs = [row]
    if with_norm:
        gn, modn = norm
        in_specs += [pl.BlockSpec(gn.shape, lambda i, j, ds: (0, 0)), modspec]
        args += [gn, modn]
        out_shape.append(jax.ShapeDtypeStruct((b, s, d), BF16))
        out_specs.append(row)
    grid_spec = pltpu.PrefetchScalarGridSpec(
        num_scalar_prefetch=1, grid=(b, nt), in_specs=in_specs, out_specs=out_specs,
        scratch_shapes=[pltpu.VMEM((2, TOP_K, tm, d), F32), pltpu.SemaphoreType.DMA((2,))])
    return pl.pallas_call(
        functools.partial(_combine_kernel, tm=tm, nt=nt, with_norm=with_norm),
        grid_spec=grid_spec,
        out_shape=out_shape,
        compiler_params=_cparams(("arbitrary", "arbitrary")),
        name="combine",
    )(dest.reshape(-1), *args)


def _dft_constants(seq):
    n1, n2 = DFT_N1, DFT_N2
    assert seq == n1 * n2 and n2 == n1 * n1
    c = np.arange(F_GROUP_DIM)
    ang = 2 * np.pi * np.outer(c, c) / F_GROUP_DIM
    fc = np.concatenate([np.cos(ang), -np.sin(ang)], axis=1)
    a, k2, m = np.meshgrid(np.arange(n1), np.arange(n2), np.arange(n2), indexing="ij")
    ang_a = -2 * np.pi * (k2 * (a + n1 * m) % seq) / seq
    tre, tim = np.cos(ang_a), np.sin(ang_a)
    ma = np.concatenate([np.concatenate([tre, -tim], axis=2), np.concatenate([tim, tre], axis=2)], axis=1)
    ang_b = -2 * np.pi * np.outer(np.arange(n1), np.arange(n1)) / n1
    eye = np.eye(n1)
    mb = np.concatenate([np.kron(np.cos(ang_b), eye), -np.kron(np.sin(ang_b), eye)], axis=1)
    mb = mb / math.sqrt(seq * F_GROUP_DIM)
    return tuple(jnp.asarray(t, F32).astype(BF16) for t in (fc, ma, mb))


def _fourier_kernel(h_ref, fc_ref, ma_ref, mb_ref, o_ref, z_ref, yre, yim):
    n1, n2, gd = DFT_N1, DFT_N2, F_GROUP_DIM
    rows = 512
    n_tiles = 2 * gd // LANES
    for r in range(0, h_ref.shape[1], rows):
        z = jnp.dot(h_ref[0, r:r + rows, :], fc_ref[...], preferred_element_type=F32)
        for t in range(n_tiles):
            z_ref[t, r:r + rows, :] = z[:, t * LANES:(t + 1) * LANES]
    for a in range(n1):
        zs = [z_ref[t, pl.ds(a, n2, stride=n1), :].astype(BF16) for t in range(n_tiles)]
        rhs = jnp.concatenate([jnp.concatenate(zs[:n_tiles // 2], axis=1),
                               jnp.concatenate(zs[n_tiles // 2:], axis=1)], axis=0)
        y = jnp.dot(ma_ref[a], rhs, preferred_element_type=F32).astype(BF16)
        yre[a] = y[:n2]
        yim[a] = y[n2:]
    for hi in range(n1):
        rhs = jnp.concatenate([yre[a, hi * n1:(hi + 1) * n1, :] for a in range(n1)]
                              + [yim[a, hi * n1:(hi + 1) * n1, :] for a in range(n1)], axis=0)
        out = jnp.dot(mb_ref[...], rhs, preferred_element_type=F32).astype(BF16)
        for k1 in range(n1):
            o_ref[0, k1 * n2 + hi * n1:k1 * n2 + (hi + 1) * n1, :] = out[k1 * n1:(k1 + 1) * n1]


def _fourier(hn, consts):
    b, s, d = hn.shape
    fc, ma, mb = consts
    gd = F_GROUP_DIM
    return pl.pallas_call(
        _fourier_kernel,
        grid=(b, F_GROUPS),
        in_specs=[pl.BlockSpec((1, s, gd), lambda i, g: (i, 0, g)),
                  _resident(fc.shape), _resident(ma.shape), _resident(mb.shape)],
        out_specs=pl.BlockSpec((1, s, gd), lambda i, g: (i, 0, g)),
        out_shape=jax.ShapeDtypeStruct((b, s, d), BF16),
        scratch_shapes=[pltpu.VMEM((2 * gd // LANES, s, LANES), F32),
                        pltpu.VMEM((DFT_N1, DFT_N2, gd), BF16), pltpu.VMEM((DFT_N1, DFT_N2, gd), BF16)],
        compiler_params=_cparams(("arbitrary", "arbitrary")),
        name="fourier",
    )(hn, fc, ma, mb)


def _rope_tables(seq):
    pos = np.arange(seq)
    row, col = (pos // GRID_W).astype(np.float64), (pos % GRID_W).astype(np.float64)

    def tables(width):
        half = width // 2
        quarter = half // 2
        freqs = ROPE_BASE ** (-np.arange(0, half, 2, dtype=np.float64) / half)
        lane = np.arange(LANES)
        ang = np.where((lane < half)[None, :], row[:, None], col[:, None]) * freqs[lane % quarter][None, :]
        live = (lane < width)[None, :]
        first = ((lane % half) < quarter)[None, :]
        cos = np.where(live, np.cos(ang), 1.0)
        sneg = np.where(live & first, -np.sin(ang), 0.0)
        spos = np.where(live & ~first, np.sin(ang), 0.0)
        return [jnp.asarray(t, F32) for t in (cos, sneg, spos)]

    return tables(HEAD_DIM) + tables(ROPE_DIM)


def _identity_tables(n):
    one, zero = jnp.ones((n, LANES), F32), jnp.zeros((n, LANES), F32)
    return [one, zero, zero, one, zero, zero]


def _pad_heads(w, lead):
    w = w.reshape(lead, B_HEADS, QK_DIM)
    return jnp.pad(w, ((0, 0), (0, 0), (0, QK_PAD - QK_DIM))).reshape(lead, B_HEADS * QK_PAD)


def kernel(x, c, ctx, c_ctx, w_ada, b_ada, g_norm, w_in, g_aqn, g_akn, g_bq_lat, w_bq_up, g_bkv_lat, w_bkv_up,
           g_bqn, g_bkn, sink, w_o_ab, w_fo, b_fo, w_router, b_router, w_gate, w_up, w_down):
    b, s, d = x.shape
    n_ctx = ctx.shape[1]

    crows = jnp.concatenate([c, c_ctx[None, :], jnp.zeros((8 - b - 1, d), F32)], axis=0)
    mods = _ada(crows, w_ada, b_ada).reshape(DEPTH, 8, 6, d)
    mod_lat = [mods[l, :b] for l in range(DEPTH)]
    mod_ctx = jnp.broadcast_to(mods[0, b][None], (b, 6, d))

    wr = jnp.pad(w_router, ((0, 0), (0, LANES - N_EXPERTS)))
    br = b_router.reshape(N_EXPERTS, 1)

    win = jnp.pad(w_in[0], ((0, 0), (0, IN_PAD - IN_WIDTH))).astype(BF16)
    wbq = _pad_heads(w_bq_up[0], Q_LORA).astype(BF16)
    wkv = w_bkv_up[0].reshape(KV_LORA, B_HEADS, NOPE_DIM + V_DIM)
    wbkv = jnp.concatenate([wkv[:, :, :NOPE_DIM].reshape(KV_LORA, -1), wkv[:, :, NOPE_DIM:].reshape(KV_LORA, -1)],
                           axis=1).astype(BF16)
    gains = (g_aqn[0][None], g_akn[0][None], g_bq_lat[0][None], g_bkv_lat[0][None],
             jnp.pad(g_bqn[0][None], ((0, 0), (0, QK_PAD - QK_DIM))),
             jnp.pad(g_bkn[0][None], ((0, 0), (0, QK_PAD - QK_DIM))))
    gn0 = g_norm[0, 0][None]
    aq, ak, av, bq, bk, bv = _proj(x, mod_lat[0], gn0, win, wbq, wbkv, *gains, _rope_tables(s), 256)
    _, akc, avc, _, bkc, bvc = _proj(ctx, mod_ctx, gn0, win, wbq, wbkv, *gains, _identity_tables(n_ctx), n_ctx)

    ya = _win_attn(sink[0], aq, ak, av, akc, avc)
    yb, (wg, wu, wd) = _mla_attn(bq, bk, bv, bkc, bvc, (w_gate, w_up, w_down), 1024, 1024)

    wo = w_o_ab[0].astype(BF16)
    n_a = A_HEADS * HEAD_DIM
    x1, h2, idx, wts = _out_proj([ya, yb], [wo[:n_a], wo[n_a:]], jnp.zeros((1, d), F32), x, mod_lat[0],
                                 g_norm[0, 1][None], wr, br, 512)
    dest, *plan = _dispatch(idx, MOE_BLK)
    yp = _moe(h2.reshape(b * s, d), *plan, wg, wu, wd, 0, MOE_BLK)
    x2, hn = _combine(dest, yp, x1, jnp.transpose(wts, (0, 2, 1)), mod_lat[0], 256,
                      norm=(g_norm[1, 0][None], mod_lat[1]))

    f = _fourier(hn, _dft_constants(s))
    x3, h2, idx, wts = _out_proj([f], [w_fo[0].astype(BF16)], b_fo[0][None], x2, mod_lat[1],
                                 g_norm[1, 1][None], wr, br, 512)
    dest, *plan = _dispatch(idx, MOE_BLK)
    yp = _moe(h2.reshape(b * s, d), *plan, wg, wu, wd, 1, MOE_BLK)
    (x4,) = _combine(dest, yp, x3, jnp.transpose(wts, (0, 2, 1)), mod_lat[1], 256)
    return x4
```

```python
import functools
import math

import numpy as np
import jax
import jax.numpy as jnp
from jax import lax
from jax.experimental import pallas as pl
from jax.experimental.pallas import tpu as pltpu

F32 = jnp.float32
BF16 = jnp.bfloat16
I32 = jnp.int32
HIGHEST = lax.Precision.HIGHEST

D_MODEL = 2048
DEPTH = 2
GRID_W = 64
HEAD_DIM = 128
A_HEADS = 8
A_KV_HEADS = 2
A_GROUP = A_HEADS // A_KV_HEADS
WINDOW = 128
WBLK = 128
B_HEADS = 8
Q_LORA = 512
KV_LORA = 256
NOPE_DIM = 128
ROPE_DIM = 64
V_DIM = 128
QK_DIM = NOPE_DIM + ROPE_DIM
QK_PAD = 256
V_PAD = 256
IN_SPLITS = (A_HEADS * HEAD_DIM, A_KV_HEADS * HEAD_DIM, A_KV_HEADS * HEAD_DIM, Q_LORA, KV_LORA, ROPE_DIM)
IN_WIDTH = sum(IN_SPLITS)
IN_PAD = 2432
F_GROUPS = 8
F_GROUP_DIM = D_MODEL // F_GROUPS
N_EXPERTS = 16
N_GROUPS = 4
EXP_PER_GROUP = N_EXPERTS // N_GROUPS
TOP_K = 2
D_EXPERT = 1024
ROPE_BASE = 10000.0
EPS = 1e-6
LOG2E = math.log2(math.e)
LANES = 128

MOE_BLK = 512
WIN_Q = 4
DFT_N1 = 16
DFT_N2 = 256
VMEM_LIMIT = 56 * 1024 * 1024


def _cparams(sem, **kw):
    return pltpu.CompilerParams(dimension_semantics=sem, vmem_limit_bytes=VMEM_LIMIT, **kw)


def _resident(shape):
    nd = len(shape)
    return pl.BlockSpec(shape, lambda *_: (0,) * nd, pipeline_mode=pl.Buffered(1))


def _rms(t, width=None):
    n = t.shape[-1] if width is None else width
    ss = jnp.sum(t * t, axis=-1, keepdims=True)
    return t * lax.rsqrt(ss * (1.0 / n) + EPS)


def _rope(t, cos, sneg, spos, dist):
    n = t.shape[-1]
    return t * cos + pltpu.roll(t, n - dist, 1) * sneg + pltpu.roll(t, dist, 1) * spos


def _ada_kernel(c_ref, w_ref, b_ref, o_ref):
    c = c_ref[...]
    s = c * jax.nn.sigmoid(c)
    o_ref[0] = jnp.dot(s, w_ref[0], precision=HIGHEST, preferred_element_type=F32) + b_ref[0]


def _ada(crows, w_ada, b_ada):
    depth, d, n = w_ada.shape
    tn = 2048
    return pl.pallas_call(
        _ada_kernel,
        grid=(depth, n // tn),
        in_specs=[
            pl.BlockSpec((8, d), lambda l, j: (0, 0)),
            pl.BlockSpec((1, d, tn), lambda l, j: (l, 0, j)),
            pl.BlockSpec((1, 1, tn), lambda l, j: (l, 0, j)),
        ],
        out_specs=pl.BlockSpec((1, 8, tn), lambda l, j: (l, 0, j)),
        out_shape=jax.ShapeDtypeStruct((depth, 8, n), F32),
        compiler_params=_cparams(("arbitrary", "arbitrary")),
        name="ada",
    )(crows, w_ada, b_ada.reshape(depth, 1, n))


def _proj_kernel(x_ref, mod_ref, gn_ref, win_ref, wbq_ref, wbkv_ref, gaq_ref, gak_ref, gbql_ref, gbkvl_ref,
                 gbq_ref, gbk_ref, ca_ref, sna_ref, spa_ref, cb_ref, snb_ref, spb_ref,
                 aq_ref, ak_ref, av_ref, bq_ref, bk_ref, bv_ref):
    x = x_ref[0]
    shift = mod_ref[0, 0:1, :]
    scale = mod_ref[0, 1:2, :]
    hb = ((_rms(x) * gn_ref[...]) * (1.0 + scale) + shift).astype(BF16)

    def cols(w_ref, lhs, lo, width):
        return jnp.dot(lhs, w_ref[:, lo:lo + width], preferred_element_type=F32)

    ca, sna, spa = ca_ref[...], sna_ref[...], spa_ref[...]
    cb, snb, spb = cb_ref[...], snb_ref[...], spb_ref[...]
    a_scale = HEAD_DIM ** -0.5 * LOG2E
    b_scale = QK_DIM ** -0.5 * LOG2E
    pair = 2 * HEAD_DIM

    for hp in range(A_HEADS // 2):
        pp = cols(win_ref, hb, hp * pair, pair)
        for j in range(2):
            hd = 2 * hp + j
            t = _rms(pp[:, j * HEAD_DIM:(j + 1) * HEAD_DIM]) * gaq_ref[...]
            aq_ref[0, :, hd * HEAD_DIM:(hd + 1) * HEAD_DIM] = (_rope(t, ca, sna, spa, 32) * a_scale).astype(BF16)
    off = A_HEADS * HEAD_DIM
    pp = cols(win_ref, hb, off, A_KV_HEADS * HEAD_DIM)
    for kh in range(A_KV_HEADS):
        t = _rms(pp[:, kh * HEAD_DIM:(kh + 1) * HEAD_DIM]) * gak_ref[...]
        ak_ref[0, :, kh * HEAD_DIM:(kh + 1) * HEAD_DIM] = _rope(t, ca, sna, spa, 32).astype(BF16)
    off += A_KV_HEADS * HEAD_DIM
    ones_col = (lax.broadcasted_iota(I32, (hb.shape[0], V_PAD - V_DIM), 1) == 0).astype(BF16)
    pp = cols(win_ref, hb, off, A_KV_HEADS * HEAD_DIM)
    for kh in range(A_KV_HEADS):
        av_ref[0, :, kh * V_PAD:kh * V_PAD + HEAD_DIM] = pp[:, kh * HEAD_DIM:(kh + 1) * HEAD_DIM].astype(BF16)
        av_ref[0, :, kh * V_PAD + HEAD_DIM:(kh + 1) * V_PAD] = ones_col
    off += A_KV_HEADS * HEAD_DIM

    ql = (_rms(cols(win_ref, hb, off, Q_LORA)) * gbql_ref[...]).astype(BF16)
    off += Q_LORA
    for hd in range(B_HEADS):
        t = _rms(cols(wbq_ref, ql, hd * QK_PAD, QK_PAD), QK_DIM) * gbq_ref[...]
        bq_ref[0, :, hd * QK_PAD:hd * QK_PAD + NOPE_DIM] = (t[:, :NOPE_DIM] * b_scale).astype(BF16)
        bq_ref[0, :, hd * QK_PAD + NOPE_DIM:(hd + 1) * QK_PAD] = (
            _rope(t[:, NOPE_DIM:], cb, snb, spb, 16) * b_scale).astype(BF16)

    kvl = (_rms(cols(win_ref, hb, off, KV_LORA)) * gbkvl_ref[...]).astype(BF16)
    off += KV_LORA
    kr = cols(win_ref, hb, off, LANES)
    kr_ss = jnp.sum(kr * kr, axis=-1, keepdims=True)
    kr_rot = _rope(kr * gbk_ref[:, NOPE_DIM:], cb, snb, spb, 16)
    for hp in range(B_HEADS // 2):
        kn2 = cols(wbkv_ref, kvl, hp * pair, pair)
        v2 = cols(wbkv_ref, kvl, B_HEADS * NOPE_DIM + hp * pair, pair)
        for j in range(2):
            hd = 2 * hp + j
            kn = kn2[:, j * NOPE_DIM:(j + 1) * NOPE_DIM]
            ss = jnp.sum(kn * kn, axis=-1, keepdims=True) + kr_ss
            r = lax.rsqrt(ss * (1.0 / QK_DIM) + EPS)
            bk_ref[0, :, hd * QK_PAD:hd * QK_PAD + NOPE_DIM] = (kn * r * gbk_ref[:, :NOPE_DIM]).astype(BF16)
            bk_ref[0, :, hd * QK_PAD + NOPE_DIM:(hd + 1) * QK_PAD] = (kr_rot * r).astype(BF16)
            bv_ref[0, :, hd * V_PAD:hd * V_PAD + V_DIM] = v2[:, j * V_DIM:(j + 1) * V_DIM].astype(BF16)
            bv_ref[0, :, hd * V_PAD + V_DIM:(hd + 1) * V_PAD] = ones_col


def _proj(x, mod, gn, win, wbq, wbkv, gaq, gak, gbql, gbkvl, gbq, gbk, tabs, tm):
    b, s, d = x.shape
    row = lambda w: pl.BlockSpec((1, tm, w), lambda i, j: (i, j, 0))
    tab = pl.BlockSpec((tm, LANES), lambda i, j: (j, 0))
    widths = (A_HEADS * HEAD_DIM, A_KV_HEADS * HEAD_DIM, A_KV_HEADS * V_PAD,
              B_HEADS * QK_PAD, B_HEADS * QK_PAD, B_HEADS * V_PAD)
    return pl.pallas_call(
        _proj_kernel,
        grid=(b, s // tm),
        in_specs=[row(d), pl.BlockSpec((1, 6, d), lambda i, j: (i, 0, 0)), _resident(gn.shape),
                  _resident(win.shape), _resident(wbq.shape), _resident(wbkv.shape),
                  _resident(gaq.shape), _resident(gak.shape), _resident(gbql.shape), _resident(gbkvl.shape),
                  _resident(gbq.shape), _resident(gbk.shape)] + [tab] * 6,
        out_specs=[row(w) for w in widths],
        out_shape=[jax.ShapeDtypeStruct((b, s, w), BF16) for w in widths],
        compiler_params=_cparams(("arbitrary", "arbitrary")),
        name="proj",
    )(x, mod, gn, win, wbq, wbkv, gaq, gak, gbql, gbkvl, gbq, gbk, *tabs)


def _win_kernel(sink_ref, q_ref, kp_ref, kc_ref, kn_ref, vp_ref, vc_ref, vn_ref, kx_ref, vx_ref, o_ref, *, seq):
    j = pl.program_id(1)
    rows = A_GROUP * WBLK
    band = 3 * WBLK
    keys = band + kx_ref.shape[1]
    r_iota = lax.broadcasted_iota(I32, (rows, keys), 0)
    c_iota = lax.broadcasted_iota(I32, (rows, keys), 1)
    head_of_row = lax.broadcasted_iota(I32, (rows, 1), 0) // WBLK
    dn = (((1,), (1,)), ((), ()))

    def key_block(refs, t, cols):
        p_ref, c_ref, n_ref = refs
        if t == 0:
            return p_ref[0, :, cols]
        if t == WIN_Q + 1:
            return n_ref[0, :, cols]
        return c_ref[0, (t - 1) * WBLK:t * WBLK, cols]

    for sub in range(WIN_Q):
        n = j * WIN_Q + sub
        qpos = n * WBLK + (r_iota & (WBLK - 1))
        kpos = (n - 1) * WBLK + c_iota
        valid = ((jnp.abs(qpos - kpos) <= WINDOW) & (kpos >= 0) & (kpos < seq)) | (c_iota >= band)
        for kh in range(A_KV_HEADS):
            cs = slice(kh * HEAD_DIM, (kh + 1) * HEAD_DIM)
            vs = slice(kh * V_PAD, (kh + 1) * V_PAD)
            q = jnp.concatenate(
                [q_ref[0, sub * WBLK:(sub + 1) * WBLK, (kh * A_GROUP + g) * HEAD_DIM:(kh * A_GROUP + g + 1) * HEAD_DIM]
                 for g in range(A_GROUP)], axis=0)
            kb = jnp.concatenate([key_block((kp_ref, kc_ref, kn_ref), sub + t, cs) for t in range(3)]
                                 + [kx_ref[0, :, cs]], axis=0)
            vb = jnp.concatenate([key_block((vp_ref, vc_ref, vn_ref), sub + t, vs) for t in range(3)]
                                 + [vx_ref[0, :, vs]], axis=0)
            s = jnp.where(valid, lax.dot_general(q, kb, dn, preferred_element_type=F32), -jnp.inf)
            sink = jnp.zeros((rows, 1), F32)
            for g in range(A_GROUP):
                sink = jnp.where(head_of_row == g, sink_ref[kh * A_GROUP + g] * LOG2E, sink)
            m = jnp.maximum(jnp.max(s, axis=-1, keepdims=True), sink)
            acc = jnp.dot(jnp.exp2(s - m).astype(BF16), vb, preferred_element_type=F32)
            o = acc[:, :HEAD_DIM] / (acc[:, HEAD_DIM:HEAD_DIM + 1] + jnp.exp2(sink - m))
            for g in range(A_GROUP):
                hd = kh * A_GROUP + g
                o_ref[0, sub * WBLK:(sub + 1) * WBLK, hd * HEAD_DIM:(hd + 1) * HEAD_DIM] = (
                    o[g * WBLK:(g + 1) * WBLK].astype(BF16))


def _win_attn(sink, aq, ak, av, akc, avc):
    b, s, _ = aq.shape
    nb = s // WBLK
    c = akc.shape[1]
    tq = WIN_Q * WBLK
    prev = lambda w: pl.BlockSpec((1, WBLK, w), lambda i, j: (i, jnp.maximum(j * WIN_Q - 1, 0), 0))
    cur = lambda w: pl.BlockSpec((1, tq, w), lambda i, j: (i, j, 0))
    nxt = lambda w: pl.BlockSpec((1, WBLK, w), lambda i, j: (i, jnp.minimum((j + 1) * WIN_Q, nb - 1), 0))
    cx = lambda w: pl.BlockSpec((1, c, w), lambda i, j: (i, 0, 0))
    kw, vw = A_KV_HEADS * HEAD_DIM, A_KV_HEADS * V_PAD
    qo = pl.BlockSpec((1, tq, A_HEADS * HEAD_DIM), lambda i, j: (i, j, 0))
    return pl.pallas_call(
        functools.partial(_win_kernel, seq=s),
        grid=(b, nb // WIN_Q),
        in_specs=[pl.BlockSpec(memory_space=pltpu.SMEM), qo, prev(kw), cur(kw), nxt(kw), prev(vw), cur(vw), nxt(vw),
                  cx(kw), cx(vw)],
        out_specs=qo,
        out_shape=jax.ShapeDtypeStruct(aq.shape, BF16),
        compiler_params=_cparams(("arbitrary", "arbitrary")),
        name="win_attn",
    )(sink, aq, ak, ak, ak, av, av, av, akc, avc)


def _mla_kernel(q_ref, k_ref, v_ref, kx_ref, vx_ref, *rest, tk, n_side):
    side_in, o_ref, side_out = rest[:n_side], rest[n_side], rest[n_side + 1:2 * n_side + 1]
    s_a, s_b, acc_ref = rest[2 * n_side + 1:]
    for w_in, w_out in zip(side_in, side_out):
        w_out[...] = w_in[...].astype(BF16)
    q = q_ref[0]
    dn = (((1,), (1,)), ((), ()))
    n_chunks = k_ref.shape[1] // tk
    s_bufs = (s_a, s_b)

    def scores_into(buf, c):
        s = lax.dot_general(q, k_ref[0, c * tk:(c + 1) * tk, :], dn, preferred_element_type=F32)
        buf[...] = s
        return jnp.max(s, axis=-1, keepdims=True)

    s0 = lax.dot_general(q, kx_ref[0], dn, preferred_element_type=F32)
    m = jnp.max(s0, axis=-1, keepdims=True)
    acc_ref[...] = jnp.dot(jnp.exp2(s0 - m).astype(BF16), vx_ref[0], preferred_element_type=F32)
    mx = scores_into(s_bufs[0], 0)
    for c in range(n_chunks):
        if c + 1 < n_chunks:
            mx_next = scores_into(s_bufs[(c + 1) % 2], c + 1)
        m_new = jnp.maximum(m, mx)
        p = jnp.exp2(s_bufs[c % 2][...] - m_new).astype(BF16)
        acc_ref[...] = (jnp.exp2(m - m_new) * acc_ref[...]
                        + jnp.dot(p, v_ref[0, c * tk:(c + 1) * tk, :], preferred_element_type=F32))
        m, mx = m_new, mx_next
    acc = acc_ref[...]
    o_ref[0] = (acc[:, :V_DIM] / acc[:, V_DIM:V_DIM + 1]).astype(BF16)


def _mla_attn(bq, bk, bv, bkc, bvc, side, tq, tk):
    b, s, _ = bq.shape
    c = bkc.shape[1]
    nq = s // tq
    steps = b * B_HEADS * nq
    side2d = [w.reshape(-1, w.shape[-1]) for w in side]
    slab = lambda i, h, j: ((i * B_HEADS + h) * nq + j, 0)
    side_specs = [pl.BlockSpec((w.shape[0] // steps, w.shape[1]), slab) for w in side2d]
    res = pl.pallas_call(
        functools.partial(_mla_kernel, tk=tk, n_side=len(side)),
        grid=(b, B_HEADS, nq),
        in_specs=[
            pl.BlockSpec((1, tq, QK_PAD), lambda i, h, j: (i, j, h)),
            pl.BlockSpec((1, s, QK_PAD), lambda i, h, j: (i, 0, h)),
            pl.BlockSpec((1, s, V_PAD), lambda i, h, j: (i, 0, h)),
            pl.BlockSpec((1, c, QK_PAD), lambda i, h, j: (i, 0, h)),
            pl.BlockSpec((1, c, V_PAD), lambda i, h, j: (i, 0, h)),
        ] + side_specs,
        out_specs=[pl.BlockSpec((1, tq, V_DIM), lambda i, h, j: (i, j, h))] + side_specs,
        out_shape=[jax.ShapeDtypeStruct((b, s, B_HEADS * V_DIM), BF16)]
        + [jax.ShapeDtypeStruct(w.shape, BF16) for w in side2d],
        scratch_shapes=[pltpu.VMEM((tq, tk), F32), pltpu.VMEM((tq, tk), F32), pltpu.VMEM((tq, V_PAD), F32)],
        compiler_params=_cparams(("arbitrary", "arbitrary", "arbitrary")),
        name="mla_attn",
    )(bq, bk, bv, bkc, bvc, *side2d)
    return res[0], [o.reshape(w.shape) for o, w in zip(res[1:], side)]


def _route(sel, aff):
    scores = []
    for g in range(N_GROUPS):
        r = sel[g * EXP_PER_GROUP:(g + 1) * EXP_PER_GROUP]
        best = None
        for a in range(EXP_PER_GROUP):
            for b in range(a + 1, EXP_PER_GROUP):
                pair = r[a] + r[b]
                best = pair if best is None else jnp.maximum(best, pair)
        scores.append(best)
    top, grp = scores[0], jnp.zeros_like(scores[0], dtype=I32)
    for g in range(1, N_GROUPS):
        take = scores[g] > top
        grp = jnp.where(take, g, grp)
        top = jnp.where(take, scores[g], top)
    masked = [jnp.where(grp == e // EXP_PER_GROUP, sel[e], -jnp.inf) for e in range(N_EXPERTS)]

    def argmax_first(vals, skip=None):
        bv = jnp.full_like(vals[0], -jnp.inf)
        bi = jnp.full_like(grp, -1)
        for e in range(N_EXPERTS):
            take = vals[e] > bv
            if skip is not None:
                take = take & (skip != e)
            bi = jnp.where(take, e, bi)
            bv = jnp.where(take, vals[e], bv)
        return bi

    i0 = argmax_first(masked)
    i1 = argmax_first(masked, skip=i0)
    a0 = jnp.zeros_like(aff[0])
    a1 = jnp.zeros_like(aff[0])
    for e in range(N_EXPERTS):
        a0 = jnp.where(i0 == e, aff[e], a0)
        a1 = jnp.where(i1 == e, aff[e], a1)
    tot = a0 + a1
    return i0, i1, a0 / tot, a1 / tot


def _out_kernel(*refs, n_lhs):
    lhs = refs[:n_lhs]
    ws = refs[n_lhs:2 * n_lhs]
    bias_ref, x_ref, mod_ref, gn_ref, wr_ref, br_ref, x1_ref, hp_ref, idx_ref, wts_ref = refs[2 * n_lhs:]
    y = bias_ref[...]
    for a, w in zip(lhs, ws):
        y = y + jnp.dot(a[0], w[...], preferred_element_type=F32)
    x1 = x_ref[0] + mod_ref[0, 2:3, :] * y
    x1_ref[0] = x1
    h2 = (_rms(x1) * gn_ref[...]) * (1.0 + mod_ref[0, 4:5, :]) + mod_ref[0, 3:4, :]
    hp_ref[0] = h2
    w = wr_ref[...]
    w_head = w.astype(BF16)
    w_tail = (w - w_head.astype(F32)).astype(BF16)
    h_head = h2.astype(BF16)
    h_tail = (h2 - h_head.astype(F32)).astype(BF16)
    t = jnp.dot(h_head, jnp.concatenate([w_head, w_tail], axis=1), preferred_element_type=F32)
    logits = (t[:, :LANES] + t[:, LANES:]) + jnp.dot(h_tail, w_head, preferred_element_type=F32)
    lt = logits.T[:N_EXPERTS]
    aff_t = jax.nn.sigmoid(lt)
    sel_t = aff_t + br_ref[...]
    sel = [sel_t[e:e + 1] for e in range(N_EXPERTS)]
    aff = [aff_t[e:e + 1] for e in range(N_EXPERTS)]
    i0, i1, w0, w1 = _route(sel, aff)
    idx_ref[0] = jnp.concatenate([i0, i1], axis=0)
    w_rows = jnp.concatenate([w0, w1, jnp.zeros((LANES - TOP_K, w0.shape[1]), F32)], axis=0)
    wts_ref[0] = w_rows.T[:, :TOP_K]


def _out_proj(lhs, ws, bias, x, mod, gn, wr, br, tm):
    b, s, d = x.shape
    n_lhs = len(lhs)
    row = pl.BlockSpec((1, tm, d), lambda i, j: (i, j, 0))
    in_specs = ([pl.BlockSpec((1, tm, a.shape[-1]), lambda i, j: (i, j, 0)) for a in lhs]
                + [_resident(w.shape) for w in ws]
                + [_resident(bias.shape), row, pl.BlockSpec((1, 6, d), lambda i, j: (i, 0, 0)),
                   _resident(gn.shape), _resident(wr.shape), _resident(br.shape)])
    return pl.pallas_call(
        functools.partial(_out_kernel, n_lhs=n_lhs),
        grid=(b, s // tm),
        in_specs=in_specs,
        out_specs=[row, row,
                   pl.BlockSpec((1, TOP_K, tm), lambda i, j: (i, 0, j)),
                   pl.BlockSpec((1, tm, TOP_K), lambda i, j: (i, j, 0))],
        out_shape=[jax.ShapeDtypeStruct((b, s, d), F32), jax.ShapeDtypeStruct((b, s, d), F32),
                   jax.ShapeDtypeStruct((b, TOP_K, s), I32), jax.ShapeDtypeStruct((b, s, TOP_K), F32)],
        compiler_params=_cparams(("arbitrary", "arbitrary")),
        name="out_proj",
    )(*lhs, *ws, bias, x, mod, gn, wr, br)


def _dispatch(idx, blk):
    b, _, s = idx.shape
    n_asg = b * TOP_K * s
    e = idx.reshape(n_asg)
    onehot = (e[:, None] == jnp.arange(N_EXPERTS, dtype=I32)[None, :]).astype(I32)
    csum = jnp.cumsum(onehot, axis=0)
    counts = csum[-1]
    rank = jnp.sum((csum - onehot) * onehot, axis=1)
    padded = (counts + blk - 1) // blk * blk
    pad_end = jnp.cumsum(padded)
    pad_start = pad_end - padded
    start = jnp.cumsum(counts) - counts
    dest = pad_start[e] + rank
    src_asg = jnp.sort(e * n_asg + jnp.arange(n_asg, dtype=I32)) % n_asg
    src_tok = src_asg // (TOP_K * s) * s + src_asg % s
    src_tok = jnp.concatenate([src_tok, jnp.zeros((blk,), src_tok.dtype)])
    n_blocks = -(-n_asg // blk) + N_EXPERTS
    n_used = pad_end[-1] // blk
    blk_ids = jnp.minimum(jnp.arange(n_blocks, dtype=I32), n_used - 1)
    blk_e = jnp.sum((blk_ids[:, None] * blk >= pad_end[None, :]).astype(I32), axis=1)
    blk_e = jnp.minimum(blk_e, N_EXPERTS - 1)
    blk_lo = start[blk_e] + blk_ids * blk - pad_start[blk_e]
    as_i32 = lambda a: a.astype(I32)
    return as_i32(dest), as_i32(src_tok), as_i32(blk_e), as_i32(blk_lo), as_i32(n_used.reshape(1))


def _moe_kernel(blk_e_ref, lo_ref, n_used_ref, tok_ref, h_ref, wg_ref, wu_ref, wd_ref, o_ref, xbuf, sem, *, blk):
    i = pl.program_id(0)
    n_used = n_used_ref[0]
    slot = i % 2

    def gather(block, slot_, unrolled):
        lo = lo_ref[block]

        def row_copy(r):
            t = tok_ref[lo + r]
            pltpu.make_async_copy(h_ref.at[pl.ds(t, 1)], xbuf.at[slot_, pl.ds(r, 1)], sem.at[slot_]).start()

        if unrolled:
            for r in range(blk):
                row_copy(r)
        else:
            def body(r, _):
                row_copy(r)
                return 0
            lax.fori_loop(0, blk, body, 0, unroll=8)

    @pl.when(i == 0)
    def _():
        gather(0, 0, False)

    @pl.when(i + 1 < n_used)
    def _():
        gather(i + 1, 1 - slot, True)

    @pl.when(i < n_used)
    def _():
        pltpu.make_async_copy(xbuf.at[slot], xbuf.at[slot], sem.at[slot]).wait()
        xb = xbuf[slot].astype(BF16)
        g = jnp.dot(xb, wg_ref[0, 0], preferred_element_type=F32)
        u = jnp.dot(xb, wu_ref[0, 0], preferred_element_type=F32)
        a = (g * jax.nn.sigmoid(g) * u).astype(BF16)
        o_ref[...] = jnp.dot(a, wd_ref[0, 0], preferred_element_type=F32)

    @pl.when(i >= n_used)
    def _():
        o_ref[...] = jnp.zeros_like(o_ref)


def _moe(h2, src_tok, blk_e, blk_lo, n_used, wg, wu, wd, layer, blk):
    n_blocks = blk_e.shape[0]
    _, _, d, de = wg.shape
    grid_spec = pltpu.PrefetchScalarGridSpec(
        num_scalar_prefetch=4,
        grid=(n_blocks,),
        in_specs=[
            pl.BlockSpec(memory_space=pl.ANY),
            pl.BlockSpec((1, 1, d, de), lambda i, be, *_: (layer, be[i], 0, 0)),
            pl.BlockSpec((1, 1, d, de), lambda i, be, *_: (layer, be[i], 0, 0)),
            pl.BlockSpec((1, 1, de, d), lambda i, be, *_: (layer, be[i], 0, 0)),
        ],
        out_specs=pl.BlockSpec((blk, d), lambda i, *_: (i, 0)),
        scratch_shapes=[pltpu.VMEM((2, blk, d), F32), pltpu.SemaphoreType.DMA((2,))],
    )
    return pl.pallas_call(
        functools.partial(_moe_kernel, blk=blk),
        grid_spec=grid_spec,
        out_shape=jax.ShapeDtypeStruct((n_blocks * blk, d), F32),
        compiler_params=_cparams(("arbitrary",)),
        name="moe_ffn",
    )(blk_e, blk_lo, n_used, src_tok, h2, wg, wu, wd)


def _combine_kernel(dest_ref, yp_ref, x_ref, wt_ref, mod_ref, *rest, tm, nt, with_norm):
    if with_norm:
        gn_ref, modn_ref, o_ref, hn_ref, buf, sem = rest
    else:
        o_ref, buf, sem = rest
    i = pl.program_id(0)
    j = pl.program_id(1)
    step = i * nt + j
    slot = step % 2

    def gather(step_, slot_, unrolled):
        bases = [((step_ // nt) * TOP_K + k) * (nt * tm) + (step_ % nt) * tm for k in range(TOP_K)]

        def row_copies(r):
            for k in range(TOP_K):
                pltpu.make_async_copy(yp_ref.at[pl.ds(dest_ref[bases[k] + r], 1)],
                                      buf.at[slot_, k, pl.ds(r, 1)], sem.at[slot_]).start()

        if unrolled:
            for r in range(tm):
                row_copies(r)
        else:
            def body(r, _):
                row_copies(r)
                return 0
            lax.fori_loop(0, tm, body, 0, unroll=8)

    @pl.when(step == 0)
    def _():
        gather(0, 0, False)

    @pl.when(step + 1 < pl.num_programs(0) * nt)
    def _():
        gather(step + 1, 1 - slot, True)

    pltpu.make_async_copy(buf.at[slot], buf.at[slot], sem.at[slot]).wait()
    w = wt_ref[0]
    y = buf[slot, 0] * w[:, 0:1] + buf[slot, 1] * w[:, 1:2]
    out = x_ref[0] + mod_ref[0, 5:6, :] * y
    o_ref[0] = out
    if with_norm:
        hn_ref[0] = ((_rms(out) * gn_ref[...]) * (1.0 + modn_ref[0, 1:2, :]) + modn_ref[0, 0:1, :]).astype(BF16)


def _combine(dest, yp, x1, wts, mod, tm, norm=None):
    b, s, d = x1.shape
    nt = s // tm
    with_norm = norm is not None
    row = pl.BlockSpec((1, tm, d), lambda i, j, ds: (i, j, 0))
    modspec = pl.BlockSpec((1, 6, d), lambda i, j, ds: (i, 0, 0))
    in_specs = [pl.BlockSpec(memory_space=pl.ANY), row,
                pl.BlockSpec((1, tm, TOP_K), lambda i, j, ds: (i, j, 0)), modspec]
    args = [yp, x1, wts, mod]
    out_shape = [jax.ShapeDtypeStruct((b, s, d), F32)]
    out_specs = [row]
    if with_norm:
        gn, modn = norm
        in_specs += [pl.BlockSpec(gn.shape, lambda i, j, ds: (0, 0)), modspec]
        args += [gn, modn]
        out_shape.append(jax.ShapeDtypeStruct((b, s, d), BF16))
        out_specs.append(row)
    grid_spec = pltpu.PrefetchScalarGridSpec(
        num_scalar_prefetch=1, grid=(b, nt), in_specs=in_specs, out_specs=out_specs,
        scratch_shapes=[pltpu.VMEM((2, TOP_K, tm, d), F32), pltpu.SemaphoreType.DMA((2,))])
    return pl.pallas_call(
        functools.partial(_combine_kernel, tm=tm, nt=nt, with_norm=with_norm),
        grid_spec=grid_spec,
        out_shape=out_shape,
        compiler_params=_cparams(("arbitrary", "arbitrary")),
        name="combine",
    )(dest, *args)


def _dft_constants(seq):
    n1, n2 = DFT_N1, DFT_N2
    assert seq == n1 * n2 and n2 == n1 * n1
    c = np.arange(F_GROUP_DIM)
    ang = 2 * np.pi * np.outer(c, c) / F_GROUP_DIM
    fc = np.concatenate([np.cos(ang), -np.sin(ang)], axis=1)
    a, k2, m = np.meshgrid(np.arange(n1), np.arange(n2), np.arange(n2), indexing="ij")
    ang_a = -2 * np.pi * (k2 * (a + n1 * m) % seq) / seq
    tre, tim = np.cos(ang_a), np.sin(ang_a)
    ma = np.concatenate([np.concatenate([tre, -tim], axis=2), np.concatenate([tim, tre], axis=2)], axis=1)
    ang_b = -2 * np.pi * np.outer(np.arange(n1), np.arange(n1)) / n1
    eye = np.eye(n1)
    mb = np.concatenate([np.kron(np.cos(ang_b), eye), -np.kron(np.sin(ang_b), eye)], axis=1)
    mb = mb / math.sqrt(seq * F_GROUP_DIM)
    return tuple(jnp.asarray(t, F32).astype(BF16) for t in (fc, ma, mb))


def _fourier_kernel(h_ref, fc_ref, ma_ref, mb_ref, o_ref, z_ref, yre, yim):
    n1, n2, gd = DFT_N1, DFT_N2, F_GROUP_DIM
    rows = 512
    n_tiles = 2 * gd // LANES
    for r in range(0, h_ref.shape[1], rows):
        z = jnp.dot(h_ref[0, r:r + rows, :], fc_ref[...], preferred_element_type=F32)
        for t in range(n_tiles):
            z_ref[t, r:r + rows, :] = z[:, t * LANES:(t + 1) * LANES]
    for a in range(n1):
        zs = [z_ref[t, pl.ds(a, n2, stride=n1), :].astype(BF16) for t in range(n_tiles)]
        rhs = jnp.concatenate([jnp.concatenate(zs[:n_tiles // 2], axis=1),
                               jnp.concatenate(zs[n_tiles // 2:], axis=1)], axis=0)
        y = jnp.dot(ma_ref[a], rhs, preferred_element_type=F32).astype(BF16)
        yre[a] = y[:n2]
        yim[a] = y[n2:]
    for hi in range(n1):
        rhs = jnp.concatenate([yre[a, hi * n1:(hi + 1) * n1, :] for a in range(n1)]
                              + [yim[a, hi * n1:(hi + 1) * n1, :] for a in range(n1)], axis=0)
        out = jnp.dot(mb_ref[...], rhs, preferred_element_type=F32).astype(BF16)
        for k1 in range(n1):
            o_ref[0, k1 * n2 + hi * n1:k1 * n2 + (hi + 1) * n1, :] = out[k1 * n1:(k1 + 1) * n1]


def _fourier(hn, consts):
    b, s, d = hn.shape
    fc, ma, mb = consts
    gd = F_GROUP_DIM
    return pl.pallas_call(
        _fourier_kernel,
        grid=(b, F_GROUPS),
        in_specs=[pl.BlockSpec((1, s, gd), lambda i, g: (i, 0, g)),
                  _resident(fc.shape), _resident(ma.shape), _resident(mb.shape)],
        out_specs=pl.BlockSpec((1, s, gd), lambda i, g: (i, 0, g)),
        out_shape=jax.ShapeDtypeStruct((b, s, d), BF16),
        scratch_shapes=[pltpu.VMEM((2 * gd // LANES, s, LANES), F32),
                        pltpu.VMEM((DFT_N1, DFT_N2, gd), BF16), pltpu.VMEM((DFT_N1, DFT_N2, gd), BF16)],
        compiler_params=_cparams(("arbitrary", "arbitrary")),
        name="fourier",
    )(hn, fc, ma, mb)


def _rope_tables(seq):
    pos = np.arange(seq)
    row, col = (pos // GRID_W).astype(np.float64), (pos % GRID_W).astype(np.float64)

    def tables(width):
        half = width // 2
        quarter = half // 2
        freqs = ROPE_BASE ** (-np.arange(0, half, 2, dtype=np.float64) / half)
        lane = np.arange(LANES)
        ang = np.where((lane < half)[None, :], row[:, None], col[:, None]) * freqs[lane % quarter][None, :]
        live = (lane < width)[None, :]
        first = ((lane % half) < quarter)[None, :]
        cos = np.where(live, np.cos(ang), 1.0)
        sneg = np.where(live & first, -np.sin(ang), 0.0)
        spos = np.where(live & ~first, np.sin(ang), 0.0)
        return [jnp.asarray(t, F32) for t in (cos, sneg, spos)]

    return tables(HEAD_DIM) + tables(ROPE_DIM)


def _identity_tables(n):
    one, zero = jnp.ones((n, LANES), F32), jnp.zeros((n, LANES), F32)
    return [one, zero, zero, one, zero, zero]


def _pad_heads(w, lead):
    w = w.reshape(lead, B_HEADS, QK_DIM)
    return jnp.pad(w, ((0, 0), (0, 0), (0, QK_PAD - QK_DIM))).reshape(lead, B_HEADS * QK_PAD)


def kernel(x, c, ctx, c_ctx, w_ada, b_ada, g_norm, w_in, g_aqn, g_akn, g_bq_lat, w_bq_up, g_bkv_lat, w_bkv_up,
           g_bqn, g_bkn, sink, w_o_ab, w_fo, b_fo, w_router, b_router, w_gate, w_up, w_down):
    b, s, d = x.shape
    n_ctx = ctx.shape[1]

    crows = jnp.concatenate([c, c_ctx[None, :], jnp.zeros((8 - b - 1, d), F32)], axis=0)
    mods = _ada(crows, w_ada, b_ada).reshape(DEPTH, 8, 6, d)
    mod_lat = [mods[l, :b] for l in range(DEPTH)]
    mod_ctx = jnp.broadcast_to(mods[0, b][None], (b, 6, d))

    wr = jnp.pad(w_router, ((0, 0), (0, LANES - N_EXPERTS)))
    br = b_router.reshape(N_EXPERTS, 1)

    win = jnp.pad(w_in[0], ((0, 0), (0, IN_PAD - IN_WIDTH))).astype(BF16)
    wbq = _pad_heads(w_bq_up[0], Q_LORA).astype(BF16)
    wkv = w_bkv_up[0].reshape(KV_LORA, B_HEADS, NOPE_DIM + V_DIM)
    wbkv = jnp.concatenate([wkv[:, :, :NOPE_DIM].reshape(KV_LORA, -1), wkv[:, :, NOPE_DIM:].reshape(KV_LORA, -1)],
                           axis=1).astype(BF16)
    gains = (g_aqn[0][None], g_akn[0][None], g_bq_lat[0][None], g_bkv_lat[0][None],
             jnp.pad(g_bqn[0][None], ((0, 0), (0, QK_PAD - QK_DIM))),
             jnp.pad(g_bkn[0][None], ((0, 0), (0, QK_PAD - QK_DIM))))
    gn0 = g_norm[0, 0][None]
    aq, ak, av, bq, bk, bv = _proj(x, mod_lat[0], gn0, win, wbq, wbkv, *gains, _rope_tables(s), 256)
    _, akc, avc, _, bkc, bvc = _proj(ctx, mod_ctx, gn0, win, wbq, wbkv, *gains, _identity_tables(n_ctx), n_ctx)

    ya = _win_attn(sink[0], aq, ak, av, akc, avc)
    yb, (wg, wu, wd) = _mla_attn(bq, bk, bv, bkc, bvc, (w_gate, w_up, w_down), 1024, 1024)

    wo = w_o_ab[0].astype(BF16)
    n_a = A_HEADS * HEAD_DIM
    x1, h2, idx, wts = _out_proj([ya, yb], [wo[:n_a], wo[n_a:]], jnp.zeros((1, d), F32), x, mod_lat[0],
                                 g_norm[0, 1][None], wr, br, 512)
    dest, *plan = _dispatch(idx, MOE_BLK)
    yp = _moe(h2.reshape(b * s, d), *plan, wg, wu, wd, 0, MOE_BLK)
    x2, hn = _combine(dest, yp, x1, wts, mod_lat[0], 512,
                      norm=(g_norm[1, 0][None], mod_lat[1]))

    f = _fourier(hn, _dft_constants(s))
    x3, h2, idx, wts = _out_proj([f], [w_fo[0].astype(BF16)], b_fo[0][None], x2, mod_lat[1],
                                 g_norm[1, 1][None], wr, br, 512)
    dest, *plan = _dispatch(idx, MOE_BLK)
    yp = _moe(h2.reshape(b * s, d), *plan, wg, wu, wd, 1, MOE_BLK)
    (x4,) = _combine(dest, yp, x3, wts, mod_lat[1], 512)
    return x4
```

```python
import functools
import math

import numpy as np
import jax
import jax.numpy as jnp
from jax import lax
from jax.experimental import pallas as pl
from jax.experimental.pallas import tpu as pltpu

F32 = jnp.float32
BF16 = jnp.bfloat16
I32 = jnp.int32
HIGHEST = lax.Precision.HIGHEST

D_MODEL = 2048
DEPTH = 2
GRID_W = 64
HEAD_DIM = 128
A_HEADS = 8
A_KV_HEADS = 2
A_GROUP = A_HEADS // A_KV_HEADS
WINDOW = 128
WBLK = 128
B_HEADS = 8
Q_LORA = 512
KV_LORA = 256
NOPE_DIM = 128
ROPE_DIM = 64
V_DIM = 128
QK_DIM = NOPE_DIM + ROPE_DIM
QK_PAD = 256
V_PAD = 256
IN_SPLITS = (A_HEADS * HEAD_DIM, A_KV_HEADS * HEAD_DIM, A_KV_HEADS * HEAD_DIM, Q_LORA, KV_LORA, ROPE_DIM)
IN_WIDTH = sum(IN_SPLITS)
IN_PAD = 2432
F_GROUPS = 8
F_GROUP_DIM = D_MODEL // F_GROUPS
N_EXPERTS = 16
N_GROUPS = 4
EXP_PER_GROUP = N_EXPERTS // N_GROUPS
TOP_K = 2
D_EXPERT = 1024
ROPE_BASE = 10000.0
EPS = 1e-6
LOG2E = math.log2(math.e)
LANES = 128

MOE_BLK = 512
WIN_Q = 4
DFT_N1 = 16
DFT_N2 = 256
VMEM_LIMIT = 56 * 1024 * 1024


def _cparams(sem, **kw):
    return pltpu.CompilerParams(dimension_semantics=sem, vmem_limit_bytes=VMEM_LIMIT, **kw)


def _resident(shape):
    nd = len(shape)
    return pl.BlockSpec(shape, lambda *_: (0,) * nd, pipeline_mode=pl.Buffered(1))


def _rms(t, width=None):
    n = t.shape[-1] if width is None else width
    ss = jnp.sum(t * t, axis=-1, keepdims=True)
    return t * lax.rsqrt(ss * (1.0 / n) + EPS)


def _rope(t, cos, sneg, spos, dist):
    n = t.shape[-1]
    return t * cos + pltpu.roll(t, n - dist, 1) * sneg + pltpu.roll(t, dist, 1) * spos


ADA_SIDE_STEPS = 16


def _ada_kernel(c_ref, w_ref, b_ref, *rest, n_side, nj):
    side_in, o_ref, side_out = rest[:n_side], rest[n_side], rest[n_side + 1:]
    c = c_ref[...]
    s = c * jax.nn.sigmoid(c)
    o_ref[0] = jnp.dot(s, w_ref[0], precision=HIGHEST, preferred_element_type=F32) + b_ref[0]

    @pl.when(pl.program_id(0) * nj + pl.program_id(1) < ADA_SIDE_STEPS)
    def _():
        for w_in, w_out in zip(side_in, side_out):
            width = w_in.shape[1]
            w_out[:, :width] = w_in[...].astype(BF16)
            if w_out.shape[1] > width:
                w_out[:, width:] = jnp.zeros((w_out.shape[0], w_out.shape[1] - width), BF16)


def _ada(crows, w_ada, b_ada, side, side_widths):
    depth, d, n = w_ada.shape
    tn = 1024
    nj = n // tn
    assert depth * nj >= ADA_SIDE_STEPS
    slab = lambda l, j: (jnp.minimum(l * nj + j, ADA_SIDE_STEPS - 1), 0)
    side_in = [pl.BlockSpec((w.shape[0] // ADA_SIDE_STEPS, w.shape[1]), slab) for w in side]
    side_out = [pl.BlockSpec((w.shape[0] // ADA_SIDE_STEPS, wd), slab) for w, wd in zip(side, side_widths)]
    res = pl.pallas_call(
        functools.partial(_ada_kernel, n_side=len(side), nj=nj),
        grid=(depth, nj),
        in_specs=[
            pl.BlockSpec((8, d), lambda l, j: (0, 0)),
            pl.BlockSpec((1, d, tn), lambda l, j: (l, 0, j)),
            pl.BlockSpec((1, 1, tn), lambda l, j: (l, 0, j)),
        ] + side_in,
        out_specs=[pl.BlockSpec((1, 8, tn), lambda l, j: (l, 0, j))] + side_out,
        out_shape=[jax.ShapeDtypeStruct((depth, 8, n), F32)]
        + [jax.ShapeDtypeStruct((w.shape[0], wd), BF16) for w, wd in zip(side, side_widths)],
        compiler_params=_cparams(("arbitrary", "arbitrary")),
        name="ada",
    )(crows, w_ada, b_ada.reshape(depth, 1, n), *side)
    return res[0], res[1:]


def _proj_kernel(x_ref, mod_ref, gn_ref, win_ref, wbq_ref, wbkv_ref, gaq_ref, gak_ref, gbql_ref, gbkvl_ref,
                 gbq_ref, gbk_ref, ca_ref, sna_ref, spa_ref, cb_ref, snb_ref, spb_ref,
                 aq_ref, ak_ref, av_ref, bq_ref, bk_ref, bv_ref):
    x = x_ref[0]
    shift = mod_ref[0, 0:1, :]
    scale = mod_ref[0, 1:2, :]
    hb = ((_rms(x) * gn_ref[...]) * (1.0 + scale) + shift).astype(BF16)

    def cols(w_ref, lhs, lo, width):
        return jnp.dot(lhs, w_ref[:, lo:lo + width], preferred_element_type=F32)

    ca, sna, spa = ca_ref[...], sna_ref[...], spa_ref[...]
    cb, snb, spb = cb_ref[...], snb_ref[...], spb_ref[...]
    a_scale = HEAD_DIM ** -0.5 * LOG2E
    b_scale = QK_DIM ** -0.5 * LOG2E
    pair = 2 * HEAD_DIM

    for hp in range(A_HEADS // 2):
        pp = cols(win_ref, hb, hp * pair, pair)
        for j in range(2):
            hd = 2 * hp + j
            t = _rms(pp[:, j * HEAD_DIM:(j + 1) * HEAD_DIM]) * gaq_ref[...]
            aq_ref[0, :, hd * HEAD_DIM:(hd + 1) * HEAD_DIM] = (_rope(t, ca, sna, spa, 32) * a_scale).astype(BF16)
    off = A_HEADS * HEAD_DIM
    pp = cols(win_ref, hb, off, A_KV_HEADS * HEAD_DIM)
    for kh in range(A_KV_HEADS):
        t = _rms(pp[:, kh * HEAD_DIM:(kh + 1) * HEAD_DIM]) * gak_ref[...]
        ak_ref[0, :, kh * HEAD_DIM:(kh + 1) * HEAD_DIM] = _rope(t, ca, sna, spa, 32).astype(BF16)
    off += A_KV_HEADS * HEAD_DIM
    ones_col = (lax.broadcasted_iota(I32, (hb.shape[0], V_PAD - V_DIM), 1) == 0).astype(BF16)
    pp = cols(win_ref, hb, off, A_KV_HEADS * HEAD_DIM)
    for kh in range(A_KV_HEADS):
        av_ref[0, :, kh * V_PAD:kh * V_PAD + HEAD_DIM] = pp[:, kh * HEAD_DIM:(kh + 1) * HEAD_DIM].astype(BF16)
        av_ref[0, :, kh * V_PAD + HEAD_DIM:(kh + 1) * V_PAD] = ones_col
    off += A_KV_HEADS * HEAD_DIM

    ql = (_rms(cols(win_ref, hb, off, Q_LORA)) * gbql_ref[...]).astype(BF16)
    off += Q_LORA
    for hd in range(B_HEADS):
        t = _rms(cols(wbq_ref, ql, hd * QK_PAD, QK_PAD), QK_DIM) * gbq_ref[...]
        bq_ref[0, :, hd * QK_PAD:hd * QK_PAD + NOPE_DIM] = (t[:, :NOPE_DIM] * b_scale).astype(BF16)
        bq_ref[0, :, hd * QK_PAD + NOPE_DIM:(hd + 1) * QK_PAD] = (
            _rope(t[:, NOPE_DIM:], cb, snb, spb, 16) * b_scale).astype(BF16)

    kvl = (_rms(cols(win_ref, hb, off, KV_LORA)) * gbkvl_ref[...]).astype(BF16)
    off += KV_LORA
    kr = cols(win_ref, hb, off, LANES)
    kr_ss = jnp.sum(kr * kr, axis=-1, keepdims=True)
    kr_rot = _rope(kr * gbk_ref[:, NOPE_DIM:], cb, snb, spb, 16)
    for hp in range(B_HEADS // 2):
        kn2 = cols(wbkv_ref, kvl, hp * pair, pair)
        v2 = cols(wbkv_ref, kvl, B_HEADS * NOPE_DIM + hp * pair, pair)
        for j in range(2):
            hd = 2 * hp + j
            kn = kn2[:, j * NOPE_DIM:(j + 1) * NOPE_DIM]
            ss = jnp.sum(kn * kn, axis=-1, keepdims=True) + kr_ss
            r = lax.rsqrt(ss * (1.0 / QK_DIM) + EPS)
            bk_ref[0, :, hd * QK_PAD:hd * QK_PAD + NOPE_DIM] = (kn * r * gbk_ref[:, :NOPE_DIM]).astype(BF16)
            bk_ref[0, :, hd * QK_PAD + NOPE_DIM:(hd + 1) * QK_PAD] = (kr_rot * r).astype(BF16)
            bv_ref[0, :, hd * V_PAD:hd * V_PAD + V_DIM] = v2[:, j * V_DIM:(j + 1) * V_DIM].astype(BF16)
            bv_ref[0, :, hd * V_PAD + V_DIM:(hd + 1) * V_PAD] = ones_col


def _proj(x, mod, gn, win, wbq, wbkv, gaq, gak, gbql, gbkvl, gbq, gbk, tabs, tm):
    b, s, d = x.shape
    row = lambda w: pl.BlockSpec((1, tm, w), lambda i, j: (i, j, 0))
    tab = pl.BlockSpec((tm, LANES), lambda i, j: (j, 0))
    widths = (A_HEADS * HEAD_DIM, A_KV_HEADS * HEAD_DIM, A_KV_HEADS * V_PAD,
              B_HEADS * QK_PAD, B_HEADS * QK_PAD, B_HEADS * V_PAD)
    return pl.pallas_call(
        _proj_kernel,
        grid=(b, s // tm),
        in_specs=[row(d), pl.BlockSpec((1, 6, d), lambda i, j: (i, 0, 0)), _resident(gn.shape),
                  _resident(win.shape), _resident(wbq.shape), _resident(wbkv.shape),
                  _resident(gaq.shape), _resident(gak.shape), _resident(gbql.shape), _resident(gbkvl.shape),
                  _resident(gbq.shape), _resident(gbk.shape)] + [tab] * 6,
        out_specs=[row(w) for w in widths],
        out_shape=[jax.ShapeDtypeStruct((b, s, w), BF16) for w in widths],
        compiler_params=_cparams(("arbitrary", "arbitrary")),
        name="proj",
    )(x, mod, gn, win, wbq, wbkv, gaq, gak, gbql, gbkvl, gbq, gbk, *tabs)


def _win_kernel(sink_ref, q_ref, kp_ref, kc_ref, kn_ref, vp_ref, vc_ref, vn_ref, kx_ref, vx_ref, o_ref, *, seq):
    j = pl.program_id(1)
    rows = A_GROUP * WBLK
    band = 3 * WBLK
    keys = band + kx_ref.shape[1]
    r_iota = lax.broadcasted_iota(I32, (rows, keys), 0)
    c_iota = lax.broadcasted_iota(I32, (rows, keys), 1)
    head_of_row = lax.broadcasted_iota(I32, (rows, 1), 0) // WBLK
    dn = (((1,), (1,)), ((), ()))

    def key_block(refs, t, cols):
        p_ref, c_ref, n_ref = refs
        if t == 0:
            return p_ref[0, :, cols]
        if t == WIN_Q + 1:
            return n_ref[0, :, cols]
        return c_ref[0, (t - 1) * WBLK:t * WBLK, cols]

    for sub in range(WIN_Q):
        n = j * WIN_Q + sub
        qpos = n * WBLK + (r_iota & (WBLK - 1))
        kpos = (n - 1) * WBLK + c_iota
        valid = ((jnp.abs(qpos - kpos) <= WINDOW) & (kpos >= 0) & (kpos < seq)) | (c_iota >= band)
        for kh in range(A_KV_HEADS):
            cs = slice(kh * HEAD_DIM, (kh + 1) * HEAD_DIM)
            vs = slice(kh * V_PAD, (kh + 1) * V_PAD)
            q = jnp.concatenate(
                [q_ref[0, sub * WBLK:(sub + 1) * WBLK, (kh * A_GROUP + g) * HEAD_DIM:(kh * A_GROUP + g + 1) * HEAD_DIM]
                 for g in range(A_GROUP)], axis=0)
            kb = jnp.concatenate([key_block((kp_ref, kc_ref, kn_ref), sub + t, cs) for t in range(3)]
                                 + [kx_ref[0, :, cs]], axis=0)
            vb = jnp.concatenate([key_block((vp_ref, vc_ref, vn_ref), sub + t, vs) for t in range(3)]
                                 + [vx_ref[0, :, vs]], axis=0)
            s = jnp.where(valid, lax.dot_general(q, kb, dn, preferred_element_type=F32), -jnp.inf)
            sink = jnp.zeros((rows, 1), F32)
            for g in range(A_GROUP):
                sink = jnp.where(head_of_row == g, sink_ref[kh * A_GROUP + g] * LOG2E, sink)
            m = jnp.maximum(jnp.max(s, axis=-1, keepdims=True), sink)
            acc = jnp.dot(jnp.exp2(s - m).astype(BF16), vb, preferred_element_type=F32)
            o = acc[:, :HEAD_DIM] / (acc[:, HEAD_DIM:HEAD_DIM + 1] + jnp.exp2(sink - m))
            for g in range(A_GROUP):
                hd = kh * A_GROUP + g
                o_ref[0, sub * WBLK:(sub + 1) * WBLK, hd * HEAD_DIM:(hd + 1) * HEAD_DIM] = (
                    o[g * WBLK:(g + 1) * WBLK].astype(BF16))


def _win_attn(sink, aq, ak, av, akc, avc):
    b, s, _ = aq.shape
    nb = s // WBLK
    c = akc.shape[1]
    tq = WIN_Q * WBLK
    prev = lambda w: pl.BlockSpec((1, WBLK, w), lambda i, j: (i, jnp.maximum(j * WIN_Q - 1, 0), 0))
    cur = lambda w: pl.BlockSpec((1, tq, w), lambda i, j: (i, j, 0))
    nxt = lambda w: pl.BlockSpec((1, WBLK, w), lambda i, j: (i, jnp.minimum((j + 1) * WIN_Q, nb - 1), 0))
    cx = lambda w: pl.BlockSpec((1, c, w), lambda i, j: (i, 0, 0))
    kw, vw = A_KV_HEADS * HEAD_DIM, A_KV_HEADS * V_PAD
    qo = pl.BlockSpec((1, tq, A_HEADS * HEAD_DIM), lambda i, j: (i, j, 0))
    return pl.pallas_call(
        functools.partial(_win_kernel, seq=s),
        grid=(b, nb // WIN_Q),
        in_specs=[pl.BlockSpec(memory_space=pltpu.SMEM), qo, prev(kw), cur(kw), nxt(kw), prev(vw), cur(vw), nxt(vw),
                  cx(kw), cx(vw)],
        out_specs=qo,
        out_shape=jax.ShapeDtypeStruct(aq.shape, BF16),
        compiler_params=_cparams(("arbitrary", "arbitrary")),
        name="win_attn",
    )(sink, aq, ak, ak, ak, av, av, av, akc, avc)


def _mla_kernel(q_ref, k_ref, v_ref, kx_ref, vx_ref, *rest, tk, n_side):
    side_in, o_ref, side_out = rest[:n_side], rest[n_side], rest[n_side + 1:2 * n_side + 1]
    s_a, s_b, acc_ref = rest[2 * n_side + 1:]
    for w_in, w_out in zip(side_in, side_out):
        w_out[...] = w_in[...].astype(BF16)
    q = q_ref[0]
    dn = (((1,), (1,)), ((), ()))
    n_chunks = k_ref.shape[1] // tk
    s_bufs = (s_a, s_b)

    def scores_into(buf, c):
        s = lax.dot_general(q, k_ref[0, c * tk:(c + 1) * tk, :], dn, preferred_element_type=F32)
        buf[...] = s
        return jnp.max(s, axis=-1, keepdims=True)

    s0 = lax.dot_general(q, kx_ref[0], dn, preferred_element_type=F32)
    m = jnp.max(s0, axis=-1, keepdims=True)
    acc_ref[...] = jnp.dot(jnp.exp2(s0 - m).astype(BF16), vx_ref[0], preferred_element_type=F32)
    mx = scores_into(s_bufs[0], 0)
    for c in range(n_chunks):
        if c + 1 < n_chunks:
            mx_next = scores_into(s_bufs[(c + 1) % 2], c + 1)
        m_new = jnp.maximum(m, mx)
        p = jnp.exp2(s_bufs[c % 2][...] - m_new).astype(BF16)
        acc_ref[...] = (jnp.exp2(m - m_new) * acc_ref[...]
                        + jnp.dot(p, v_ref[0, c * tk:(c + 1) * tk, :], preferred_element_type=F32))
        m, mx = m_new, mx_next
    acc = acc_ref[...]
    o_ref[0] = (acc[:, :V_DIM] / acc[:, V_DIM:V_DIM + 1]).astype(BF16)


def _mla_attn(bq, bk, bv, bkc, bvc, side, tq, tk):
    b, s, _ = bq.shape
    c = bkc.shape[1]
    nq = s // tq
    steps = b * B_HEADS * nq
    side2d = [w.reshape(-1, w.shape[-1]) for w in side]
    slab = lambda i, h, j: ((i * B_HEADS + h) * nq + j, 0)
    side_specs = [pl.BlockSpec((w.shape[0] // steps, w.shape[1]), slab) for w in side2d]
    res = pl.pallas_call(
        functools.partial(_mla_kernel, tk=tk, n_side=len(side)),
        grid=(b, B_HEADS, nq),
        in_specs=[
            pl.BlockSpec((1, tq, QK_PAD), lambda i, h, j: (i, j, h)),
            pl.BlockSpec((1, s, QK_PAD), lambda i, h, j: (i, 0, h)),
            pl.BlockSpec((1, s, V_PAD), lambda i, h, j: (i, 0, h)),
            pl.BlockSpec((1, c, QK_PAD), lambda i, h, j: (i, 0, h)),
            pl.BlockSpec((1, c, V_PAD), lambda i, h, j: (i, 0, h)),
        ] + side_specs,
        out_specs=[pl.BlockSpec((1, tq, V_DIM), lambda i, h, j: (i, j, h))] + side_specs,
        out_shape=[jax.ShapeDtypeStruct((b, s, B_HEADS * V_DIM), BF16)]
        + [jax.ShapeDtypeStruct(w.shape, BF16) for w in side2d],
        scratch_shapes=[pltpu.VMEM((tq, tk), F32), pltpu.VMEM((tq, tk), F32), pltpu.VMEM((tq, V_PAD), F32)],
        compiler_params=_cparams(("arbitrary", "arbitrary", "arbitrary")),
        name="mla_attn",
    )(bq, bk, bv, bkc, bvc, *side2d)
    return res[0], [o.reshape(w.shape) for o, w in zip(res[1:], side)]


def _route(sel, aff):
    scores = []
    for g in range(N_GROUPS):
        r = sel[g * EXP_PER_GROUP:(g + 1) * EXP_PER_GROUP]
        best = None
        for a in range(EXP_PER_GROUP):
            for b in range(a + 1, EXP_PER_GROUP):
                pair = r[a] + r[b]
                best = pair if best is None else jnp.maximum(best, pair)
        scores.append(best)
    top, grp = scores[0], jnp.zeros_like(scores[0], dtype=I32)
    for g in range(1, N_GROUPS):
        take = scores[g] > top
        grp = jnp.where(take, g, grp)
        top = jnp.where(take, scores[g], top)
    masked = [jnp.where(grp == e // EXP_PER_GROUP, sel[e], -jnp.inf) for e in range(N_EXPERTS)]

    def argmax_first(vals, skip=None):
        bv = jnp.full_like(vals[0], -jnp.inf)
        bi = jnp.full_like(grp, -1)
        for e in range(N_EXPERTS):
            take = vals[e] > bv
            if skip is not None:
                take = take & (skip != e)
            bi = jnp.where(take, e, bi)
            bv = jnp.where(take, vals[e], bv)
        return bi

    i0 = argmax_first(masked)
    i1 = argmax_first(masked, skip=i0)
    a0 = jnp.zeros_like(aff[0])
    a1 = jnp.zeros_like(aff[0])
    for e in range(N_EXPERTS):
        a0 = jnp.where(i0 == e, aff[e], a0)
        a1 = jnp.where(i1 == e, aff[e], a1)
    tot = a0 + a1
    return i0, i1, a0 / tot, a1 / tot


def _out_kernel(*refs, n_lhs):
    lhs = refs[:n_lhs]
    ws = refs[n_lhs:2 * n_lhs]
    bias_ref, x_ref, mod_ref, gn_ref, wr_ref, br_ref, x1_ref, hp_ref, idx_ref, wts_ref = refs[2 * n_lhs:]
    y = bias_ref[...]
    for a, w in zip(lhs, ws):
        y = y + jnp.dot(a[0], w[...], preferred_element_type=F32)
    x1 = x_ref[0] + mod_ref[0, 2:3, :] * y
    x1_ref[0] = x1
    h2 = (_rms(x1) * gn_ref[...]) * (1.0 + mod_ref[0, 4:5, :]) + mod_ref[0, 3:4, :]
    hp_ref[0] = h2
    w = wr_ref[...]
    w_head = w.astype(BF16)
    w_tail = (w - w_head.astype(F32)).astype(BF16)
    h_head = h2.astype(BF16)
    h_tail = (h2 - h_head.astype(F32)).astype(BF16)
    t = jnp.dot(h_head, jnp.concatenate([w_head, w_tail], axis=1), preferred_element_type=F32)
    logits = (t[:, :LANES] + t[:, LANES:]) + jnp.dot(h_tail, w_head, preferred_element_type=F32)
    lt = logits.T[:N_EXPERTS]
    aff_t = jax.nn.sigmoid(lt)
    sel_t = aff_t + br_ref[...]
    sel = [sel_t[e:e + 1] for e in range(N_EXPERTS)]
    aff = [aff_t[e:e + 1] for e in range(N_EXPERTS)]
    i0, i1, w0, w1 = _route(sel, aff)
    idx_ref[0] = jnp.concatenate([i0, i1], axis=0)
    w_rows = jnp.concatenate([w0, w1, jnp.zeros((LANES - TOP_K, w0.shape[1]), F32)], axis=0)
    wts_ref[0] = w_rows.T[:, :TOP_K]


def _out_proj(lhs, ws, bias, x, mod, gn, wr, br, tm):
    b, s, d = x.shape
    n_lhs = len(lhs)
    row = pl.BlockSpec((1, tm, d), lambda i, j: (i, j, 0))
    in_specs = ([pl.BlockSpec((1, tm, a.shape[-1]), lambda i, j: (i, j, 0)) for a in lhs]
                + [_resident(w.shape) for w in ws]
                + [_resident(bias.shape), row, pl.BlockSpec((1, 6, d), lambda i, j: (i, 0, 0)),
                   _resident(gn.shape), _resident(wr.shape), _resident(br.shape)])
    return pl.pallas_call(
        functools.partial(_out_kernel, n_lhs=n_lhs),
        grid=(b, s // tm),
        in_specs=in_specs,
        out_specs=[row, row,
                   pl.BlockSpec((1, TOP_K, tm), lambda i, j: (i, 0, j)),
                   pl.BlockSpec((1, tm, TOP_K), lambda i, j: (i, j, 0))],
        out_shape=[jax.ShapeDtypeStruct((b, s, d), F32), jax.ShapeDtypeStruct((b, s, d), F32),
                   jax.ShapeDtypeStruct((b, TOP_K, s), I32), jax.ShapeDtypeStruct((b, s, TOP_K), F32)],
        compiler_params=_cparams(("arbitrary", "arbitrary")),
        name="out_proj",
    )(*lhs, *ws, bias, x, mod, gn, wr, br)


def _dispatch(idx, blk):
    b, _, s = idx.shape
    n_asg = b * TOP_K * s
    e = idx.reshape(n_asg)
    onehot = (e[None, :] == jnp.arange(N_EXPERTS, dtype=I32)[:, None]).astype(I32)
    csum = jnp.cumsum(onehot, axis=1)
    counts = csum[:, -1]
    rank = jnp.sum((csum - onehot) * onehot, axis=0)
    padded = (counts + blk - 1) // blk * blk
    pad_end = jnp.cumsum(padded)
    pad_start = pad_end - padded
    start = jnp.cumsum(counts) - counts
    dest = pad_start[e] + rank
    src_asg = jnp.sort(e * n_asg + jnp.arange(n_asg, dtype=I32)) % n_asg
    src_tok = src_asg // (TOP_K * s) * s + src_asg % s
    src_tok = jnp.concatenate([src_tok, jnp.zeros((blk,), src_tok.dtype)])
    n_blocks = -(-n_asg // blk) + N_EXPERTS
    n_used = pad_end[-1] // blk
    blk_ids = jnp.minimum(jnp.arange(n_blocks, dtype=I32), n_used - 1)
    blk_e = jnp.sum((blk_ids[:, None] * blk >= pad_end[None, :]).astype(I32), axis=1)
    blk_e = jnp.minimum(blk_e, N_EXPERTS - 1)
    blk_lo = start[blk_e] + blk_ids * blk - pad_start[blk_e]
    as_i32 = lambda a: a.astype(I32)
    return as_i32(dest), as_i32(src_tok), as_i32(blk_e), as_i32(blk_lo), as_i32(n_used.reshape(1))


def _moe_kernel(blk_e_ref, lo_ref, n_used_ref, tok_ref, h_ref, wg_ref, wu_ref, wd_ref, o_ref, xbuf, sem, *, blk):
    i = pl.program_id(0)
    n_used = n_used_ref[0]
    slot = i % 2

    def gather(block, slot_, unrolled):
        lo = lo_ref[block]

        def row_copy(r):
            t = tok_ref[lo + r]
            pltpu.make_async_copy(h_ref.at[pl.ds(t, 1)], xbuf.at[slot_, pl.ds(r, 1)], sem.at[slot_]).start()

        if unrolled:
            for r in range(blk):
                row_copy(r)
        else:
            def body(r, _):
                row_copy(r)
                return 0
            lax.fori_loop(0, blk, body, 0, unroll=8)

    @pl.when(i == 0)
    def _():
        gather(0, 0, False)

    @pl.when(i + 1 < n_used)
    def _():
        gather(i + 1, 1 - slot, True)

    @pl.when(i < n_used)
    def _():
        pltpu.make_async_copy(xbuf.at[slot], xbuf.at[slot], sem.at[slot]).wait()
        xb = xbuf[slot].astype(BF16)
        g = jnp.dot(xb, wg_ref[0, 0], preferred_element_type=F32)
        u = jnp.dot(xb, wu_ref[0, 0], preferred_element_type=F32)
        a = (g * jax.nn.sigmoid(g) * u).astype(BF16)
        o_ref[...] = jnp.dot(a, wd_ref[0, 0], preferred_element_type=F32)

    @pl.when(i >= n_used)
    def _():
        o_ref[...] = jnp.zeros_like(o_ref)


def _moe(h2, src_tok, blk_e, blk_lo, n_used, wg, wu, wd, layer, blk):
    n_blocks = blk_e.shape[0]
    _, _, d, de = wg.shape
    grid_spec = pltpu.PrefetchScalarGridSpec(
        num_scalar_prefetch=4,
        grid=(n_blocks,),
        in_specs=[
            pl.BlockSpec(memory_space=pl.ANY),
            pl.BlockSpec((1, 1, d, de), lambda i, be, *_: (layer, be[i], 0, 0)),
            pl.BlockSpec((1, 1, d, de), lambda i, be, *_: (layer, be[i], 0, 0)),
            pl.BlockSpec((1, 1, de, d), lambda i, be, *_: (layer, be[i], 0, 0)),
        ],
        out_specs=pl.BlockSpec((blk, d), lambda i, *_: (i, 0)),
        scratch_shapes=[pltpu.VMEM((2, blk, d), F32), pltpu.SemaphoreType.DMA((2,))],
    )
    return pl.pallas_call(
        functools.partial(_moe_kernel, blk=blk),
        grid_spec=grid_spec,
        out_shape=jax.ShapeDtypeStruct((n_blocks * blk, d), F32),
        compiler_params=_cparams(("arbitrary",)),
        name="moe_ffn",
    )(blk_e, blk_lo, n_used, src_tok, h2, wg, wu, wd)


def _combine_kernel(dest_ref, yp_ref, x_ref, wt_ref, mod_ref, *rest, tm, nt, with_norm):
    if with_norm:
        gn_ref, modn_ref, o_ref, hn_ref, buf, sem = rest
    else:
        o_ref, buf, sem = rest
    i = pl.program_id(0)
    j = pl.program_id(1)
    step = i * nt + j
    slot = step % 2

    def gather(step_, slot_, unrolled):
        bases = [((step_ // nt) * TOP_K + k) * (nt * tm) + (step_ % nt) * tm for k in range(TOP_K)]

        def row_copies(r):
            for k in range(TOP_K):
                pltpu.make_async_copy(yp_ref.at[pl.ds(dest_ref[bases[k] + r], 1)],
                                      buf.at[slot_, k, pl.ds(r, 1)], sem.at[slot_]).start()

        if unrolled:
            for r in range(tm):
                row_copies(r)
        else:
            def body(r, _):
                row_copies(r)
                return 0
            lax.fori_loop(0, tm, body, 0, unroll=8)

    @pl.when(step == 0)
    def _():
        gather(0, 0, False)

    @pl.when(step + 1 < pl.num_programs(0) * nt)
    def _():
        gather(step + 1, 1 - slot, True)

    pltpu.make_async_copy(buf.at[slot], buf.at[slot], sem.at[slot]).wait()
    w = wt_ref[0]
    y = buf[slot, 0] * w[:, 0:1] + buf[slot, 1] * w[:, 1:2]
    out = x_ref[0] + mod_ref[0, 5:6, :] * y
    o_ref[0] = out
    if with_norm:
        hn_ref[0] = ((_rms(out) * gn_ref[...]) * (1.0 + modn_ref[0, 1:2, :]) + modn_ref[0, 0:1, :]).astype(BF16)


def _combine(dest, yp, x1, wts, mod, tm, norm=None):
    b, s, d = x1.shape
    nt = s // tm
    with_norm = norm is not None
    row = pl.BlockSpec((1, tm, d), lambda i, j, ds: (i, j, 0))
    modspec = pl.BlockSpec((1, 6, d), lambda i, j, ds: (i, 0, 0))
    in_specs = [pl.BlockSpec(memory_space=pl.ANY), row,
                pl.BlockSpec((1, tm, TOP_K), lambda i, j, ds: (i, j, 0)), modspec]
    args = [yp, x1, wts, mod]
    out_shape = [jax.ShapeDtypeStruct((b, s, d), F32)]
    out_specs = [row]
    if with_norm:
        gn, modn = norm
        in_specs += [pl.BlockSpec(gn.shape, lambda i, j, ds: (0, 0)), modspec]
        args += [gn, modn]
        out_shape.append(jax.ShapeDtypeStruct((b, s, d), BF16))
        out_specs.append(row)
    grid_spec = pltpu.PrefetchScalarGridSpec(
        num_scalar_prefetch=1, grid=(b, nt), in_specs=in_specs, out_specs=out_specs,
        scratch_shapes=[pltpu.VMEM((2, TOP_K, tm, d), F32), pltpu.SemaphoreType.DMA((2,))])
    return pl.pallas_call(
        functools.partial(_combine_kernel, tm=tm, nt=nt, with_norm=with_norm),
        grid_spec=grid_spec,
        out_shape=out_shape,
        compiler_params=_cparams(("arbitrary", "arbitrary")),
        name="combine",
    )(dest, *args)


def _dft_constants(seq):
    n1, n2 = DFT_N1, DFT_N2
    assert seq == n1 * n2 and n2 == n1 * n1
    c = np.arange(F_GROUP_DIM)
    ang = 2 * np.pi * np.outer(c, c) / F_GROUP_DIM
    fc = np.concatenate([np.cos(ang), -np.sin(ang)], axis=1)
    a, k2, m = np.meshgrid(np.arange(n1), np.arange(n2), np.arange(n2), indexing="ij")
    ang_a = -2 * np.pi * (k2 * (a + n1 * m) % seq) / seq
    tre, tim = np.cos(ang_a), np.sin(ang_a)
    ma = np.concatenate([np.concatenate([tre, -tim], axis=2), np.concatenate([tim, tre], axis=2)], axis=1)
    ang_b = -2 * np.pi * np.outer(np.arange(n1), np.arange(n1)) / n1
    eye = np.eye(n1)
    mb = np.concatenate([np.kron(np.cos(ang_b), eye), -np.kron(np.sin(ang_b), eye)], axis=1)
    mb = mb / math.sqrt(seq * F_GROUP_DIM)
    return tuple(jnp.asarray(t, F32).astype(BF16) for t in (fc, ma, mb))


def _fourier_kernel(h_ref, fc_ref, ma_ref, mb_ref, o_ref, z_ref, yre, yim):
    n1, n2, gd = DFT_N1, DFT_N2, F_GROUP_DIM
    rows = 512
    n_tiles = 2 * gd // LANES
    for r in range(0, h_ref.shape[1], rows):
        z = jnp.dot(h_ref[0, r:r + rows, :], fc_ref[...], preferred_element_type=F32)
        for t in range(n_tiles):
            z_ref[t, r:r + rows, :] = z[:, t * LANES:(t + 1) * LANES]
    for a in range(n1):
        zs = [z_ref[t, pl.ds(a, n2, stride=n1), :].astype(BF16) for t in range(n_tiles)]
        rhs = jnp.concatenate([jnp.concatenate(zs[:n_tiles // 2], axis=1),
                               jnp.concatenate(zs[n_tiles // 2:], axis=1)], axis=0)
        y = jnp.dot(ma_ref[a], rhs, preferred_element_type=F32).astype(BF16)
        yre[a] = y[:n2]
        yim[a] = y[n2:]
    for hi in range(n1):
        rhs = jnp.concatenate([yre[a, hi * n1:(hi + 1) * n1, :] for a in range(n1)]
                              + [yim[a, hi * n1:(hi + 1) * n1, :] for a in range(n1)], axis=0)
        out = jnp.dot(mb_ref[...], rhs, preferred_element_type=F32).astype(BF16)
        for k1 in range(n1):
            o_ref[0, k1 * n2 + hi * n1:k1 * n2 + (hi + 1) * n1, :] = out[k1 * n1:(k1 + 1) * n1]


def _fourier(hn, consts):
    b, s, d = hn.shape
    fc, ma, mb = consts
    gd = F_GROUP_DIM
    return pl.pallas_call(
        _fourier_kernel,
        grid=(b, F_GROUPS),
        in_specs=[pl.BlockSpec((1, s, gd), lambda i, g: (i, 0, g)),
                  _resident(fc.shape), _resident(ma.shape), _resident(mb.shape)],
        out_specs=pl.BlockSpec((1, s, gd), lambda i, g: (i, 0, g)),
        out_shape=jax.ShapeDtypeStruct((b, s, d), BF16),
        scratch_shapes=[pltpu.VMEM((2 * gd // LANES, s, LANES), F32),
                        pltpu.VMEM((DFT_N1, DFT_N2, gd), BF16), pltpu.VMEM((DFT_N1, DFT_N2, gd), BF16)],
        compiler_params=_cparams(("arbitrary", "arbitrary")),
        name="fourier",
    )(hn, fc, ma, mb)


def _rope_tables(seq):
    pos = np.arange(seq)
    row, col = (pos // GRID_W).astype(np.float64), (pos % GRID_W).astype(np.float64)

    def tables(width):
        half = width // 2
        quarter = half // 2
        freqs = ROPE_BASE ** (-np.arange(0, half, 2, dtype=np.float64) / half)
        lane = np.arange(LANES)
        ang = np.where((lane < half)[None, :], row[:, None], col[:, None]) * freqs[lane % quarter][None, :]
        live = (lane < width)[None, :]
        first = ((lane % half) < quarter)[None, :]
        cos = np.where(live, np.cos(ang), 1.0)
        sneg = np.where(live & first, -np.sin(ang), 0.0)
        spos = np.where(live & ~first, np.sin(ang), 0.0)
        return [jnp.asarray(t, F32) for t in (cos, sneg, spos)]

    return tables(HEAD_DIM) + tables(ROPE_DIM)


def _identity_tables(n):
    one, zero = jnp.ones((n, LANES), F32), jnp.zeros((n, LANES), F32)
    return [one, zero, zero, one, zero, zero]


def _pad_heads(w, lead):
    w = w.reshape(lead, B_HEADS, QK_DIM)
    return jnp.pad(w, ((0, 0), (0, 0), (0, QK_PAD - QK_DIM))).reshape(lead, B_HEADS * QK_PAD)


def kernel(x, c, ctx, c_ctx, w_ada, b_ada, g_norm, w_in, g_aqn, g_akn, g_bq_lat, w_bq_up, g_bkv_lat, w_bkv_up,
           g_bqn, g_bkn, sink, w_o_ab, w_fo, b_fo, w_router, b_router, w_gate, w_up, w_down):
    b, s, d = x.shape
    n_ctx = ctx.shape[1]

    crows = jnp.concatenate([c, c_ctx[None, :], jnp.zeros((8 - b - 1, d), F32)], axis=0)
    mods, (win, wo, wfo) = _ada(crows, w_ada, b_ada, (w_in[0], w_o_ab[0], w_fo[0]), (IN_PAD, d, d))
    mods = mods.reshape(DEPTH, 8, 6, d)
    mod_lat = [mods[l, :b] for l in range(DEPTH)]
    mod_ctx = jnp.broadcast_to(mods[0, b][None], (b, 6, d))

    wr = jnp.pad(w_router, ((0, 0), (0, LANES - N_EXPERTS)))
    br = b_router.reshape(N_EXPERTS, 1)

    wbq = _pad_heads(w_bq_up[0], Q_LORA).astype(BF16)
    wkv = w_bkv_up[0].reshape(KV_LORA, B_HEADS, NOPE_DIM + V_DIM)
    wbkv = jnp.concatenate([wkv[:, :, :NOPE_DIM].reshape(KV_LORA, -1), wkv[:, :, NOPE_DIM:].reshape(KV_LORA, -1)],
                           axis=1).astype(BF16)
    gains = (g_aqn[0][None], g_akn[0][None], g_bq_lat[0][None], g_bkv_lat[0][None],
             jnp.pad(g_bqn[0][None], ((0, 0), (0, QK_PAD - QK_DIM))),
             jnp.pad(g_bkn[0][None], ((0, 0), (0, QK_PAD - QK_DIM))))
    gn0 = g_norm[0, 0][None]
    aq, ak, av, bq, bk, bv = _proj(x, mod_lat[0], gn0, win, wbq, wbkv, *gains, _rope_tables(s), 256)
    _, akc, avc, _, bkc, bvc = _proj(ctx, mod_ctx, gn0, win, wbq, wbkv, *gains, _identity_tables(n_ctx), n_ctx)

    ya = _win_attn(sink[0], aq, ak, av, akc, avc)
    yb, (wg, wu, wd) = _mla_attn(bq, bk, bv, bkc, bvc, (w_gate, w_up, w_down), 1024, 2048)

    n_a = A_HEADS * HEAD_DIM
    x1, h2, idx, wts = _out_proj([ya, yb], [wo[:n_a], wo[n_a:]], jnp.zeros((1, d), F32), x, mod_lat[0],
                                 g_norm[0, 1][None], wr, br, 512)
    dest, *plan = _dispatch(idx, MOE_BLK)
    yp = _moe(h2.reshape(b * s, d), *plan, wg, wu, wd, 0, MOE_BLK)
    x2, hn = _combine(dest, yp, x1, wts, mod_lat[0], 256,
                      norm=(g_norm[1, 0][None], mod_lat[1]))

    f = _fourier(hn, _dft_constants(s))
    x3, h2, idx, wts = _out_proj([f], [wfo], b_fo[0][None], x2, mod_lat[1],
                                 g_norm[1, 1][None], wr, br, 512)
    dest, *plan = _dispatch(idx, MOE_BLK)
    yp = _moe(h2.reshape(b * s, d), *plan, wg, wu, wd, 1, MOE_BLK)
    (x4,) = _combine(dest, yp, x3, wts, mod_lat[1], 256)
    return x4
```

```python
import functools
import math

import numpy as np
import jax
import jax.numpy as jnp
from jax import lax
from jax.experimental import pallas as pl
from jax.experimental.pallas import tpu as pltpu

F32 = jnp.float32
BF16 = jnp.bfloat16
I32 = jnp.int32
HIGHEST = lax.Precision.HIGHEST

D_MODEL = 2048
DEPTH = 2
GRID_W = 64
HEAD_DIM = 128
A_HEADS = 8
A_KV_HEADS = 2
A_GROUP = A_HEADS // A_KV_HEADS
WINDOW = 128
WBLK = 128
B_HEADS = 8
Q_LORA = 512
KV_LORA = 256
NOPE_DIM = 128
ROPE_DIM = 64
V_DIM = 128
QK_DIM = NOPE_DIM + ROPE_DIM
QK_PAD = 256
V_PAD = 256
IN_SPLITS = (A_HEADS * HEAD_DIM, A_KV_HEADS * HEAD_DIM, A_KV_HEADS * HEAD_DIM, Q_LORA, KV_LORA, ROPE_DIM)
IN_WIDTH = sum(IN_SPLITS)
IN_PAD = 2432
F_GROUPS = 8
F_GROUP_DIM = D_MODEL // F_GROUPS
N_EXPERTS = 16
N_GROUPS = 4
EXP_PER_GROUP = N_EXPERTS // N_GROUPS
TOP_K = 2
D_EXPERT = 1024
ROPE_BASE = 10000.0
EPS = 1e-6
LOG2E = math.log2(math.e)
LANES = 128

MOE_BLK = 512
WIN_Q = 4
DFT_N1 = 16
DFT_N2 = 256
VMEM_LIMIT = 56 * 1024 * 1024


def _cparams(sem, **kw):
    return pltpu.CompilerParams(dimension_semantics=sem, vmem_limit_bytes=VMEM_LIMIT, **kw)


def _resident(shape):
    nd = len(shape)
    return pl.BlockSpec(shape, lambda *_: (0,) * nd, pipeline_mode=pl.Buffered(1))


def _rms(t, width=None):
    n = t.shape[-1] if width is None else width
    ss = jnp.sum(t * t, axis=-1, keepdims=True)
    return t * lax.rsqrt(ss * (1.0 / n) + EPS)


def _rope(t, cos, sneg, spos, dist):
    n = t.shape[-1]
    return t * cos + pltpu.roll(t, n - dist, 1) * sneg + pltpu.roll(t, dist, 1) * spos


ADA_SIDE_STEPS = 16


def _ada_kernel(c_ref, w_ref, b_ref, *rest, n_side, nj):
    side_in, o_ref, side_out = rest[:n_side], rest[n_side], rest[n_side + 1:]
    c = c_ref[...]
    s = c * jax.nn.sigmoid(c)
    o_ref[0] = jnp.dot(s, w_ref[0], precision=HIGHEST, preferred_element_type=F32) + b_ref[0]

    @pl.when(pl.program_id(0) * nj + pl.program_id(1) < ADA_SIDE_STEPS)
    def _():
        for w_in, w_out in zip(side_in, side_out):
            width = w_in.shape[1]
            w_out[:, :width] = w_in[...].astype(BF16)
            if w_out.shape[1] > width:
                w_out[:, width:] = jnp.zeros((w_out.shape[0], w_out.shape[1] - width), BF16)


def _ada(crows, w_ada, b_ada, side, side_widths):
    depth, d, n = w_ada.shape
    tn = 1024
    nj = n // tn
    assert depth * nj >= ADA_SIDE_STEPS
    slab = lambda l, j: (jnp.minimum(l * nj + j, ADA_SIDE_STEPS - 1), 0)
    slab3 = lambda l, j: (0,) + slab(l, j)
    assert all(w.ndim == 3 and w.shape[0] == 1 for w in side)
    side_in = [pl.BlockSpec((None, w.shape[1] // ADA_SIDE_STEPS, w.shape[2]), slab3) for w in side]
    side_out = [pl.BlockSpec((w.shape[1] // ADA_SIDE_STEPS, wd), slab) for w, wd in zip(side, side_widths)]
    res = pl.pallas_call(
        functools.partial(_ada_kernel, n_side=len(side), nj=nj),
        grid=(depth, nj),
        in_specs=[
            pl.BlockSpec((8, d), lambda l, j: (0, 0)),
            pl.BlockSpec((1, d, tn), lambda l, j: (l, 0, j)),
            pl.BlockSpec((1, 1, tn), lambda l, j: (l, 0, j)),
        ] + side_in,
        out_specs=[pl.BlockSpec((1, 8, tn), lambda l, j: (l, 0, j))] + side_out,
        out_shape=[jax.ShapeDtypeStruct((depth, 8, n), F32)]
        + [jax.ShapeDtypeStruct((w.shape[1], wd), BF16) for w, wd in zip(side, side_widths)],
        compiler_params=_cparams(("arbitrary", "arbitrary")),
        name="ada",
    )(crows, w_ada, b_ada.reshape(depth, 1, n), *side)
    return res[0], res[1:]


def _proj_kernel(x_ref, mod_ref, gn_ref, win_ref, wbq_ref, wbkv_ref, gaq_ref, gak_ref, gbql_ref, gbkvl_ref,
                 gbq_ref, gbk_ref, ca_ref, sna_ref, spa_ref, cb_ref, snb_ref, spb_ref,
                 aq_ref, ak_ref, av_ref, bq_ref, bk_ref, bv_ref):
    x = x_ref[0]
    shift = mod_ref[0, 0:1, :]
    scale = mod_ref[0, 1:2, :]
    hb = ((_rms(x) * gn_ref[...]) * (1.0 + scale) + shift).astype(BF16)

    def cols(w_ref, lhs, lo, width):
        return jnp.dot(lhs, w_ref[:, lo:lo + width], preferred_element_type=F32)

    ca, sna, spa = ca_ref[...], sna_ref[...], spa_ref[...]
    cb, snb, spb = cb_ref[...], snb_ref[...], spb_ref[...]
    a_scale = HEAD_DIM ** -0.5 * LOG2E
    b_scale = QK_DIM ** -0.5 * LOG2E
    pair = 2 * HEAD_DIM

    for hp in range(A_HEADS // 2):
        pp = cols(win_ref, hb, hp * pair, pair)
        for j in range(2):
            hd = 2 * hp + j
            t = _rms(pp[:, j * HEAD_DIM:(j + 1) * HEAD_DIM]) * gaq_ref[...]
            aq_ref[0, :, hd * HEAD_DIM:(hd + 1) * HEAD_DIM] = (_rope(t, ca, sna, spa, 32) * a_scale).astype(BF16)
    off = A_HEADS * HEAD_DIM
    pp = cols(win_ref, hb, off, A_KV_HEADS * HEAD_DIM)
    for kh in range(A_KV_HEADS):
        t = _rms(pp[:, kh * HEAD_DIM:(kh + 1) * HEAD_DIM]) * gak_ref[...]
        ak_ref[0, :, kh * HEAD_DIM:(kh + 1) * HEAD_DIM] = _rope(t, ca, sna, spa, 32).astype(BF16)
    off += A_KV_HEADS * HEAD_DIM
    ones_col = (lax.broadcasted_iota(I32, (hb.shape[0], V_PAD - V_DIM), 1) == 0).astype(BF16)
    pp = cols(win_ref, hb, off, A_KV_HEADS * HEAD_DIM)
    for kh in range(A_KV_HEADS):
        av_ref[0, :, kh * V_PAD:kh * V_PAD + HEAD_DIM] = pp[:, kh * HEAD_DIM:(kh + 1) * HEAD_DIM].astype(BF16)
        av_ref[0, :, kh * V_PAD + HEAD_DIM:(kh + 1) * V_PAD] = ones_col
    off += A_KV_HEADS * HEAD_DIM

    ql = (_rms(cols(win_ref, hb, off, Q_LORA)) * gbql_ref[...]).astype(BF16)
    off += Q_LORA
    for hd in range(B_HEADS):
        t = _rms(cols(wbq_ref, ql, hd * QK_PAD, QK_PAD), QK_DIM) * gbq_ref[...]
        bq_ref[0, :, hd * QK_PAD:hd * QK_PAD + NOPE_DIM] = (t[:, :NOPE_DIM] * b_scale).astype(BF16)
        bq_ref[0, :, hd * QK_PAD + NOPE_DIM:(hd + 1) * QK_PAD] = (
            _rope(t[:, NOPE_DIM:], cb, snb, spb, 16) * b_scale).astype(BF16)

    kvl = (_rms(cols(win_ref, hb, off, KV_LORA)) * gbkvl_ref[...]).astype(BF16)
    off += KV_LORA
    kr = cols(win_ref, hb, off, LANES)
    kr_ss = jnp.sum(kr * kr, axis=-1, keepdims=True)
    kr_rot = _rope(kr * gbk_ref[:, NOPE_DIM:], cb, snb, spb, 16)
    for hp in range(B_HEADS // 2):
        kn2 = cols(wbkv_ref, kvl, hp * pair, pair)
        v2 = cols(wbkv_ref, kvl, B_HEADS * NOPE_DIM + hp * pair, pair)
        for j in range(2):
            hd = 2 * hp + j
            kn = kn2[:, j * NOPE_DIM:(j + 1) * NOPE_DIM]
            ss = jnp.sum(kn * kn, axis=-1, keepdims=True) + kr_ss
            r = lax.rsqrt(ss * (1.0 / QK_DIM) + EPS)
            bk_ref[0, :, hd * QK_PAD:hd * QK_PAD + NOPE_DIM] = (kn * r * gbk_ref[:, :NOPE_DIM]).astype(BF16)
            bk_ref[0, :, hd * QK_PAD + NOPE_DIM:(hd + 1) * QK_PAD] = (kr_rot * r).astype(BF16)
            bv_ref[0, :, hd * V_PAD:hd * V_PAD + V_DIM] = v2[:, j * V_DIM:(j + 1) * V_DIM].astype(BF16)
            bv_ref[0, :, hd * V_PAD + V_DIM:(hd + 1) * V_PAD] = ones_col


def _proj(x, mod, gn, win, wbq, wbkv, gaq, gak, gbql, gbkvl, gbq, gbk, tabs, tm):
    b, s, d = x.shape
    row = lambda w: pl.BlockSpec((1, tm, w), lambda i, j: (i, j, 0))
    tab = pl.BlockSpec((tm, LANES), lambda i, j: (j, 0))
    widths = (A_HEADS * HEAD_DIM, A_KV_HEADS * HEAD_DIM, A_KV_HEADS * V_PAD,
              B_HEADS * QK_PAD, B_HEADS * QK_PAD, B_HEADS * V_PAD)
    return pl.pallas_call(
        _proj_kernel,
        grid=(b, s // tm),
        in_specs=[row(d), pl.BlockSpec((1, 6, d), lambda i, j: (i, 0, 0)), _resident(gn.shape),
                  _resident(win.shape), _resident(wbq.shape), _resident(wbkv.shape),
                  _resident(gaq.shape), _resident(gak.shape), _resident(gbql.shape), _resident(gbkvl.shape),
                  _resident(gbq.shape), _resident(gbk.shape)] + [tab] * 6,
        out_specs=[row(w) for w in widths],
        out_shape=[jax.ShapeDtypeStruct((b, s, w), BF16) for w in widths],
        compiler_params=_cparams(("arbitrary", "arbitrary")),
        name="proj",
    )(x, mod, gn, win, wbq, wbkv, gaq, gak, gbql, gbkvl, gbq, gbk, *tabs)


def _win_kernel(sink_ref, q_ref, kp_ref, kc_ref, kn_ref, vp_ref, vc_ref, vn_ref, kx_ref, vx_ref, o_ref, *, seq):
    j = pl.program_id(1)
    rows = A_GROUP * WBLK
    band = 3 * WBLK
    keys = band + kx_ref.shape[1]
    r_iota = lax.broadcasted_iota(I32, (rows, keys), 0)
    c_iota = lax.broadcasted_iota(I32, (rows, keys), 1)
    head_of_row = lax.broadcasted_iota(I32, (rows, 1), 0) // WBLK
    dn = (((1,), (1,)), ((), ()))

    def key_block(refs, t, cols):
        p_ref, c_ref, n_ref = refs
        if t == 0:
            return p_ref[0, :, cols]
        if t == WIN_Q + 1:
            return n_ref[0, :, cols]
        return c_ref[0, (t - 1) * WBLK:t * WBLK, cols]

    for sub in range(WIN_Q):
        n = j * WIN_Q + sub
        qpos = n * WBLK + (r_iota & (WBLK - 1))
        kpos = (n - 1) * WBLK + c_iota
        valid = ((jnp.abs(qpos - kpos) <= WINDOW) & (kpos >= 0) & (kpos < seq)) | (c_iota >= band)
        for kh in range(A_KV_HEADS):
            cs = slice(kh * HEAD_DIM, (kh + 1) * HEAD_DIM)
            vs = slice(kh * V_PAD, (kh + 1) * V_PAD)
            q = jnp.concatenate(
                [q_ref[0, sub * WBLK:(sub + 1) * WBLK, (kh * A_GROUP + g) * HEAD_DIM:(kh * A_GROUP + g + 1) * HEAD_DIM]
                 for g in range(A_GROUP)], axis=0)
            kb = jnp.concatenate([key_block((kp_ref, kc_ref, kn_ref), sub + t, cs) for t in range(3)]
                                 + [kx_ref[0, :, cs]], axis=0)
            vb = jnp.concatenate([key_block((vp_ref, vc_ref, vn_ref), sub + t, vs) for t in range(3)]
                                 + [vx_ref[0, :, vs]], axis=0)
            s = jnp.where(valid, lax.dot_general(q, kb, dn, preferred_element_type=F32), -jnp.inf)
            sink = jnp.zeros((rows, 1), F32)
            for g in range(A_GROUP):
                sink = jnp.where(head_of_row == g, sink_ref[kh * A_GROUP + g] * LOG2E, sink)
            m = jnp.maximum(jnp.max(s, axis=-1, keepdims=True), sink)
            acc = jnp.dot(jnp.exp2(s - m).astype(BF16), vb, preferred_element_type=F32)
            o = acc[:, :HEAD_DIM] / (acc[:, HEAD_DIM:HEAD_DIM + 1] + jnp.exp2(sink - m))
            for g in range(A_GROUP):
                hd = kh * A_GROUP + g
                o_ref[0, sub * WBLK:(sub + 1) * WBLK, hd * HEAD_DIM:(hd + 1) * HEAD_DIM] = (
                    o[g * WBLK:(g + 1) * WBLK].astype(BF16))


def _win_attn(sink, aq, ak, av, akc, avc):
    b, s, _ = aq.shape
    nb = s // WBLK
    c = akc.shape[1]
    tq = WIN_Q * WBLK
    prev = lambda w: pl.BlockSpec((1, WBLK, w), lambda i, j: (i, jnp.maximum(j * WIN_Q - 1, 0), 0))
    cur = lambda w: pl.BlockSpec((1, tq, w), lambda i, j: (i, j, 0))
    nxt = lambda w: pl.BlockSpec((1, WBLK, w), lambda i, j: (i, jnp.minimum((j + 1) * WIN_Q, nb - 1), 0))
    cx = lambda w: pl.BlockSpec((1, c, w), lambda i, j: (i, 0, 0))
    kw, vw = A_KV_HEADS * HEAD_DIM, A_KV_HEADS * V_PAD
    qo = pl.BlockSpec((1, tq, A_HEADS * HEAD_DIM), lambda i, j: (i, j, 0))
    return pl.pallas_call(
        functools.partial(_win_kernel, seq=s),
        grid=(b, nb // WIN_Q),
        in_specs=[pl.BlockSpec(memory_space=pltpu.SMEM), qo, prev(kw), cur(kw), nxt(kw), prev(vw), cur(vw), nxt(vw),
                  cx(kw), cx(vw)],
        out_specs=qo,
        out_shape=jax.ShapeDtypeStruct(aq.shape, BF16),
        compiler_params=_cparams(("arbitrary", "arbitrary")),
        name="win_attn",
    )(sink, aq, ak, ak, ak, av, av, av, akc, avc)


def _mla_kernel(q_ref, k_ref, v_ref, kx_ref, vx_ref, *rest, tk, n_side):
    side_in, o_ref, side_out = rest[:n_side], rest[n_side], rest[n_side + 1:2 * n_side + 1]
    s_a, s_b, acc_ref = rest[2 * n_side + 1:]
    for w_in, w_out in zip(side_in, side_out):
        w_out[...] = w_in[...].astype(BF16)
    q = q_ref[0]
    dn = (((1,), (1,)), ((), ()))
    n_chunks = k_ref.shape[1] // tk
    s_bufs = (s_a, s_b)

    def scores_into(buf, c):
        s = lax.dot_general(q, k_ref[0, c * tk:(c + 1) * tk, :], dn, preferred_element_type=F32)
        buf[...] = s
        return jnp.max(s, axis=-1, keepdims=True)

    s0 = lax.dot_general(q, kx_ref[0], dn, preferred_element_type=F32)
    m = jnp.max(s0, axis=-1, keepdims=True)
    acc_ref[...] = jnp.dot(jnp.exp2(s0 - m).astype(BF16), vx_ref[0], preferred_element_type=F32)
    mx = scores_into(s_bufs[0], 0)
    for c in range(n_chunks):
        if c + 1 < n_chunks:
            mx_next = scores_into(s_bufs[(c + 1) % 2], c + 1)
        m_new = jnp.maximum(m, mx)
        p = jnp.exp2(s_bufs[c % 2][...] - m_new).astype(BF16)
        acc_ref[...] = (jnp.exp2(m - m_new) * acc_ref[...]
                        + jnp.dot(p, v_ref[0, c * tk:(c + 1) * tk, :], preferred_element_type=F32))
        m, mx = m_new, mx_next
    acc = acc_ref[...]
    o_ref[0] = (acc[:, :V_DIM] / acc[:, V_DIM:V_DIM + 1]).astype(BF16)


def _mla_attn(bq, bk, bv, bkc, bvc, side, tq, tk):
    b, s, _ = bq.shape
    c = bkc.shape[1]
    nq = s // tq
    steps = b * B_HEADS * nq
    side2d = [w.reshape(-1, w.shape[-1]) for w in side]
    slab = lambda i, h, j: ((i * B_HEADS + h) * nq + j, 0)
    side_specs = [pl.BlockSpec((w.shape[0] // steps, w.shape[1]), slab) for w in side2d]
    res = pl.pallas_call(
        functools.partial(_mla_kernel, tk=tk, n_side=len(side)),
        grid=(b, B_HEADS, nq),
        in_specs=[
            pl.BlockSpec((1, tq, QK_PAD), lambda i, h, j: (i, j, h)),
            pl.BlockSpec((1, s, QK_PAD), lambda i, h, j: (i, 0, h)),
            pl.BlockSpec((1, s, V_PAD), lambda i, h, j: (i, 0, h)),
            pl.BlockSpec((1, c, QK_PAD), lambda i, h, j: (i, 0, h)),
            pl.BlockSpec((1, c, V_PAD), lambda i, h, j: (i, 0, h)),
        ] + side_specs,
        out_specs=[pl.BlockSpec((1, tq, V_DIM), lambda i, h, j: (i, j, h))] + side_specs,
        out_shape=[jax.ShapeDtypeStruct((b, s, B_HEADS * V_DIM), BF16)]
        + [jax.ShapeDtypeStruct(w.shape, BF16) for w in side2d],
        scratch_shapes=[pltpu.VMEM((tq, tk), F32), pltpu.VMEM((tq, tk), F32), pltpu.VMEM((tq, V_PAD), F32)],
        compiler_params=_cparams(("arbitrary", "arbitrary", "arbitrary")),
        name="mla_attn",
    )(bq, bk, bv, bkc, bvc, *side2d)
    return res[0], [o.reshape(w.shape) for o, w in zip(res[1:], side)]


def _route(sel, aff):
    scores = []
    for g in range(N_GROUPS):
        r = sel[g * EXP_PER_GROUP:(g + 1) * EXP_PER_GROUP]
        best = None
        for a in range(EXP_PER_GROUP):
            for b in range(a + 1, EXP_PER_GROUP):
                pair = r[a] + r[b]
                best = pair if best is None else jnp.maximum(best, pair)
        scores.append(best)
    top, grp = scores[0], jnp.zeros_like(scores[0], dtype=I32)
    for g in range(1, N_GROUPS):
        take = scores[g] > top
        grp = jnp.where(take, g, grp)
        top = jnp.where(take, scores[g], top)
    masked = [jnp.where(grp == e // EXP_PER_GROUP, sel[e], -jnp.inf) for e in range(N_EXPERTS)]

    def argmax_first(vals, skip=None):
        bv = jnp.full_like(vals[0], -jnp.inf)
        bi = jnp.full_like(grp, -1)
        for e in range(N_EXPERTS):
            take = vals[e] > bv
            if skip is not None:
                take = take & (skip != e)
            bi = jnp.where(take, e, bi)
            bv = jnp.where(take, vals[e], bv)
        return bi

    i0 = argmax_first(masked)
    i1 = argmax_first(masked, skip=i0)
    a0 = jnp.zeros_like(aff[0])
    a1 = jnp.zeros_like(aff[0])
    for e in range(N_EXPERTS):
        a0 = jnp.where(i0 == e, aff[e], a0)
        a1 = jnp.where(i1 == e, aff[e], a1)
    tot = a0 + a1
    return i0, i1, a0 / tot, a1 / tot


def _out_kernel(*refs, n_lhs):
    lhs = refs[:n_lhs]
    ws = refs[n_lhs:2 * n_lhs]
    bias_ref, x_ref, mod_ref, gn_ref, wr_ref, br_ref, x1_ref, hp_ref, idx_ref, wts_ref = refs[2 * n_lhs:]
    y = bias_ref[...]
    for a, w in zip(lhs, ws):
        y = y + jnp.dot(a[0], w[...], preferred_element_type=F32)
    x1 = x_ref[0] + mod_ref[0, 2:3, :] * y
    x1_ref[0] = x1
    h2 = (_rms(x1) * gn_ref[...]) * (1.0 + mod_ref[0, 4:5, :]) + mod_ref[0, 3:4, :]
    hp_ref[0] = h2
    w = wr_ref[...]
    w_head = w.astype(BF16)
    w_tail = (w - w_head.astype(F32)).astype(BF16)
    h_head = h2.astype(BF16)
    h_tail = (h2 - h_head.astype(F32)).astype(BF16)
    t = jnp.dot(h_head, jnp.concatenate([w_head, w_tail], axis=1), preferred_element_type=F32)
    logits = (t[:, :LANES] + t[:, LANES:]) + jnp.dot(h_tail, w_head, preferred_element_type=F32)
    lt = logits.T[:N_EXPERTS]
    aff_t = jax.nn.sigmoid(lt)
    sel_t = aff_t + br_ref[...]
    sel = [sel_t[e:e + 1] for e in range(N_EXPERTS)]
    aff = [aff_t[e:e + 1] for e in range(N_EXPERTS)]
    i0, i1, w0, w1 = _route(sel, aff)
    idx_ref[0] = jnp.concatenate([i0, i1], axis=0)
    w_rows = jnp.concatenate([w0, w1, jnp.zeros((LANES - TOP_K, w0.shape[1]), F32)], axis=0)
    wts_ref[0] = w_rows.T[:, :TOP_K]


def _out_proj(lhs, ws, bias, x, mod, gn, wr, br, tm):
    b, s, d = x.shape
    n_lhs = len(lhs)
    row = pl.BlockSpec((1, tm, d), lambda i, j: (i, j, 0))
    in_specs = ([pl.BlockSpec((1, tm, a.shape[-1]), lambda i, j: (i, j, 0)) for a in lhs]
                + [_resident(w.shape) for w in ws]
                + [_resident(bias.shape), row, pl.BlockSpec((1, 6, d), lambda i, j: (i, 0, 0)),
                   _resident(gn.shape), _resident(wr.shape), _resident(br.shape)])
    return pl.pallas_call(
        functools.partial(_out_kernel, n_lhs=n_lhs),
        grid=(b, s // tm),
        in_specs=in_specs,
        out_specs=[row, row,
                   pl.BlockSpec((1, TOP_K, tm), lambda i, j: (i, 0, j)),
                   pl.BlockSpec((1, tm, TOP_K), lambda i, j: (i, j, 0))],
        out_shape=[jax.ShapeDtypeStruct((b, s, d), F32), jax.ShapeDtypeStruct((b, s, d), F32),
                   jax.ShapeDtypeStruct((b, TOP_K, s), I32), jax.ShapeDtypeStruct((b, s, TOP_K), F32)],
        compiler_params=_cparams(("arbitrary", "arbitrary")),
        name="out_proj",
    )(*lhs, *ws, bias, x, mod, gn, wr, br)


def _dispatch(idx, blk):
    b, _, s = idx.shape
    n_asg = b * TOP_K * s
    e = idx.reshape(n_asg)
    onehot = (e[None, :] == jnp.arange(N_EXPERTS, dtype=I32)[:, None]).astype(I32)
    csum = jnp.cumsum(onehot, axis=1)
    counts = csum[:, -1]
    rank = jnp.sum((csum - onehot) * onehot, axis=0)
    padded = (counts + blk - 1) // blk * blk
    pad_end = jnp.cumsum(padded)
    pad_start = pad_end - padded
    start = jnp.cumsum(counts) - counts
    dest = pad_start[e] + rank
    src_asg = jnp.sort(e * n_asg + jnp.arange(n_asg, dtype=I32)) % n_asg
    src_tok = src_asg // (TOP_K * s) * s + src_asg % s
    src_tok = jnp.concatenate([src_tok, jnp.zeros((blk,), src_tok.dtype)])
    n_blocks = -(-n_asg // blk) + N_EXPERTS
    n_used = pad_end[-1] // blk
    blk_ids = jnp.minimum(jnp.arange(n_blocks, dtype=I32), n_used - 1)
    blk_e = jnp.sum((blk_ids[:, None] * blk >= pad_end[None, :]).astype(I32), axis=1)
    blk_e = jnp.minimum(blk_e, N_EXPERTS - 1)
    blk_lo = start[blk_e] + blk_ids * blk - pad_start[blk_e]
    blk_rows = jnp.clip(start[blk_e] + counts[blk_e] - blk_lo, 0, blk)
    as_i32 = lambda a: a.astype(I32)
    return (as_i32(dest), as_i32(src_tok), as_i32(blk_e), as_i32(blk_lo), as_i32(blk_rows),
            as_i32(n_used.reshape(1)))


def _moe_kernel(blk_e_ref, lo_ref, rows_ref, n_used_ref, tok_ref, h_ref, wg_ref, wu_ref, wd_ref, o_ref, xbuf, sem,
                *, blk):
    i = pl.program_id(0)
    n_used = n_used_ref[0]
    slot = i % 2
    half = blk // 2

    def gather(block, slot_, unrolled):
        lo = lo_ref[block]
        short = rows_ref[block] <= half

        def row_copy(r):
            t = tok_ref[lo + r]
            pltpu.make_async_copy(h_ref.at[pl.ds(t, 1)], xbuf.at[slot_, pl.ds(r, 1)], sem.at[slot_]).start()

        if unrolled:
            for r in range(half):
                row_copy(r)

            @pl.when(jnp.logical_not(short))
            def _():
                for r in range(half, blk):
                    row_copy(r)
        else:
            def body(r, _):
                row_copy(r)
                return 0
            lax.fori_loop(0, jnp.where(short, half, blk), body, 0)

    def ffn(rows):
        pltpu.make_async_copy(xbuf.at[slot, pl.ds(0, rows)], xbuf.at[slot, pl.ds(0, rows)], sem.at[slot]).wait()
        xb = xbuf[slot, :rows, :].astype(BF16)
        g = jnp.dot(xb, wg_ref[0, 0], preferred_element_type=F32)
        u = jnp.dot(xb, wu_ref[0, 0], preferred_element_type=F32)
        a = (g * jax.nn.sigmoid(g) * u).astype(BF16)
        o_ref[:rows, :] = jnp.dot(a, wd_ref[0, 0], preferred_element_type=F32)
        if rows < blk:
            o_ref[rows:, :] = jnp.zeros((blk - rows, o_ref.shape[1]), F32)

    @pl.when(i == 0)
    def _():
        gather(0, 0, False)

    @pl.when(i + 1 < n_used)
    def _():
        gather(i + 1, 1 - slot, True)

    used = i < n_used
    short = rows_ref[i] <= half

    @pl.when(used & jnp.logical_not(short))
    def _():
        ffn(blk)

    @pl.when(used & short)
    def _():
        ffn(half)

    @pl.when(jnp.logical_not(used))
    def _():
        o_ref[...] = jnp.zeros_like(o_ref)


def _moe(h2, src_tok, blk_e, blk_lo, blk_rows, n_used, wg, wu, wd, layer, blk):
    n_blocks = blk_e.shape[0]
    _, _, d, de = wg.shape
    grid_spec = pltpu.PrefetchScalarGridSpec(
        num_scalar_prefetch=5,
        grid=(n_blocks,),
        in_specs=[
            pl.BlockSpec(memory_space=pl.ANY),
            pl.BlockSpec((1, 1, d, de), lambda i, be, *_: (layer, be[i], 0, 0)),
            pl.BlockSpec((1, 1, d, de), lambda i, be, *_: (layer, be[i], 0, 0)),
            pl.BlockSpec((1, 1, de, d), lambda i, be, *_: (layer, be[i], 0, 0)),
        ],
        out_specs=pl.BlockSpec((blk, d), lambda i, *_: (i, 0)),
        scratch_shapes=[pltpu.VMEM((2, blk, d), F32), pltpu.SemaphoreType.DMA((2,))],
    )
    return pl.pallas_call(
        functools.partial(_moe_kernel, blk=blk),
        grid_spec=grid_spec,
        out_shape=jax.ShapeDtypeStruct((n_blocks * blk, d), F32),
        compiler_params=_cparams(("arbitrary",)),
        name="moe_ffn",
    )(blk_e, blk_lo, blk_rows, n_used, src_tok, h2, wg, wu, wd)


def _combine_kernel(dest_ref, yp_ref, x_ref, wt_ref, mod_ref, *rest, tm, nt, with_norm):
    if with_norm:
        gn_ref, modn_ref, o_ref, hn_ref, buf, sem = rest
    else:
        o_ref, buf, sem = rest
    i = pl.program_id(0)
    j = pl.program_id(1)
    step = i * nt + j
    slot = step % 2

    def gather(step_, slot_, unrolled):
        bases = [((step_ // nt) * TOP_K + k) * (nt * tm) + (step_ % nt) * tm for k in range(TOP_K)]

        def row_copies(r):
            for k in range(TOP_K):
                pltpu.make_async_copy(yp_ref.at[pl.ds(dest_ref[bases[k] + r], 1)],
                                      buf.at[slot_, k, pl.ds(r, 1)], sem.at[slot_]).start()

        if unrolled:
            for r in range(tm):
                row_copies(r)
        else:
            def body(r, _):
                row_copies(r)
                return 0
            lax.fori_loop(0, tm, body, 0, unroll=8)

    @pl.when(step == 0)
    def _():
        gather(0, 0, False)

    @pl.when(step + 1 < pl.num_programs(0) * nt)
    def _():
        gather(step + 1, 1 - slot, True)

    pltpu.make_async_copy(buf.at[slot], buf.at[slot], sem.at[slot]).wait()
    w = wt_ref[0]
    y = buf[slot, 0] * w[:, 0:1] + buf[slot, 1] * w[:, 1:2]
    out = x_ref[0] + mod_ref[0, 5:6, :] * y
    o_ref[0] = out
    if with_norm:
        hn_ref[0] = ((_rms(out) * gn_ref[...]) * (1.0 + modn_ref[0, 1:2, :]) + modn_ref[0, 0:1, :]).astype(BF16)


def _combine(dest, yp, x1, wts, mod, tm, norm=None):
    b, s, d = x1.shape
    nt = s // tm
    with_norm = norm is not None
    row = pl.BlockSpec((1, tm, d), lambda i, j, ds: (i, j, 0))
    modspec = pl.BlockSpec((1, 6, d), lambda i, j, ds: (i, 0, 0))
    in_specs = [pl.BlockSpec(memory_space=pl.ANY), row,
                pl.BlockSpec((1, tm, TOP_K), lambda i, j, ds: (i, j, 0)), modspec]
    args = [yp, x1, wts, mod]
    out_shape = [jax.ShapeDtypeStruct((b, s, d), F32)]
    out_specs = [row]
    if with_norm:
        gn, modn = norm
        in_specs += [pl.BlockSpec(gn.shape, lambda i, j, ds: (0, 0)), modspec]
        args += [gn, modn]
        out_shape.append(jax.ShapeDtypeStruct((b, s, d), BF16))
        out_specs.append(row)
    grid_spec = pltpu.PrefetchScalarGridSpec(
        num_scalar_prefetch=1, grid=(b, nt), in_specs=in_specs, out_specs=out_specs,
        scratch_shapes=[pltpu.VMEM((2, TOP_K, tm, d), F32), pltpu.SemaphoreType.DMA((2,))])
    return pl.pallas_call(
        functools.partial(_combine_kernel, tm=tm, nt=nt, with_norm=with_norm),
        grid_spec=grid_spec,
        out_shape=out_shape,
        compiler_params=_cparams(("arbitrary", "arbitrary")),
        name="combine",
    )(dest, *args)


def _dft_constants(seq):
    n1, n2 = DFT_N1, DFT_N2
    assert seq == n1 * n2 and n2 == n1 * n1
    c = np.arange(F_GROUP_DIM)
    ang = 2 * np.pi * np.outer(c, c) / F_GROUP_DIM
    fc = np.concatenate([np.cos(ang), -np.sin(ang)], axis=1)
    a, k2, m = np.meshgrid(np.arange(n1), np.arange(n2), np.arange(n2), indexing="ij")
    ang_a = -2 * np.pi * (k2 * (a + n1 * m) % seq) / seq
    tre, tim = np.cos(ang_a), np.sin(ang_a)
    ma = np.concatenate([np.concatenate([tre, -tim], axis=2), np.concatenate([tim, tre], axis=2)], axis=1)
    ang_b = -2 * np.pi * np.outer(np.arange(n1), np.arange(n1)) / n1
    eye = np.eye(n1)
    mb = np.concatenate([np.kron(np.cos(ang_b), eye), -np.kron(np.sin(ang_b), eye)], axis=1)
    mb = mb / math.sqrt(seq * F_GROUP_DIM)
    return tuple(jnp.asarray(t, F32).astype(BF16) for t in (fc, ma, mb))


def _fourier_kernel(h_ref, fc_ref, ma_ref, mb_ref, o_ref, z_ref, yre, yim):
    n1, n2, gd = DFT_N1, DFT_N2, F_GROUP_DIM
    rows = 512
    n_tiles = 2 * gd // LANES
    for r in range(0, h_ref.shape[1], rows):
        z = jnp.dot(h_ref[0, r:r + rows, :], fc_ref[...], preferred_element_type=F32)
        for t in range(n_tiles):
            z_ref[t, r:r + rows, :] = z[:, t * LANES:(t + 1) * LANES]
    for a in range(n1):
        zs = [z_ref[t, pl.ds(a, n2, stride=n1), :].astype(BF16) for t in range(n_tiles)]
        rhs = jnp.concatenate([jnp.concatenate(zs[:n_tiles // 2], axis=1),
                               jnp.concatenate(zs[n_tiles // 2:], axis=1)], axis=0)
        y = jnp.dot(ma_ref[a], rhs, preferred_element_type=F32).astype(BF16)
        yre[a] = y[:n2]
        yim[a] = y[n2:]
    for hi in range(n1):
        rhs = jnp.concatenate([yre[a, hi * n1:(hi + 1) * n1, :] for a in range(n1)]
                              + [yim[a, hi * n1:(hi + 1) * n1, :] for a in range(n1)], axis=0)
        out = jnp.dot(mb_ref[...], rhs, preferred_element_type=F32).astype(BF16)
        for k1 in range(n1):
            o_ref[0, k1 * n2 + hi * n1:k1 * n2 + (hi + 1) * n1, :] = out[k1 * n1:(k1 + 1) * n1]


def _fourier(hn, consts):
    b, s, d = hn.shape
    fc, ma, mb = consts
    gd = F_GROUP_DIM
    return pl.pallas_call(
        _fourier_kernel,
        grid=(b, F_GROUPS),
        in_specs=[pl.BlockSpec((1, s, gd), lambda i, g: (i, 0, g)),
                  _resident(fc.shape), _resident(ma.shape), _resident(mb.shape)],
        out_specs=pl.BlockSpec((1, s, gd), lambda i, g: (i, 0, g)),
        out_shape=jax.ShapeDtypeStruct((b, s, d), BF16),
        scratch_shapes=[pltpu.VMEM((2 * gd // LANES, s, LANES), F32),
                        pltpu.VMEM((DFT_N1, DFT_N2, gd), BF16), pltpu.VMEM((DFT_N1, DFT_N2, gd), BF16)],
        compiler_params=_cparams(("arbitrary", "arbitrary")),
        name="fourier",
    )(hn, fc, ma, mb)


def _rope_tables(seq):
    pos = np.arange(seq)
    row, col = (pos // GRID_W).astype(np.float64), (pos % GRID_W).astype(np.float64)

    def tables(width):
        half = width // 2
        quarter = half // 2
        freqs = ROPE_BASE ** (-np.arange(0, half, 2, dtype=np.float64) / half)
        lane = np.arange(LANES)
        ang = np.where((lane < half)[None, :], row[:, None], col[:, None]) * freqs[lane % quarter][None, :]
        live = (lane < width)[None, :]
        first = ((lane % half) < quarter)[None, :]
        cos = np.where(live, np.cos(ang), 1.0)
        sneg = np.where(live & first, -np.sin(ang), 0.0)
        spos = np.where(live & ~first, np.sin(ang), 0.0)
        return [jnp.asarray(t, F32) for t in (cos, sneg, spos)]

    return tables(HEAD_DIM) + tables(ROPE_DIM)


def _identity_tables(n):
    one, zero = jnp.ones((n, LANES), F32), jnp.zeros((n, LANES), F32)
    return [one, zero, zero, one, zero, zero]


def _pad_heads(w, lead):
    w = w.reshape(lead, B_HEADS, QK_DIM)
    return jnp.pad(w, ((0, 0), (0, 0), (0, QK_PAD - QK_DIM))).reshape(lead, B_HEADS * QK_PAD)


def kernel(x, c, ctx, c_ctx, w_ada, b_ada, g_norm, w_in, g_aqn, g_akn, g_bq_lat, w_bq_up, g_bkv_lat, w_bkv_up,
           g_bqn, g_bkn, sink, w_o_ab, w_fo, b_fo, w_router, b_router, w_gate, w_up, w_down):
    b, s, d = x.shape
    n_ctx = ctx.shape[1]

    crows = jnp.concatenate([c, c_ctx[None, :], jnp.zeros((8 - b - 1, d), F32)], axis=0)
    mods, (win, wo, wfo) = _ada(crows, w_ada, b_ada, (w_in, w_o_ab, w_fo), (IN_PAD, d, d))
    mods = mods.reshape(DEPTH, 8, 6, d)
    mod_lat = [mods[l, :b] for l in range(DEPTH)]
    mod_ctx = jnp.broadcast_to(mods[0, b][None], (b, 6, d))

    wr = jnp.pad(w_router, ((0, 0), (0, LANES - N_EXPERTS)))
    br = b_router.reshape(N_EXPERTS, 1)

    wbq = _pad_heads(w_bq_up[0], Q_LORA).astype(BF16)
    wkv = w_bkv_up[0].reshape(KV_LORA, B_HEADS, NOPE_DIM + V_DIM)
    wbkv = jnp.concatenate([wkv[:, :, :NOPE_DIM].reshape(KV_LORA, -1), wkv[:, :, NOPE_DIM:].reshape(KV_LORA, -1)],
                           axis=1).astype(BF16)
    gains = (g_aqn[0][None], g_akn[0][None], g_bq_lat[0][None], g_bkv_lat[0][None],
             jnp.pad(g_bqn[0][None], ((0, 0), (0, QK_PAD - QK_DIM))),
             jnp.pad(g_bkn[0][None], ((0, 0), (0, QK_PAD - QK_DIM))))
    gn0 = g_norm[0, 0][None]
    aq, ak, av, bq, bk, bv = _proj(x, mod_lat[0], gn0, win, wbq, wbkv, *gains, _rope_tables(s), 256)
    _, akc, avc, _, bkc, bvc = _proj(ctx, mod_ctx, gn0, win, wbq, wbkv, *gains, _identity_tables(n_ctx), n_ctx)

    ya = _win_attn(sink[0], aq, ak, av, akc, avc)
    yb, (wg, wu, wd) = _mla_attn(bq, bk, bv, bkc, bvc, (w_gate, w_up, w_down), 1024, 2048)

    n_a = A_HEADS * HEAD_DIM
    x1, h2, idx, wts = _out_proj([ya, yb], [wo[:n_a], wo[n_a:]], jnp.zeros((1, d), F32), x, mod_lat[0],
                                 g_norm[0, 1][None], wr, br, 512)
    dest, *plan = _dispatch(idx, MOE_BLK)
    yp = _moe(h2.reshape(b * s, d), *plan, wg, wu, wd, 0, MOE_BLK)
    x2, hn = _combine(dest, yp, x1, wts, mod_lat[0], 256,
                      norm=(g_norm[1, 0][None], mod_lat[1]))

    f = _fourier(hn, _dft_constants(s))
    x3, h2, idx, wts = _out_proj([f], [wfo], b_fo[0][None], x2, mod_lat[1],
                                 g_norm[1, 1][None], wr, br, 512)
    dest, *plan = _dispatch(idx, MOE_BLK)
    yp = _moe(h2.reshape(b * s, d), *plan, wg, wu, wd, 1, MOE_BLK)
    (x4,) = _combine(dest, yp, x3, wts, mod_lat[1], 256)
    return x4
```

```python
import functools
import math

import numpy as np
import jax
import jax.numpy as jnp
from jax import lax
from jax.experimental import pallas as pl
from jax.experimental.pallas import tpu as pltpu

F32 = jnp.float32
BF16 = jnp.bfloat16
I32 = jnp.int32
HIGHEST = lax.Precision.HIGHEST

D_MODEL = 2048
DEPTH = 2
GRID_W = 64
HEAD_DIM = 128
A_HEADS = 8
A_KV_HEADS = 2
A_GROUP = A_HEADS // A_KV_HEADS
WINDOW = 128
WBLK = 128
B_HEADS = 8
Q_LORA = 512
KV_LORA = 256
NOPE_DIM = 128
ROPE_DIM = 64
V_DIM = 128
QK_DIM = NOPE_DIM + ROPE_DIM
QK_PAD = 256
V_PAD = 256
IN_SPLITS = (A_HEADS * HEAD_DIM, A_KV_HEADS * HEAD_DIM, A_KV_HEADS * HEAD_DIM, Q_LORA, KV_LORA, ROPE_DIM)
IN_WIDTH = sum(IN_SPLITS)
IN_PAD = 2432
F_GROUPS = 8
F_GROUP_DIM = D_MODEL // F_GROUPS
N_EXPERTS = 16
N_GROUPS = 4
EXP_PER_GROUP = N_EXPERTS // N_GROUPS
TOP_K = 2
D_EXPERT = 1024
ROPE_BASE = 10000.0
EPS = 1e-6
LOG2E = math.log2(math.e)
LANES = 128

MOE_BLK = 512
WIN_Q = 8
PROJ_TM = 256
OUT_TM = 512
COMBINE_TM = 256
MLA_TQ = 1024
MLA_TK = 2048
ADA_TN = 1024
FOURIER_ROWS = 512
DFT_N1 = 16
DFT_N2 = 256
VMEM_LIMIT = 56 * 1024 * 1024


def _cparams(sem, **kw):
    return pltpu.CompilerParams(dimension_semantics=sem, vmem_limit_bytes=VMEM_LIMIT, **kw)


def _resident(shape):
    nd = len(shape)
    return pl.BlockSpec(shape, lambda *_: (0,) * nd, pipeline_mode=pl.Buffered(1))


def _rms(t, width=None):
    n = t.shape[-1] if width is None else width
    ss = jnp.sum(t * t, axis=-1, keepdims=True)
    return t * lax.rsqrt(ss * (1.0 / n) + EPS)


def _rope(t, cos, sneg, spos, dist):
    n = t.shape[-1]
    return t * cos + pltpu.roll(t, n - dist, 1) * sneg + pltpu.roll(t, dist, 1) * spos


ADA_SIDE_STEPS = 16


def _ada_kernel(c_ref, w_ref, b_ref, *rest, n_side, nj):
    side_in, o_ref, side_out = rest[:n_side], rest[n_side], rest[n_side + 1:]
    c = c_ref[...]
    s = c * jax.nn.sigmoid(c)
    o_ref[0] = jnp.dot(s, w_ref[0], precision=HIGHEST, preferred_element_type=F32) + b_ref[0]

    @pl.when(pl.program_id(0) * nj + pl.program_id(1) < ADA_SIDE_STEPS)
    def _():
        for w_in, w_out in zip(side_in, side_out):
            width = w_in.shape[1]
            w_out[:, :width] = w_in[...].astype(BF16)
            if w_out.shape[1] > width:
                w_out[:, width:] = jnp.zeros((w_out.shape[0], w_out.shape[1] - width), BF16)


def _ada(crows, w_ada, b_ada, side, side_widths):
    depth, d, n = w_ada.shape
    tn = ADA_TN
    nj = n // tn
    assert depth * nj >= ADA_SIDE_STEPS
    slab = lambda l, j: (jnp.minimum(l * nj + j, ADA_SIDE_STEPS - 1), 0)
    slab3 = lambda l, j: (0,) + slab(l, j)
    assert all(w.ndim == 3 and w.shape[0] == 1 for w in side)
    side_in = [pl.BlockSpec((None, w.shape[1] // ADA_SIDE_STEPS, w.shape[2]), slab3) for w in side]
    side_out = [pl.BlockSpec((w.shape[1] // ADA_SIDE_STEPS, wd), slab) for w, wd in zip(side, side_widths)]
    res = pl.pallas_call(
        functools.partial(_ada_kernel, n_side=len(side), nj=nj),
        grid=(depth, nj),
        in_specs=[
            pl.BlockSpec((8, d), lambda l, j: (0, 0)),
            pl.BlockSpec((1, d, tn), lambda l, j: (l, 0, j)),
            pl.BlockSpec((1, 1, tn), lambda l, j: (l, 0, j)),
        ] + side_in,
        out_specs=[pl.BlockSpec((1, 8, tn), lambda l, j: (l, 0, j))] + side_out,
        out_shape=[jax.ShapeDtypeStruct((depth, 8, n), F32)]
        + [jax.ShapeDtypeStruct((w.shape[1], wd), BF16) for w, wd in zip(side, side_widths)],
        compiler_params=_cparams(("arbitrary", "arbitrary")),
        name="ada",
    )(crows, w_ada, b_ada.reshape(depth, 1, n), *side)
    return res[0], res[1:]


def _proj_kernel(x_ref, mod_ref, gn_ref, win_ref, wbq_ref, wbkv_ref, gaq_ref, gak_ref, gbql_ref, gbkvl_ref,
                 gbq_ref, gbk_ref, ca_ref, sna_ref, spa_ref, cb_ref, snb_ref, spb_ref,
                 aq_ref, ak_ref, av_ref, bq_ref, bk_ref, bv_ref):
    x = x_ref[0]
    shift = mod_ref[0, 0:1, :]
    scale = mod_ref[0, 1:2, :]
    hb = ((_rms(x) * gn_ref[...]) * (1.0 + scale) + shift).astype(BF16)

    def cols(w_ref, lhs, lo, width):
        return jnp.dot(lhs, w_ref[:, lo:lo + width], preferred_element_type=F32)

    ca, sna, spa = ca_ref[...], sna_ref[...], spa_ref[...]
    cb, snb, spb = cb_ref[...], snb_ref[...], spb_ref[...]
    a_scale = HEAD_DIM ** -0.5 * LOG2E
    b_scale = QK_DIM ** -0.5 * LOG2E
    pair = 2 * HEAD_DIM

    for hp in range(A_HEADS // 2):
        pp = cols(win_ref, hb, hp * pair, pair)
        for j in range(2):
            hd = 2 * hp + j
            t = _rms(pp[:, j * HEAD_DIM:(j + 1) * HEAD_DIM]) * gaq_ref[...]
            aq_ref[0, :, hd * HEAD_DIM:(hd + 1) * HEAD_DIM] = (_rope(t, ca, sna, spa, HEAD_DIM // 4) * a_scale).astype(BF16)
    off = A_HEADS * HEAD_DIM
    pp = cols(win_ref, hb, off, A_KV_HEADS * HEAD_DIM)
    for kh in range(A_KV_HEADS):
        t = _rms(pp[:, kh * HEAD_DIM:(kh + 1) * HEAD_DIM]) * gak_ref[...]
        ak_ref[0, :, kh * HEAD_DIM:(kh + 1) * HEAD_DIM] = _rope(t, ca, sna, spa, HEAD_DIM // 4).astype(BF16)
    off += A_KV_HEADS * HEAD_DIM
    ones_col = (lax.broadcasted_iota(I32, (hb.shape[0], V_PAD - V_DIM), 1) == 0).astype(BF16)
    pp = cols(win_ref, hb, off, A_KV_HEADS * HEAD_DIM)
    for kh in range(A_KV_HEADS):
        av_ref[0, :, kh * V_PAD:kh * V_PAD + HEAD_DIM] = pp[:, kh * HEAD_DIM:(kh + 1) * HEAD_DIM].astype(BF16)
        av_ref[0, :, kh * V_PAD + HEAD_DIM:(kh + 1) * V_PAD] = ones_col
    off += A_KV_HEADS * HEAD_DIM

    ql = (_rms(cols(win_ref, hb, off, Q_LORA)) * gbql_ref[...]).astype(BF16)
    off += Q_LORA
    for hd in range(B_HEADS):
        t = _rms(cols(wbq_ref, ql, hd * QK_PAD, QK_PAD), QK_DIM) * gbq_ref[...]
        bq_ref[0, :, hd * QK_PAD:hd * QK_PAD + NOPE_DIM] = (t[:, :NOPE_DIM] * b_scale).astype(BF16)
        bq_ref[0, :, hd * QK_PAD + NOPE_DIM:(hd + 1) * QK_PAD] = (
            _rope(t[:, NOPE_DIM:], cb, snb, spb, ROPE_DIM // 4) * b_scale).astype(BF16)

    kvl = (_rms(cols(win_ref, hb, off, KV_LORA)) * gbkvl_ref[...]).astype(BF16)
    off += KV_LORA
    kr = cols(win_ref, hb, off, LANES)
    kr_ss = jnp.sum(kr * kr, axis=-1, keepdims=True)
    kr_rot = _rope(kr * gbk_ref[:, NOPE_DIM:], cb, snb, spb, ROPE_DIM // 4)
    for hp in range(B_HEADS // 2):
        kn2 = cols(wbkv_ref, kvl, hp * pair, pair)
        v2 = cols(wbkv_ref, kvl, B_HEADS * NOPE_DIM + hp * pair, pair)
        for j in range(2):
            hd = 2 * hp + j
            kn = kn2[:, j * NOPE_DIM:(j + 1) * NOPE_DIM]
            ss = jnp.sum(kn * kn, axis=-1, keepdims=True) + kr_ss
            r = lax.rsqrt(ss * (1.0 / QK_DIM) + EPS)
            bk_ref[0, :, hd * QK_PAD:hd * QK_PAD + NOPE_DIM] = (kn * r * gbk_ref[:, :NOPE_DIM]).astype(BF16)
            bk_ref[0, :, hd * QK_PAD + NOPE_DIM:(hd + 1) * QK_PAD] = (kr_rot * r).astype(BF16)
            bv_ref[0, :, hd * V_PAD:hd * V_PAD + V_DIM] = v2[:, j * V_DIM:(j + 1) * V_DIM].astype(BF16)
            bv_ref[0, :, hd * V_PAD + V_DIM:(hd + 1) * V_PAD] = ones_col


def _proj(x, mod, gn, win, wbq, wbkv, gaq, gak, gbql, gbkvl, gbq, gbk, tabs, tm):
    b, s, d = x.shape
    row = lambda w: pl.BlockSpec((1, tm, w), lambda i, j: (i, j, 0))
    tab = pl.BlockSpec((tm, LANES), lambda i, j: (j, 0))
    widths = (A_HEADS * HEAD_DIM, A_KV_HEADS * HEAD_DIM, A_KV_HEADS * V_PAD,
              B_HEADS * QK_PAD, B_HEADS * QK_PAD, B_HEADS * V_PAD)
    return pl.pallas_call(
        _proj_kernel,
        grid=(b, s // tm),
        in_specs=[row(d), pl.BlockSpec((1, 6, d), lambda i, j: (i, 0, 0)), _resident(gn.shape),
                  _resident(win.shape), _resident(wbq.shape), _resident(wbkv.shape),
                  _resident(gaq.shape), _resident(gak.shape), _resident(gbql.shape), _resident(gbkvl.shape),
                  _resident(gbq.shape), _resident(gbk.shape)] + [tab] * 6,
        out_specs=[row(w) for w in widths],
        out_shape=[jax.ShapeDtypeStruct((b, s, w), BF16) for w in widths],
        compiler_params=_cparams(("arbitrary", "arbitrary")),
        name="proj",
    )(x, mod, gn, win, wbq, wbkv, gaq, gak, gbql, gbkvl, gbq, gbk, *tabs)


def _win_kernel(sink_ref, q_ref, kp_ref, kc_ref, kn_ref, vp_ref, vc_ref, vn_ref, kx_ref, vx_ref, o_ref, *, seq):
    j = pl.program_id(1)
    rows = A_GROUP * WBLK
    band = 3 * WBLK
    keys = band + kx_ref.shape[1]
    r_iota = lax.broadcasted_iota(I32, (rows, keys), 0)
    c_iota = lax.broadcasted_iota(I32, (rows, keys), 1)
    head_of_row = lax.broadcasted_iota(I32, (rows, 1), 0) // WBLK
    dn = (((1,), (1,)), ((), ()))

    def key_block(refs, t, cols):
        p_ref, c_ref, n_ref = refs
        if t == 0:
            return p_ref[0, :, cols]
        if t == WIN_Q + 1:
            return n_ref[0, :, cols]
        return c_ref[0, (t - 1) * WBLK:t * WBLK, cols]

    for sub in range(WIN_Q):
        n = j * WIN_Q + sub
        qpos = n * WBLK + (r_iota & (WBLK - 1))
        kpos = (n - 1) * WBLK + c_iota
        valid = ((jnp.abs(qpos - kpos) <= WINDOW) & (kpos >= 0) & (kpos < seq)) | (c_iota >= band)
        for kh in range(A_KV_HEADS):
            cs = slice(kh * HEAD_DIM, (kh + 1) * HEAD_DIM)
            vs = slice(kh * V_PAD, (kh + 1) * V_PAD)
            q = jnp.concatenate(
                [q_ref[0, sub * WBLK:(sub + 1) * WBLK, (kh * A_GROUP + g) * HEAD_DIM:(kh * A_GROUP + g + 1) * HEAD_DIM]
                 for g in range(A_GROUP)], axis=0)
            kb = jnp.concatenate([key_block((kp_ref, kc_ref, kn_ref), sub + t, cs) for t in range(3)]
                                 + [kx_ref[0, :, cs]], axis=0)
            vb = jnp.concatenate([key_block((vp_ref, vc_ref, vn_ref), sub + t, vs) for t in range(3)]
                                 + [vx_ref[0, :, vs]], axis=0)
            s = jnp.where(valid, lax.dot_general(q, kb, dn, preferred_element_type=F32), -jnp.inf)
            sink = jnp.zeros((rows, 1), F32)
            for g in range(A_GROUP):
                sink = jnp.where(head_of_row == g, sink_ref[kh * A_GROUP + g] * LOG2E, sink)
            m = jnp.maximum(jnp.max(s, axis=-1, keepdims=True), sink)
            acc = jnp.dot(jnp.exp2(s - m).astype(BF16), vb, preferred_element_type=F32)
            o = acc[:, :HEAD_DIM] / (acc[:, HEAD_DIM:HEAD_DIM + 1] + jnp.exp2(sink - m))
            for g in range(A_GROUP):
                hd = kh * A_GROUP + g
                o_ref[0, sub * WBLK:(sub + 1) * WBLK, hd * HEAD_DIM:(hd + 1) * HEAD_DIM] = (
                    o[g * WBLK:(g + 1) * WBLK].astype(BF16))


def _win_attn(sink, aq, ak, av, akc, avc):
    b, s, _ = aq.shape
    nb = s // WBLK
    c = akc.shape[1]
    tq = WIN_Q * WBLK
    prev = lambda w: pl.BlockSpec((1, WBLK, w), lambda i, j: (i, jnp.maximum(j * WIN_Q - 1, 0), 0))
    cur = lambda w: pl.BlockSpec((1, tq, w), lambda i, j: (i, j, 0))
    nxt = lambda w: pl.BlockSpec((1, WBLK, w), lambda i, j: (i, jnp.minimum((j + 1) * WIN_Q, nb - 1), 0))
    cx = lambda w: pl.BlockSpec((1, c, w), lambda i, j: (i, 0, 0))
    kw, vw = A_KV_HEADS * HEAD_DIM, A_KV_HEADS * V_PAD
    qo = pl.BlockSpec((1, tq, A_HEADS * HEAD_DIM), lambda i, j: (i, j, 0))
    return pl.pallas_call(
        functools.partial(_win_kernel, seq=s),
        grid=(b, nb // WIN_Q),
        in_specs=[pl.BlockSpec(memory_space=pltpu.SMEM), qo, prev(kw), cur(kw), nxt(kw), prev(vw), cur(vw), nxt(vw),
                  cx(kw), cx(vw)],
        out_specs=qo,
        out_shape=jax.ShapeDtypeStruct(aq.shape, BF16),
        compiler_params=_cparams(("arbitrary", "arbitrary")),
        name="win_attn",
    )(sink, aq, ak, ak, ak, av, av, av, akc, avc)


def _mla_kernel(q_ref, k_ref, v_ref, kx_ref, vx_ref, *rest, tk, n_side):
    side_in, o_ref, side_out = rest[:n_side], rest[n_side], rest[n_side + 1:2 * n_side + 1]
    s_a, s_b, acc_ref = rest[2 * n_side + 1:]
    for w_in, w_out in zip(side_in, side_out):
        w_out[...] = w_in[...].astype(BF16)
    q = q_ref[0]
    dn = (((1,), (1,)), ((), ()))
    n_chunks = k_ref.shape[1] // tk
    s_bufs = (s_a, s_b)

    def scores_into(buf, c):
        s = lax.dot_general(q, k_ref[0, c * tk:(c + 1) * tk, :], dn, preferred_element_type=F32)
        buf[...] = s
        return jnp.max(s, axis=-1, keepdims=True)

    s0 = lax.dot_general(q, kx_ref[0], dn, preferred_element_type=F32)
    m = jnp.max(s0, axis=-1, keepdims=True)
    acc_ref[...] = jnp.dot(jnp.exp2(s0 - m).astype(BF16), vx_ref[0], preferred_element_type=F32)
    mx = scores_into(s_bufs[0], 0)
    for c in range(n_chunks):
        if c + 1 < n_chunks:
            mx_next = scores_into(s_bufs[(c + 1) % 2], c + 1)
        m_new = jnp.maximum(m, mx)
        p = jnp.exp2(s_bufs[c % 2][...] - m_new).astype(BF16)
        acc_ref[...] = (jnp.exp2(m - m_new) * acc_ref[...]
                        + jnp.dot(p, v_ref[0, c * tk:(c + 1) * tk, :], preferred_element_type=F32))
        m, mx = m_new, mx_next
    acc = acc_ref[...]
    o_ref[0] = (acc[:, :V_DIM] / acc[:, V_DIM:V_DIM + 1]).astype(BF16)


def _mla_attn(bq, bk, bv, bkc, bvc, side, tq, tk):
    b, s, _ = bq.shape
    c = bkc.shape[1]
    nq = s // tq
    steps = b * B_HEADS * nq
    side2d = [w.reshape(-1, w.shape[-1]) for w in side]
    slab = lambda i, h, j: ((i * B_HEADS + h) * nq + j, 0)
    side_specs = [pl.BlockSpec((w.shape[0] // steps, w.shape[1]), slab) for w in side2d]
    res = pl.pallas_call(
        functools.partial(_mla_kernel, tk=tk, n_side=len(side)),
        grid=(b, B_HEADS, nq),
        in_specs=[
            pl.BlockSpec((1, tq, QK_PAD), lambda i, h, j: (i, j, h)),
            pl.BlockSpec((1, s, QK_PAD), lambda i, h, j: (i, 0, h)),
            pl.BlockSpec((1, s, V_PAD), lambda i, h, j: (i, 0, h)),
            pl.BlockSpec((1, c, QK_PAD), lambda i, h, j: (i, 0, h)),
            pl.BlockSpec((1, c, V_PAD), lambda i, h, j: (i, 0, h)),
        ] + side_specs,
        out_specs=[pl.BlockSpec((1, tq, V_DIM), lambda i, h, j: (i, j, h))] + side_specs,
        out_shape=[jax.ShapeDtypeStruct((b, s, B_HEADS * V_DIM), BF16)]
        + [jax.ShapeDtypeStruct(w.shape, BF16) for w in side2d],
        scratch_shapes=[pltpu.VMEM((tq, tk), F32), pltpu.VMEM((tq, tk), F32), pltpu.VMEM((tq, V_PAD), F32)],
        compiler_params=_cparams(("arbitrary", "arbitrary", "arbitrary")),
        name="mla_attn",
    )(bq, bk, bv, bkc, bvc, *side2d)
    return res[0], [o.reshape(w.shape) for o, w in zip(res[1:], side)]


def _route(sel, aff):
    scores = []
    for g in range(N_GROUPS):
        r = sel[g * EXP_PER_GROUP:(g + 1) * EXP_PER_GROUP]
        best = None
        for a in range(EXP_PER_GROUP):
            for b in range(a + 1, EXP_PER_GROUP):
                pair = r[a] + r[b]
                best = pair if best is None else jnp.maximum(best, pair)
        scores.append(best)
    top, grp = scores[0], jnp.zeros_like(scores[0], dtype=I32)
    for g in range(1, N_GROUPS):
        take = scores[g] > top
        grp = jnp.where(take, g, grp)
        top = jnp.where(take, scores[g], top)
    masked = [jnp.where(grp == e // EXP_PER_GROUP, sel[e], -jnp.inf) for e in range(N_EXPERTS)]

    def argmax_first(vals, skip=None):
        bv = jnp.full_like(vals[0], -jnp.inf)
        bi = jnp.full_like(grp, -1)
        for e in range(N_EXPERTS):
            take = vals[e] > bv
            if skip is not None:
                take = take & (skip != e)
            bi = jnp.where(take, e, bi)
            bv = jnp.where(take, vals[e], bv)
        return bi

    i0 = argmax_first(masked)
    i1 = argmax_first(masked, skip=i0)
    a0 = jnp.zeros_like(aff[0])
    a1 = jnp.zeros_like(aff[0])
    for e in range(N_EXPERTS):
        a0 = jnp.where(i0 == e, aff[e], a0)
        a1 = jnp.where(i1 == e, aff[e], a1)
    tot = a0 + a1
    return i0, i1, a0 / tot, a1 / tot


def _out_kernel(*refs, n_lhs):
    lhs = refs[:n_lhs]
    ws = refs[n_lhs:2 * n_lhs]
    bias_ref, x_ref, mod_ref, gn_ref, wr_ref, br_ref, x1_ref, hp_ref, idx_ref, wts_ref = refs[2 * n_lhs:]
    y = bias_ref[...]
    for a, w in zip(lhs, ws):
        y = y + jnp.dot(a[0], w[...], preferred_element_type=F32)
    x1 = x_ref[0] + mod_ref[0, 2:3, :] * y
    x1_ref[0] = x1
    h2 = (_rms(x1) * gn_ref[...]) * (1.0 + mod_ref[0, 4:5, :]) + mod_ref[0, 3:4, :]
    hp_ref[0] = h2
    w = wr_ref[...]
    w_head = w.astype(BF16)
    w_tail = (w - w_head.astype(F32)).astype(BF16)
    h_head = h2.astype(BF16)
    h_tail = (h2 - h_head.astype(F32)).astype(BF16)
    t = jnp.dot(h_head, jnp.concatenate([w_head, w_tail], axis=1), preferred_element_type=F32)
    logits = (t[:, :LANES] + t[:, LANES:]) + jnp.dot(h_tail, w_head, preferred_element_type=F32)
    lt = logits.T[:N_EXPERTS]
    aff_t = jax.nn.sigmoid(lt)
    sel_t = aff_t + br_ref[...]
    sel = [sel_t[e:e + 1] for e in range(N_EXPERTS)]
    aff = [aff_t[e:e + 1] for e in range(N_EXPERTS)]
    i0, i1, w0, w1 = _route(sel, aff)
    idx_ref[0] = jnp.concatenate([i0, i1], axis=0)
    w_rows = jnp.concatenate([w0, w1, jnp.zeros((LANES - TOP_K, w0.shape[1]), F32)], axis=0)
    wts_ref[0] = w_rows.T[:, :TOP_K]


def _out_proj(lhs, ws, bias, x, mod, gn, wr, br, tm):
    b, s, d = x.shape
    n_lhs = len(lhs)
    row = pl.BlockSpec((1, tm, d), lambda i, j: (i, j, 0))
    in_specs = ([pl.BlockSpec((1, tm, a.shape[-1]), lambda i, j: (i, j, 0)) for a in lhs]
                + [_resident(w.shape) for w in ws]
                + [_resident(bias.shape), row, pl.BlockSpec((1, 6, d), lambda i, j: (i, 0, 0)),
                   _resident(gn.shape), _resident(wr.shape), _resident(br.shape)])
    return pl.pallas_call(
        functools.partial(_out_kernel, n_lhs=n_lhs),
        grid=(b, s // tm),
        in_specs=in_specs,
        out_specs=[row, row,
                   pl.BlockSpec((1, TOP_K, tm), lambda i, j: (i, 0, j)),
                   pl.BlockSpec((1, tm, TOP_K), lambda i, j: (i, j, 0))],
        out_shape=[jax.ShapeDtypeStruct((b, s, d), F32), jax.ShapeDtypeStruct((b, s, d), F32),
                   jax.ShapeDtypeStruct((b, TOP_K, s), I32), jax.ShapeDtypeStruct((b, s, TOP_K), F32)],
        compiler_params=_cparams(("arbitrary", "arbitrary")),
        name="out_proj",
    )(*lhs, *ws, bias, x, mod, gn, wr, br)


def _dispatch(idx, blk):
    b, _, s = idx.shape
    n_asg = b * TOP_K * s
    e = idx.reshape(n_asg)
    onehot = (e[None, :] == jnp.arange(N_EXPERTS, dtype=I32)[:, None]).astype(I32)
    csum = jnp.cumsum(onehot, axis=1)
    counts = csum[:, -1]
    rank = jnp.sum((csum - onehot) * onehot, axis=0)
    padded = (counts + blk - 1) // blk * blk
    pad_end = jnp.cumsum(padded)
    pad_start = pad_end - padded
    start = jnp.cumsum(counts) - counts
    dest = pad_start[e] + rank
    src_asg = jnp.sort(e * n_asg + jnp.arange(n_asg, dtype=I32)) % n_asg
    src_tok = src_asg // (TOP_K * s) * s + src_asg % s
    src_tok = jnp.concatenate([src_tok, jnp.zeros((blk,), src_tok.dtype)])
    n_blocks = -(-n_asg // blk) + N_EXPERTS
    n_used = pad_end[-1] // blk
    blk_ids = jnp.minimum(jnp.arange(n_blocks, dtype=I32), n_used - 1)
    blk_e = jnp.sum((blk_ids[:, None] * blk >= pad_end[None, :]).astype(I32), axis=1)
    blk_e = jnp.minimum(blk_e, N_EXPERTS - 1)
    blk_lo = start[blk_e] + blk_ids * blk - pad_start[blk_e]
    blk_rows = jnp.clip(start[blk_e] + counts[blk_e] - blk_lo, 0, blk)
    as_i32 = lambda a: a.astype(I32)
    return (as_i32(dest), as_i32(src_tok), as_i32(blk_e), as_i32(blk_lo), as_i32(blk_rows),
            as_i32(n_used.reshape(1)))


def _moe_kernel(blk_e_ref, lo_ref, rows_ref, n_used_ref, tok_ref, h_ref, wg_ref, wu_ref, wd_ref, o_ref, xbuf, sem,
                *, blk):
    i = pl.program_id(0)
    n_used = n_used_ref[0]
    slot = i % 2
    half = blk // 2

    def gather(block, slot_):
        lo = lo_ref[block]

        def row_copy(r):
            t = tok_ref[lo + r]
            pltpu.make_async_copy(h_ref.at[pl.ds(t, 1)], xbuf.at[slot_, pl.ds(r, 1)], sem.at[slot_]).start()

        for r in range(half):
            row_copy(r)

        @pl.when(rows_ref[block] > half)
        def _():
            for r in range(half, blk):
                row_copy(r)

    def ffn(rows):
        pltpu.make_async_copy(xbuf.at[slot, pl.ds(0, rows)], xbuf.at[slot, pl.ds(0, rows)], sem.at[slot]).wait()
        xb = xbuf[slot, :rows, :].astype(BF16)
        g = jnp.dot(xb, wg_ref[0, 0], preferred_element_type=F32)
        u = jnp.dot(xb, wu_ref[0, 0], preferred_element_type=F32)
        a = (g * jax.nn.sigmoid(g) * u).astype(BF16)
        o_ref[:rows, :] = jnp.dot(a, wd_ref[0, 0], preferred_element_type=F32)
        if rows < blk:
            o_ref[rows:, :] = jnp.zeros((blk - rows, o_ref.shape[1]), F32)

    @pl.when(i == 0)
    def _():
        gather(0, 0)

    @pl.when(i + 1 < n_used)
    def _():
        gather(i + 1, 1 - slot)

    used = i < n_used
    short = rows_ref[i] <= half

    @pl.when(used & jnp.logical_not(short))
    def _():
        ffn(blk)

    @pl.when(used & short)
    def _():
        ffn(half)

    @pl.when(jnp.logical_not(used))
    def _():
        o_ref[...] = jnp.zeros_like(o_ref)


def _moe(h2, src_tok, blk_e, blk_lo, blk_rows, n_used, wg, wu, wd, layer, blk):
    n_blocks = blk_e.shape[0]
    _, _, d, de = wg.shape
    grid_spec = pltpu.PrefetchScalarGridSpec(
        num_scalar_prefetch=5,
        grid=(n_blocks,),
        in_specs=[
            pl.BlockSpec(memory_space=pl.ANY),
            pl.BlockSpec((1, 1, d, de), lambda i, be, *_: (layer, be[i], 0, 0)),
            pl.BlockSpec((1, 1, d, de), lambda i, be, *_: (layer, be[i], 0, 0)),
            pl.BlockSpec((1, 1, de, d), lambda i, be, *_: (layer, be[i], 0, 0)),
        ],
        out_specs=pl.BlockSpec((blk, d), lambda i, *_: (i, 0)),
        scratch_shapes=[pltpu.VMEM((2, blk, d), F32), pltpu.SemaphoreType.DMA((2,))],
    )
    return pl.pallas_call(
        functools.partial(_moe_kernel, blk=blk),
        grid_spec=grid_spec,
        out_shape=jax.ShapeDtypeStruct((n_blocks * blk, d), F32),
        compiler_params=_cparams(("arbitrary",)),
        name="moe_ffn",
    )(blk_e, blk_lo, blk_rows, n_used, src_tok, h2, wg, wu, wd)


def _combine_kernel(dest_ref, yp_ref, x_ref, wt_ref, mod_ref, *rest, tm, nt, with_norm):
    if with_norm:
        gn_ref, modn_ref, o_ref, hn_ref, buf, sem = rest
    else:
        o_ref, buf, sem = rest
    i = pl.program_id(0)
    j = pl.program_id(1)
    step = i * nt + j
    slot = step % 2

    def gather(step_, slot_):
        bases = [((step_ // nt) * TOP_K + k) * (nt * tm) + (step_ % nt) * tm for k in range(TOP_K)]
        for r in range(tm):
            for k in range(TOP_K):
                pltpu.make_async_copy(yp_ref.at[pl.ds(dest_ref[bases[k] + r], 1)],
                                      buf.at[slot_, k, pl.ds(r, 1)], sem.at[slot_]).start()

    @pl.when(step == 0)
    def _():
        gather(0, 0)

    @pl.when(step + 1 < pl.num_programs(0) * nt)
    def _():
        gather(step + 1, 1 - slot)

    pltpu.make_async_copy(buf.at[slot], buf.at[slot], sem.at[slot]).wait()
    w = wt_ref[0]
    y = buf[slot, 0] * w[:, 0:1] + buf[slot, 1] * w[:, 1:2]
    out = x_ref[0] + mod_ref[0, 5:6, :] * y
    o_ref[0] = out
    if with_norm:
        hn_ref[0] = ((_rms(out) * gn_ref[...]) * (1.0 + modn_ref[0, 1:2, :]) + modn_ref[0, 0:1, :]).astype(BF16)


def _combine(dest, yp, x1, wts, mod, tm, norm=None):
    b, s, d = x1.shape
    nt = s // tm
    with_norm = norm is not None
    row = pl.BlockSpec((1, tm, d), lambda i, j, ds: (i, j, 0))
    modspec = pl.BlockSpec((1, 6, d), lambda i, j, ds: (i, 0, 0))
    in_specs = [pl.BlockSpec(memory_space=pl.ANY), row,
                pl.BlockSpec((1, tm, TOP_K), lambda i, j, ds: (i, j, 0)), modspec]
    args = [yp, x1, wts, mod]
    out_shape = [jax.ShapeDtypeStruct((b, s, d), F32)]
    out_specs = [row]
    if with_norm:
        gn, modn = norm
        in_specs += [pl.BlockSpec(gn.shape, lambda i, j, ds: (0, 0)), modspec]
        args += [gn, modn]
        out_shape.append(jax.ShapeDtypeStruct((b, s, d), BF16))
        out_specs.append(row)
    grid_spec = pltpu.PrefetchScalarGridSpec(
        num_scalar_prefetch=1, grid=(b, nt), in_specs=in_specs, out_specs=out_specs,
        scratch_shapes=[pltpu.VMEM((2, TOP_K, tm, d), F32), pltpu.SemaphoreType.DMA((2,))])
    return pl.pallas_call(
        functools.partial(_combine_kernel, tm=tm, nt=nt, with_norm=with_norm),
        grid_spec=grid_spec,
        out_shape=out_shape,
        compiler_params=_cparams(("arbitrary", "arbitrary")),
        name="combine",
    )(dest, *args)


def _dft_constants(seq):
    n1, n2 = DFT_N1, DFT_N2
    assert seq == n1 * n2 and n2 == n1 * n1
    c = np.arange(F_GROUP_DIM)
    ang = 2 * np.pi * np.outer(c, c) / F_GROUP_DIM
    fc = np.concatenate([np.cos(ang), -np.sin(ang)], axis=1)
    a, k2, m = np.meshgrid(np.arange(n1), np.arange(n2), np.arange(n2), indexing="ij")
    ang_a = -2 * np.pi * (k2 * (a + n1 * m) % seq) / seq
    tre, tim = np.cos(ang_a), np.sin(ang_a)
    ma = np.concatenate([np.concatenate([tre, -tim], axis=2), np.concatenate([tim, tre], axis=2)], axis=1)
    ang_b = -2 * np.pi * np.outer(np.arange(n1), np.arange(n1)) / n1
    eye = np.eye(n1)
    mb = np.concatenate([np.kron(np.cos(ang_b), eye), -np.kron(np.sin(ang_b), eye)], axis=1)
    mb = mb / math.sqrt(seq * F_GROUP_DIM)
    return tuple(jnp.asarray(t, F32).astype(BF16) for t in (fc, ma, mb))


def _fourier_kernel(h_ref, fc_ref, ma_ref, mb_ref, o_ref, z_ref, yre, yim):
    n1, n2, gd = DFT_N1, DFT_N2, F_GROUP_DIM
    rows = FOURIER_ROWS
    n_tiles = 2 * gd // LANES
    for r in range(0, h_ref.shape[1], rows):
        z = jnp.dot(h_ref[0, r:r + rows, :], fc_ref[...], preferred_element_type=F32)
        for t in range(n_tiles):
            z_ref[t, r:r + rows, :] = z[:, t * LANES:(t + 1) * LANES]
    for a in range(n1):
        zs = [z_ref[t, pl.ds(a, n2, stride=n1), :].astype(BF16) for t in range(n_tiles)]
        rhs = jnp.concatenate([jnp.concatenate(zs[:n_tiles // 2], axis=1),
                               jnp.concatenate(zs[n_tiles // 2:], axis=1)], axis=0)
        y = jnp.dot(ma_ref[a], rhs, preferred_element_type=F32).astype(BF16)
        yre[a] = y[:n2]
        yim[a] = y[n2:]
    for hi in range(n1):
        rhs = jnp.concatenate([yre[a, hi * n1:(hi + 1) * n1, :] for a in range(n1)]
                              + [yim[a, hi * n1:(hi + 1) * n1, :] for a in range(n1)], axis=0)
        out = jnp.dot(mb_ref[...], rhs, preferred_element_type=F32).astype(BF16)
        for k1 in range(n1):
            o_ref[0, k1 * n2 + hi * n1:k1 * n2 + (hi + 1) * n1, :] = out[k1 * n1:(k1 + 1) * n1]


def _fourier(hn, consts):
    b, s, d = hn.shape
    fc, ma, mb = consts
    gd = F_GROUP_DIM
    return pl.pallas_call(
        _fourier_kernel,
        grid=(b, F_GROUPS),
        in_specs=[pl.BlockSpec((1, s, gd), lambda i, g: (i, 0, g)),
                  _resident(fc.shape), _resident(ma.shape), _resident(mb.shape)],
        out_specs=pl.BlockSpec((1, s, gd), lambda i, g: (i, 0, g)),
        out_shape=jax.ShapeDtypeStruct((b, s, d), BF16),
        scratch_shapes=[pltpu.VMEM((2 * gd // LANES, s, LANES), F32),
                        pltpu.VMEM((DFT_N1, DFT_N2, gd), BF16), pltpu.VMEM((DFT_N1, DFT_N2, gd), BF16)],
        compiler_params=_cparams(("arbitrary", "arbitrary")),
        name="fourier",
    )(hn, fc, ma, mb)


def _rope_tables(seq):
    pos = np.arange(seq)
    row, col = (pos // GRID_W).astype(np.float64), (pos % GRID_W).astype(np.float64)

    def tables(width):
        half = width // 2
        quarter = half // 2
        freqs = ROPE_BASE ** (-np.arange(0, half, 2, dtype=np.float64) / half)
        lane = np.arange(LANES)
        ang = np.where((lane < half)[None, :], row[:, None], col[:, None]) * freqs[lane % quarter][None, :]
        live = (lane < width)[None, :]
        first = ((lane % half) < quarter)[None, :]
        cos = np.where(live, np.cos(ang), 1.0)
        sneg = np.where(live & first, -np.sin(ang), 0.0)
        spos = np.where(live & ~first, np.sin(ang), 0.0)
        return [jnp.asarray(t, F32) for t in (cos, sneg, spos)]

    return tables(HEAD_DIM) + tables(ROPE_DIM)


def _identity_tables(n):
    one, zero = jnp.ones((n, LANES), F32), jnp.zeros((n, LANES), F32)
    return [one, zero, zero, one, zero, zero]


def _pad_heads(w, lead):
    w = w.reshape(lead, B_HEADS, QK_DIM)
    return jnp.pad(w, ((0, 0), (0, 0), (0, QK_PAD - QK_DIM))).reshape(lead, B_HEADS * QK_PAD)


def kernel(x, c, ctx, c_ctx, w_ada, b_ada, g_norm, w_in, g_aqn, g_akn, g_bq_lat, w_bq_up, g_bkv_lat, w_bkv_up,
           g_bqn, g_bkn, sink, w_o_ab, w_fo, b_fo, w_router, b_router, w_gate, w_up, w_down):
    b, s, d = x.shape
    n_ctx = ctx.shape[1]

    crows = jnp.concatenate([c, c_ctx[None, :], jnp.zeros((8 - b - 1, d), F32)], axis=0)
    mods, (win, wo, wfo) = _ada(crows, w_ada, b_ada, (w_in, w_o_ab, w_fo), (IN_PAD, d, d))
    mods = mods.reshape(DEPTH, 8, 6, d)
    mod_lat = [mods[l, :b] for l in range(DEPTH)]
    mod_ctx = jnp.broadcast_to(mods[0, b][None], (b, 6, d))

    wr = jnp.pad(w_router, ((0, 0), (0, LANES - N_EXPERTS)))
    br = b_router.reshape(N_EXPERTS, 1)

    wbq = _pad_heads(w_bq_up[0], Q_LORA).astype(BF16)
    wkv = w_bkv_up[0].reshape(KV_LORA, B_HEADS, NOPE_DIM + V_DIM)
    wbkv = jnp.concatenate([wkv[:, :, :NOPE_DIM].reshape(KV_LORA, -1), wkv[:, :, NOPE_DIM:].reshape(KV_LORA, -1)],
                           axis=1).astype(BF16)
    gains = (g_aqn[0][None], g_akn[0][None], g_bq_lat[0][None], g_bkv_lat[0][None],
             jnp.pad(g_bqn[0][None], ((0, 0), (0, QK_PAD - QK_DIM))),
             jnp.pad(g_bkn[0][None], ((0, 0), (0, QK_PAD - QK_DIM))))
    gn0 = g_norm[0, 0][None]
    aq, ak, av, bq, bk, bv = _proj(x, mod_lat[0], gn0, win, wbq, wbkv, *gains, _rope_tables(s), PROJ_TM)
    _, akc, avc, _, bkc, bvc = _proj(ctx, mod_ctx, gn0, win, wbq, wbkv, *gains, _identity_tables(n_ctx), n_ctx)

    ya = _win_attn(sink[0], aq, ak, av, akc, avc)
    yb, (wg, wu, wd) = _mla_attn(bq, bk, bv, bkc, bvc, (w_gate, w_up, w_down), MLA_TQ, MLA_TK)

    n_a = A_HEADS * HEAD_DIM
    x1, h2, idx, wts = _out_proj([ya, yb], [wo[:n_a], wo[n_a:]], jnp.zeros((1, d), F32), x, mod_lat[0],
                                 g_norm[0, 1][None], wr, br, OUT_TM)
    dest, *plan = _dispatch(idx, MOE_BLK)
    yp = _moe(h2.reshape(b * s, d), *plan, wg, wu, wd, 0, MOE_BLK)
    x2, hn = _combine(dest, yp, x1, wts, mod_lat[0], COMBINE_TM, norm=(g_norm[1, 0][None], mod_lat[1]))

    f = _fourier(hn, _dft_constants(s))
    x3, h2, idx, wts = _out_proj([f], [wfo], b_fo[0][None], x2, mod_lat[1],
                                 g_norm[1, 1][None], wr, br, OUT_TM)
    dest, *plan = _dispatch(idx, MOE_BLK)
    yp = _moe(h2.reshape(b * s, d), *plan, wg, wu, wd, 1, MOE_BLK)
    (x4,) = _combine(dest, yp, x3, wts, mod_lat[1], COMBINE_TM)
    return x4
```

```python
import functools
import math

import numpy as np
import jax
import jax.numpy as jnp
from jax import lax
from jax.experimental import pallas as pl
from jax.experimental.pallas import tpu as pltpu

F32 = jnp.float32
BF16 = jnp.bfloat16
I32 = jnp.int32

D_MODEL = 2048
DEPTH = 2
GRID_W = 64
HEAD_DIM = 128
A_HEADS = 8
A_KV_HEADS = 2
A_GROUP = A_HEADS // A_KV_HEADS
WINDOW = 128
WBLK = 128
B_HEADS = 8
Q_LORA = 512
KV_LORA = 256
NOPE_DIM = 128
ROPE_DIM = 64
V_DIM = 128
QK_DIM = NOPE_DIM + ROPE_DIM
QK_PAD = 256
V_PAD = 256
IN_SPLITS = (A_HEADS * HEAD_DIM, A_KV_HEADS * HEAD_DIM, A_KV_HEADS * HEAD_DIM, Q_LORA, KV_LORA, ROPE_DIM)
IN_WIDTH = sum(IN_SPLITS)
IN_PAD = 2432
F_GROUPS = 8
F_GROUP_DIM = D_MODEL // F_GROUPS
N_EXPERTS = 16
N_GROUPS = 4
EXP_PER_GROUP = N_EXPERTS // N_GROUPS
TOP_K = 2
D_EXPERT = 1024
ROPE_BASE = 10000.0
EPS = 1e-6
LOG2E = math.log2(math.e)
LANES = 128

MOE_BLK = 512
WIN_Q = 8
PROJ_TM = 256
OUT_TM = 512
COMBINE_TM = 256
MLA_TQ = 1024
MLA_TK = 2048
ADA_TN = 1024
FOURIER_ROWS = 512
DFT_N1 = 16
DFT_N2 = 256
VMEM_LIMIT = 56 * 1024 * 1024


def _cparams(sem, **kw):
    return pltpu.CompilerParams(dimension_semantics=sem, vmem_limit_bytes=VMEM_LIMIT, **kw)


def _resident(shape):
    nd = len(shape)
    return pl.BlockSpec(shape, lambda *_: (0,) * nd, pipeline_mode=pl.Buffered(1))


def _rms(t, width=None):
    n = t.shape[-1] if width is None else width
    ss = jnp.sum(t * t, axis=-1, keepdims=True)
    return t * lax.rsqrt(ss * (1.0 / n) + EPS)


def _rope(t, cos, sneg, spos, dist):
    n = t.shape[-1]
    return t * cos + pltpu.roll(t, n - dist, 1) * sneg + pltpu.roll(t, dist, 1) * spos


ADA_SIDE_STEPS = 16


def _ada_kernel(c_ref, w_ref, b_ref, *rest, n_side, nj):
    side_in, o_ref, side_out = rest[:n_side], rest[n_side], rest[n_side + 1:]
    c = c_ref[...]
    s = c * jax.nn.sigmoid(c)
    w = w_ref[0]
    w_head = w.astype(BF16)
    w_tail = (w - w_head.astype(F32)).astype(BF16)
    s_head = s.astype(BF16)
    s_tail = (s - s_head.astype(F32)).astype(BF16)
    r = jnp.dot(jnp.concatenate([s_head, s_tail], axis=0), w_head, preferred_element_type=F32)
    rows = s.shape[0]
    o_ref[0] = (r[:rows] + r[rows:]) + jnp.dot(s_head, w_tail, preferred_element_type=F32) + b_ref[0]

    @pl.when(pl.program_id(0) * nj + pl.program_id(1) < ADA_SIDE_STEPS)
    def _():
        for w_in, w_out in zip(side_in, side_out):
            width = w_in.shape[1]
            w_out[:, :width] = w_in[...].astype(BF16)
            if w_out.shape[1] > width:
                w_out[:, width:] = jnp.zeros((w_out.shape[0], w_out.shape[1] - width), BF16)


def _ada(crows, w_ada, b_ada, side, side_widths):
    depth, d, n = w_ada.shape
    tn = ADA_TN
    nj = n // tn
    assert depth * nj >= ADA_SIDE_STEPS
    slab = lambda l, j: (jnp.minimum(l * nj + j, ADA_SIDE_STEPS - 1), 0)
    slab3 = lambda l, j: (0,) + slab(l, j)
    assert all(w.ndim == 3 and w.shape[0] == 1 for w in side)
    side_in = [pl.BlockSpec((None, w.shape[1] // ADA_SIDE_STEPS, w.shape[2]), slab3) for w in side]
    side_out = [pl.BlockSpec((w.shape[1] // ADA_SIDE_STEPS, wd), slab) for w, wd in zip(side, side_widths)]
    res = pl.pallas_call(
        functools.partial(_ada_kernel, n_side=len(side), nj=nj),
        grid=(depth, nj),
        in_specs=[
            pl.BlockSpec((8, d), lambda l, j: (0, 0)),
            pl.BlockSpec((1, d, tn), lambda l, j: (l, 0, j)),
            pl.BlockSpec((1, 1, tn), lambda l, j: (l, 0, j)),
        ] + side_in,
        out_specs=[pl.BlockSpec((1, 8, tn), lambda l, j: (l, 0, j))] + side_out,
        out_shape=[jax.ShapeDtypeStruct((depth, 8, n), F32)]
        + [jax.ShapeDtypeStruct((w.shape[1], wd), BF16) for w, wd in zip(side, side_widths)],
        compiler_params=_cparams(("arbitrary", "arbitrary")),
        name="ada",
    )(crows, w_ada, b_ada.reshape(depth, 1, n), *side)
    return res[0], res[1:]


def _proj_kernel(x_ref, mod_ref, gn_ref, win_ref, wbq_ref, wbkv_ref, gaq_ref, gak_ref, gbql_ref, gbkvl_ref,
                 gbq_ref, gbk_ref, ca_ref, sna_ref, spa_ref, cb_ref, snb_ref, spb_ref,
                 aq_ref, ak_ref, av_ref, bq_ref, bk_ref, bv_ref):
    x = x_ref[0]
    shift = mod_ref[0, 0:1, :]
    scale = mod_ref[0, 1:2, :]
    hb = ((_rms(x) * gn_ref[...]) * (1.0 + scale) + shift).astype(BF16)

    def cols(w_ref, lhs, lo, width):
        return jnp.dot(lhs, w_ref[:, lo:lo + width], preferred_element_type=F32)

    ca, sna, spa = ca_ref[...], sna_ref[...], spa_ref[...]
    cb, snb, spb = cb_ref[...], snb_ref[...], spb_ref[...]
    a_scale = HEAD_DIM ** -0.5 * LOG2E
    b_scale = QK_DIM ** -0.5 * LOG2E
    pair = 2 * HEAD_DIM

    for hp in range(A_HEADS // 2):
        pp = cols(win_ref, hb, hp * pair, pair)
        for j in range(2):
            hd = 2 * hp + j
            t = _rms(pp[:, j * HEAD_DIM:(j + 1) * HEAD_DIM]) * gaq_ref[...]
            aq_ref[0, :, hd * HEAD_DIM:(hd + 1) * HEAD_DIM] = (_rope(t, ca, sna, spa, HEAD_DIM // 4) * a_scale).astype(BF16)
    off = A_HEADS * HEAD_DIM
    pp = cols(win_ref, hb, off, A_KV_HEADS * HEAD_DIM)
    for kh in range(A_KV_HEADS):
        t = _rms(pp[:, kh * HEAD_DIM:(kh + 1) * HEAD_DIM]) * gak_ref[...]
        ak_ref[0, :, kh * HEAD_DIM:(kh + 1) * HEAD_DIM] = _rope(t, ca, sna, spa, HEAD_DIM // 4).astype(BF16)
    off += A_KV_HEADS * HEAD_DIM
    ones_col = (lax.broadcasted_iota(I32, (hb.shape[0], V_PAD - V_DIM), 1) == 0).astype(BF16)
    pp = cols(win_ref, hb, off, A_KV_HEADS * HEAD_DIM)
    for kh in range(A_KV_HEADS):
        av_ref[0, :, kh * V_PAD:kh * V_PAD + HEAD_DIM] = pp[:, kh * HEAD_DIM:(kh + 1) * HEAD_DIM].astype(BF16)
        av_ref[0, :, kh * V_PAD + HEAD_DIM:(kh + 1) * V_PAD] = ones_col
    off += A_KV_HEADS * HEAD_DIM

    ql = (_rms(cols(win_ref, hb, off, Q_LORA)) * gbql_ref[...]).astype(BF16)
    off += Q_LORA
    for hd in range(B_HEADS):
        t = _rms(cols(wbq_ref, ql, hd * QK_PAD, QK_PAD), QK_DIM) * gbq_ref[...]
        bq_ref[0, :, hd * QK_PAD:hd * QK_PAD + NOPE_DIM] = (t[:, :NOPE_DIM] * b_scale).astype(BF16)
        bq_ref[0, :, hd * QK_PAD + NOPE_DIM:(hd + 1) * QK_PAD] = (
            _rope(t[:, NOPE_DIM:], cb, snb, spb, ROPE_DIM // 4) * b_scale).astype(BF16)

    kvl = (_rms(cols(win_ref, hb, off, KV_LORA)) * gbkvl_ref[...]).astype(BF16)
    off += KV_LORA
    kr = cols(win_ref, hb, off, LANES)
    kr_ss = jnp.sum(kr * kr, axis=-1, keepdims=True)
    kr_rot = _rope(kr * gbk_ref[:, NOPE_DIM:], cb, snb, spb, ROPE_DIM // 4)
    for hp in range(B_HEADS // 2):
        kn2 = cols(wbkv_ref, kvl, hp * pair, pair)
        v2 = cols(wbkv_ref, kvl, B_HEADS * NOPE_DIM + hp * pair, pair)
        for j in range(2):
            hd = 2 * hp + j
            kn = kn2[:, j * NOPE_DIM:(j + 1) * NOPE_DIM]
            ss = jnp.sum(kn * kn, axis=-1, keepdims=True) + kr_ss
            r = lax.rsqrt(ss * (1.0 / QK_DIM) + EPS)
            bk_ref[0, :, hd * QK_PAD:hd * QK_PAD + NOPE_DIM] = (kn * r * gbk_ref[:, :NOPE_DIM]).astype(BF16)
            bk_ref[0, :, hd * QK_PAD + NOPE_DIM:(hd + 1) * QK_PAD] = (kr_rot * r).astype(BF16)
            bv_ref[0, :, hd * V_PAD:hd * V_PAD + V_DIM] = v2[:, j * V_DIM:(j + 1) * V_DIM].astype(BF16)
            bv_ref[0, :, hd * V_PAD + V_DIM:(hd + 1) * V_PAD] = ones_col


def _proj(x, mod, gn, win, wbq, wbkv, gaq, gak, gbql, gbkvl, gbq, gbk, tabs, tm):
    b, s, d = x.shape
    row = lambda w: pl.BlockSpec((1, tm, w), lambda i, j: (i, j, 0))
    tab = pl.BlockSpec((tm, LANES), lambda i, j: (j, 0))
    widths = (A_HEADS * HEAD_DIM, A_KV_HEADS * HEAD_DIM, A_KV_HEADS * V_PAD,
              B_HEADS * QK_PAD, B_HEADS * QK_PAD, B_HEADS * V_PAD)
    return pl.pallas_call(
        _proj_kernel,
        grid=(b, s // tm),
        in_specs=[row(d), pl.BlockSpec((1, 6, d), lambda i, j: (i, 0, 0)), _resident(gn.shape),
                  _resident(win.shape), _resident(wbq.shape), _resident(wbkv.shape),
                  _resident(gaq.shape), _resident(gak.shape), _resident(gbql.shape), _resident(gbkvl.shape),
                  _resident(gbq.shape), _resident(gbk.shape)] + [tab] * 6,
        out_specs=[row(w) for w in widths],
        out_shape=[jax.ShapeDtypeStruct((b, s, w), BF16) for w in widths],
        compiler_params=_cparams(("arbitrary", "arbitrary")),
        name="proj",
    )(x, mod, gn, win, wbq, wbkv, gaq, gak, gbql, gbkvl, gbq, gbk, *tabs)


def _win_kernel(sink_ref, q_ref, kp_ref, kc_ref, kn_ref, vp_ref, vc_ref, vn_ref, kx_ref, vx_ref, o_ref, *, seq):
    j = pl.program_id(1)
    rows = A_GROUP * WBLK
    band = 3 * WBLK
    keys = band + kx_ref.shape[1]
    r_iota = lax.broadcasted_iota(I32, (rows, keys), 0)
    c_iota = lax.broadcasted_iota(I32, (rows, keys), 1)
    head_of_row = lax.broadcasted_iota(I32, (rows, 1), 0) // WBLK
    dn = (((1,), (1,)), ((), ()))

    def key_block(refs, t, cols):
        p_ref, c_ref, n_ref = refs
        if t == 0:
            return p_ref[0, :, cols]
        if t == WIN_Q + 1:
            return n_ref[0, :, cols]
        return c_ref[0, (t - 1) * WBLK:t * WBLK, cols]

    for sub in range(WIN_Q):
        n = j * WIN_Q + sub
        qpos = n * WBLK + (r_iota & (WBLK - 1))
        kpos = (n - 1) * WBLK + c_iota
        valid = ((jnp.abs(qpos - kpos) <= WINDOW) & (kpos >= 0) & (kpos < seq)) | (c_iota >= band)
        for kh in range(A_KV_HEADS):
            cs = slice(kh * HEAD_DIM, (kh + 1) * HEAD_DIM)
            vs = slice(kh * V_PAD, (kh + 1) * V_PAD)
            q = jnp.concatenate(
                [q_ref[0, sub * WBLK:(sub + 1) * WBLK, (kh * A_GROUP + g) * HEAD_DIM:(kh * A_GROUP + g + 1) * HEAD_DIM]
                 for g in range(A_GROUP)], axis=0)
            kb = jnp.concatenate([key_block((kp_ref, kc_ref, kn_ref), sub + t, cs) for t in range(3)]
                                 + [kx_ref[0, :, cs]], axis=0)
            vb = jnp.concatenate([key_block((vp_ref, vc_ref, vn_ref), sub + t, vs) for t in range(3)]
                                 + [vx_ref[0, :, vs]], axis=0)
            s = jnp.where(valid, lax.dot_general(q, kb, dn, preferred_element_type=F32), -jnp.inf)
            sink = jnp.zeros((rows, 1), F32)
            for g in range(A_GROUP):
                sink = jnp.where(head_of_row == g, sink_ref[kh * A_GROUP + g] * LOG2E, sink)
            m = jnp.maximum(jnp.max(s, axis=-1, keepdims=True), sink)
            acc = jnp.dot(jnp.exp2(s - m).astype(BF16), vb, preferred_element_type=F32)
            o = acc[:, :HEAD_DIM] / (acc[:, HEAD_DIM:HEAD_DIM + 1] + jnp.exp2(sink - m))
            for g in range(A_GROUP):
                hd = kh * A_GROUP + g
                o_ref[0, sub * WBLK:(sub + 1) * WBLK, hd * HEAD_DIM:(hd + 1) * HEAD_DIM] = (
                    o[g * WBLK:(g + 1) * WBLK].astype(BF16))


def _win_attn(sink, aq, ak, av, akc, avc):
    b, s, _ = aq.shape
    nb = s // WBLK
    c = akc.shape[1]
    tq = WIN_Q * WBLK
    prev = lambda w: pl.BlockSpec((1, WBLK, w), lambda i, j: (i, jnp.maximum(j * WIN_Q - 1, 0), 0))
    cur = lambda w: pl.BlockSpec((1, tq, w), lambda i, j: (i, j, 0))
    nxt = lambda w: pl.BlockSpec((1, WBLK, w), lambda i, j: (i, jnp.minimum((j + 1) * WIN_Q, nb - 1), 0))
    cx = lambda w: pl.BlockSpec((1, c, w), lambda i, j: (i, 0, 0))
    kw, vw = A_KV_HEADS * HEAD_DIM, A_KV_HEADS * V_PAD
    qo = pl.BlockSpec((1, tq, A_HEADS * HEAD_DIM), lambda i, j: (i, j, 0))
    return pl.pallas_call(
        functools.partial(_win_kernel, seq=s),
        grid=(b, nb // WIN_Q),
        in_specs=[pl.BlockSpec(memory_space=pltpu.SMEM), qo, prev(kw), cur(kw), nxt(kw), prev(vw), cur(vw), nxt(vw),
                  cx(kw), cx(vw)],
        out_specs=qo,
        out_shape=jax.ShapeDtypeStruct(aq.shape, BF16),
        compiler_params=_cparams(("arbitrary", "arbitrary")),
        name="win_attn",
    )(sink, aq, ak, ak, ak, av, av, av, akc, avc)


def _mla_kernel(q_ref, k_ref, v_ref, kx_ref, vx_ref, *rest, tk, n_side):
    side_in, o_ref, side_out = rest[:n_side], rest[n_side], rest[n_side + 1:2 * n_side + 1]
    s_a, s_b, acc_ref = rest[2 * n_side + 1:]
    for w_in, w_out in zip(side_in, side_out):
        w_out[...] = w_in[...].astype(BF16)
    q = q_ref[0]
    dn = (((1,), (1,)), ((), ()))
    n_chunks = k_ref.shape[1] // tk
    s_bufs = (s_a, s_b)

    def scores_into(buf, c):
        s = lax.dot_general(q, k_ref[0, c * tk:(c + 1) * tk, :], dn, preferred_element_type=F32)
        buf[...] = s
        return jnp.max(s, axis=-1, keepdims=True)

    s0 = lax.dot_general(q, kx_ref[0], dn, preferred_element_type=F32)
    m = jnp.max(s0, axis=-1, keepdims=True)
    acc_ref[...] = jnp.dot(jnp.exp2(s0 - m).astype(BF16), vx_ref[0], preferred_element_type=F32)
    mx = scores_into(s_bufs[0], 0)
    for c in range(n_chunks):
        if c + 1 < n_chunks:
            mx_next = scores_into(s_bufs[(c + 1) % 2], c + 1)
        m_new = jnp.maximum(m, mx)
        p = jnp.exp2(s_bufs[c % 2][...] - m_new).astype(BF16)
        acc_ref[...] = (jnp.exp2(m - m_new) * acc_ref[...]
                        + jnp.dot(p, v_ref[0, c * tk:(c + 1) * tk, :], preferred_element_type=F32))
        m, mx = m_new, mx_next
    acc = acc_ref[...]
    o_ref[0] = (acc[:, :V_DIM] / acc[:, V_DIM:V_DIM + 1]).astype(BF16)


def _mla_attn(bq, bk, bv, bkc, bvc, side, tq, tk):
    b, s, _ = bq.shape
    c = bkc.shape[1]
    nq = s // tq
    steps = b * B_HEADS * nq
    side2d = [w.reshape(-1, w.shape[-1]) for w in side]
    slab = lambda i, h, j: ((i * B_HEADS + h) * nq + j, 0)
    side_specs = [pl.BlockSpec((w.shape[0] // steps, w.shape[1]), slab) for w in side2d]
    res = pl.pallas_call(
        functools.partial(_mla_kernel, tk=tk, n_side=len(side)),
        grid=(b, B_HEADS, nq),
        in_specs=[
            pl.BlockSpec((1, tq, QK_PAD), lambda i, h, j: (i, j, h)),
            pl.BlockSpec((1, s, QK_PAD), lambda i, h, j: (i, 0, h)),
            pl.BlockSpec((1, s, V_PAD), lambda i, h, j: (i, 0, h)),
            pl.BlockSpec((1, c, QK_PAD), lambda i, h, j: (i, 0, h)),
            pl.BlockSpec((1, c, V_PAD), lambda i, h, j: (i, 0, h)),
        ] + side_specs,
        out_specs=[pl.BlockSpec((1, tq, V_DIM), lambda i, h, j: (i, j, h))] + side_specs,
        out_shape=[jax.ShapeDtypeStruct((b, s, B_HEADS * V_DIM), BF16)]
        + [jax.ShapeDtypeStruct(w.shape, BF16) for w in side2d],
        scratch_shapes=[pltpu.VMEM((tq, tk), F32), pltpu.VMEM((tq, tk), F32), pltpu.VMEM((tq, V_PAD), F32)],
        compiler_params=_cparams(("arbitrary", "arbitrary", "arbitrary")),
        name="mla_attn",
    )(bq, bk, bv, bkc, bvc, *side2d)
    return res[0], [o.reshape(w.shape) for o, w in zip(res[1:], side)]


def _route(sel, aff):
    scores = []
    for g in range(N_GROUPS):
        r = sel[g * EXP_PER_GROUP:(g + 1) * EXP_PER_GROUP]
        best = None
        for a in range(EXP_PER_GROUP):
            for b in range(a + 1, EXP_PER_GROUP):
                pair = r[a] + r[b]
                best = pair if best is None else jnp.maximum(best, pair)
        scores.append(best)
    top, grp = scores[0], jnp.zeros_like(scores[0], dtype=I32)
    for g in range(1, N_GROUPS):
        take = scores[g] > top
        grp = jnp.where(take, g, grp)
        top = jnp.where(take, scores[g], top)
    masked = [jnp.where(grp == e // EXP_PER_GROUP, sel[e], -jnp.inf) for e in range(N_EXPERTS)]

    def argmax_first(vals, skip=None):
        bv = jnp.full_like(vals[0], -jnp.inf)
        bi = jnp.full_like(grp, -1)
        for e in range(N_EXPERTS):
            take = vals[e] > bv
            if skip is not None:
                take = take & (skip != e)
            bi = jnp.where(take, e, bi)
            bv = jnp.where(take, vals[e], bv)
        return bi

    i0 = argmax_first(masked)
    i1 = argmax_first(masked, skip=i0)
    a0 = jnp.zeros_like(aff[0])
    a1 = jnp.zeros_like(aff[0])
    for e in range(N_EXPERTS):
        a0 = jnp.where(i0 == e, aff[e], a0)
        a1 = jnp.where(i1 == e, aff[e], a1)
    tot = a0 + a1
    return i0, i1, a0 / tot, a1 / tot


def _out_kernel(*refs, n_lhs):
    lhs = refs[:n_lhs]
    ws = refs[n_lhs:2 * n_lhs]
    bias_ref, x_ref, mod_ref, gn_ref, wr_ref, br_ref, x1_ref, hp_ref, idx_ref, wts_ref = refs[2 * n_lhs:]
    y = bias_ref[...]
    for a, w in zip(lhs, ws):
        y = y + jnp.dot(a[0], w[...], preferred_element_type=F32)
    x1 = x_ref[0] + mod_ref[0, 2:3, :] * y
    x1_ref[0] = x1
    h2 = (_rms(x1) * gn_ref[...]) * (1.0 + mod_ref[0, 4:5, :]) + mod_ref[0, 3:4, :]
    hp_ref[0] = h2
    w = wr_ref[...]
    w_head = w.astype(BF16)
    w_tail = (w - w_head.astype(F32)).astype(BF16)
    h_head = h2.astype(BF16)
    h_tail = (h2 - h_head.astype(F32)).astype(BF16)
    t = jnp.dot(h_head, jnp.concatenate([w_head, w_tail], axis=1), preferred_element_type=F32)
    logits = (t[:, :LANES] + t[:, LANES:]) + jnp.dot(h_tail, w_head, preferred_element_type=F32)
    lt = logits.T[:N_EXPERTS]
    aff_t = jax.nn.sigmoid(lt)
    sel_t = aff_t + br_ref[...]
    sel = [sel_t[e:e + 1] for e in range(N_EXPERTS)]
    aff = [aff_t[e:e + 1] for e in range(N_EXPERTS)]
    i0, i1, w0, w1 = _route(sel, aff)
    idx_ref[0] = jnp.concatenate([i0, i1], axis=0)
    w_rows = jnp.concatenate([w0, w1, jnp.zeros((LANES - TOP_K, w0.shape[1]), F32)], axis=0)
    wts_ref[0] = w_rows.T[:, :TOP_K]


def _out_proj(lhs, ws, bias, x, mod, gn, wr, br, tm):
    b, s, d = x.shape
    n_lhs = len(lhs)
    row = pl.BlockSpec((1, tm, d), lambda i, j: (i, j, 0))
    in_specs = ([pl.BlockSpec((1, tm, a.shape[-1]), lambda i, j: (i, j, 0)) for a in lhs]
                + [_resident(w.shape) for w in ws]
                + [_resident(bias.shape), row, pl.BlockSpec((1, 6, d), lambda i, j: (i, 0, 0)),
                   _resident(gn.shape), _resident(wr.shape), _resident(br.shape)])
    return pl.pallas_call(
        functools.partial(_out_kernel, n_lhs=n_lhs),
        grid=(b, s // tm),
        in_specs=in_specs,
        out_specs=[row, row,
                   pl.BlockSpec((1, TOP_K, tm), lambda i, j: (i, 0, j)),
                   pl.BlockSpec((1, tm, TOP_K), lambda i, j: (i, j, 0))],
        out_shape=[jax.ShapeDtypeStruct((b, s, d), F32), jax.ShapeDtypeStruct((b, s, d), F32),
                   jax.ShapeDtypeStruct((b, TOP_K, s), I32), jax.ShapeDtypeStruct((b, s, TOP_K), F32)],
        compiler_params=_cparams(("arbitrary", "arbitrary")),
        name="out_proj",
    )(*lhs, *ws, bias, x, mod, gn, wr, br)


def _dispatch(idx, blk):
    b, _, s = idx.shape
    n_asg = b * TOP_K * s
    assert max(n_asg, N_EXPERTS) * n_asg < 2 ** 31
    e = idx.reshape(n_asg)
    counts = jnp.sum((e[None, :] == jnp.arange(N_EXPERTS, dtype=I32)[:, None]).astype(I32), axis=1)
    padded = (counts + blk - 1) // blk * blk
    pad_end = jnp.cumsum(padded)
    pad_start = pad_end - padded
    start = jnp.cumsum(counts) - counts
    iota = jnp.arange(n_asg, dtype=I32)
    src_asg = jnp.sort(e * n_asg + iota) % n_asg
    sorted_pos = jnp.sort(src_asg * n_asg + iota) % n_asg
    dest = sorted_pos + (pad_start - start)[e]
    src_tok = src_asg // (TOP_K * s) * s + src_asg % s
    src_tok = jnp.concatenate([src_tok, jnp.zeros((blk,), src_tok.dtype)])
    n_blocks = -(-n_asg // blk) + N_EXPERTS
    n_used = pad_end[-1] // blk
    blk_ids = jnp.minimum(jnp.arange(n_blocks, dtype=I32), n_used - 1)
    blk_e = jnp.sum((blk_ids[:, None] * blk >= pad_end[None, :]).astype(I32), axis=1)
    blk_e = jnp.minimum(blk_e, N_EXPERTS - 1)
    blk_lo = start[blk_e] + blk_ids * blk - pad_start[blk_e]
    blk_rows = jnp.clip(start[blk_e] + counts[blk_e] - blk_lo, 0, blk)
    as_i32 = lambda a: a.astype(I32)
    return (as_i32(dest), as_i32(src_tok), as_i32(blk_e), as_i32(blk_lo), as_i32(blk_rows),
            as_i32(n_used.reshape(1)))


def _moe_kernel(blk_e_ref, lo_ref, rows_ref, n_used_ref, tok_ref, h_ref, wg_ref, wu_ref, wd_ref, o_ref, xbuf, sem,
                *, blk):
    i = pl.program_id(0)
    n_used = n_used_ref[0]
    slot = i % 2
    half = blk // 2

    def gather(block, slot_):
        lo = lo_ref[block]

        def row_copy(r):
            t = tok_ref[lo + r]
            pltpu.make_async_copy(h_ref.at[pl.ds(t, 1)], xbuf.at[slot_, pl.ds(r, 1)], sem.at[slot_]).start()

        for r in range(half):
            row_copy(r)

        @pl.when(rows_ref[block] > half)
        def _():
            for r in range(half, blk):
                row_copy(r)

    def ffn(rows):
        pltpu.make_async_copy(xbuf.at[slot, pl.ds(0, rows)], xbuf.at[slot, pl.ds(0, rows)], sem.at[slot]).wait()
        xb = xbuf[slot, :rows, :].astype(BF16)
        g = jnp.dot(xb, wg_ref[0, 0], preferred_element_type=F32)
        u = jnp.dot(xb, wu_ref[0, 0], preferred_element_type=F32)
        a = (g * jax.nn.sigmoid(g) * u).astype(BF16)
        o_ref[:rows, :] = jnp.dot(a, wd_ref[0, 0], preferred_element_type=F32)
        if rows < blk:
            o_ref[rows:, :] = jnp.zeros((blk - rows, o_ref.shape[1]), F32)

    @pl.when(i == 0)
    def _():
        gather(0, 0)

    @pl.when(i + 1 < n_used)
    def _():
        gather(i + 1, 1 - slot)

    used = i < n_used
    short = rows_ref[i] <= half

    @pl.when(used & jnp.logical_not(short))
    def _():
        ffn(blk)

    @pl.when(used & short)
    def _():
        ffn(half)

    @pl.when(jnp.logical_not(used))
    def _():
        o_ref[...] = jnp.zeros_like(o_ref)


def _moe(h2, src_tok, blk_e, blk_lo, blk_rows, n_used, wg, wu, wd, layer, blk):
    n_blocks = blk_e.shape[0]
    _, _, d, de = wg.shape
    grid_spec = pltpu.PrefetchScalarGridSpec(
        num_scalar_prefetch=5,
        grid=(n_blocks,),
        in_specs=[
            pl.BlockSpec(memory_space=pl.ANY),
            pl.BlockSpec((1, 1, d, de), lambda i, be, *_: (layer, be[i], 0, 0)),
            pl.BlockSpec((1, 1, d, de), lambda i, be, *_: (layer, be[i], 0, 0)),
            pl.BlockSpec((1, 1, de, d), lambda i, be, *_: (layer, be[i], 0, 0)),
        ],
        out_specs=pl.BlockSpec((blk, d), lambda i, *_: (i, 0)),
        scratch_shapes=[pltpu.VMEM((2, blk, d), F32), pltpu.SemaphoreType.DMA((2,))],
    )
    return pl.pallas_call(
        functools.partial(_moe_kernel, blk=blk),
        grid_spec=grid_spec,
        out_shape=jax.ShapeDtypeStruct((n_blocks * blk, d), F32),
        compiler_params=_cparams(("arbitrary",)),
        name="moe_ffn",
    )(blk_e, blk_lo, blk_rows, n_used, src_tok, h2, wg, wu, wd)


def _combine_kernel(dest_ref, yp_ref, x_ref, wt_ref, mod_ref, *rest, tm, nt, with_norm):
    if with_norm:
        gn_ref, modn_ref, o_ref, hn_ref, buf, sem = rest
    else:
        o_ref, buf, sem = rest
    i = pl.program_id(0)
    j = pl.program_id(1)
    step = i * nt + j
    slot = step % 2

    def gather(step_, slot_):
        bases = [((step_ // nt) * TOP_K + k) * (nt * tm) + (step_ % nt) * tm for k in range(TOP_K)]
        for r in range(tm):
            for k in range(TOP_K):
                pltpu.make_async_copy(yp_ref.at[pl.ds(dest_ref[bases[k] + r], 1)],
                                      buf.at[slot_, k, pl.ds(r, 1)], sem.at[slot_]).start()

    @pl.when(step == 0)
    def _():
        gather(0, 0)

    @pl.when(step + 1 < pl.num_programs(0) * nt)
    def _():
        gather(step + 1, 1 - slot)

    pltpu.make_async_copy(buf.at[slot], buf.at[slot], sem.at[slot]).wait()
    w = wt_ref[0]
    y = buf[slot, 0] * w[:, 0:1] + buf[slot, 1] * w[:, 1:2]
    out = x_ref[0] + mod_ref[0, 5:6, :] * y
    o_ref[0] = out
    if with_norm:
        hn_ref[0] = ((_rms(out) * gn_ref[...]) * (1.0 + modn_ref[0, 1:2, :]) + modn_ref[0, 0:1, :]).astype(BF16)


def _combine(dest, yp, x1, wts, mod, tm, norm=None):
    b, s, d = x1.shape
    nt = s // tm
    with_norm = norm is not None
    row = pl.BlockSpec((1, tm, d), lambda i, j, ds: (i, j, 0))
    modspec = pl.BlockSpec((1, 6, d), lambda i, j, ds: (i, 0, 0))
    in_specs = [pl.BlockSpec(memory_space=pl.ANY), row,
                pl.BlockSpec((1, tm, TOP_K), lambda i, j, ds: (i, j, 0)), modspec]
    args = [yp, x1, wts, mod]
    out_shape = [jax.ShapeDtypeStruct((b, s, d), F32)]
    out_specs = [row]
    if with_norm:
        gn, modn = norm
        in_specs += [pl.BlockSpec(gn.shape, lambda i, j, ds: (0, 0)), modspec]
        args += [gn, modn]
        out_shape.append(jax.ShapeDtypeStruct((b, s, d), BF16))
        out_specs.append(row)
    grid_spec = pltpu.PrefetchScalarGridSpec(
        num_scalar_prefetch=1, grid=(b, nt), in_specs=in_specs, out_specs=out_specs,
        scratch_shapes=[pltpu.VMEM((2, TOP_K, tm, d), F32), pltpu.SemaphoreType.DMA((2,))])
    return pl.pallas_call(
        functools.partial(_combine_kernel, tm=tm, nt=nt, with_norm=with_norm),
        grid_spec=grid_spec,
        out_shape=out_shape,
        compiler_params=_cparams(("arbitrary", "arbitrary")),
        name="combine",
    )(dest, *args)


def _dft_constants(seq):
    n1, n2 = DFT_N1, DFT_N2
    assert seq == n1 * n2 and n2 == n1 * n1
    c = np.arange(F_GROUP_DIM)
    ang = 2 * np.pi * np.outer(c, c) / F_GROUP_DIM
    fc = np.concatenate([np.cos(ang), -np.sin(ang)], axis=1)
    a, k2, m = np.meshgrid(np.arange(n1), np.arange(n2), np.arange(n2), indexing="ij")
    ang_a = -2 * np.pi * (k2 * (a + n1 * m) % seq) / seq
    tre, tim = np.cos(ang_a), np.sin(ang_a)
    ma = np.concatenate([np.concatenate([tre, -tim], axis=2), np.concatenate([tim, tre], axis=2)], axis=1)
    ang_b = -2 * np.pi * np.outer(np.arange(n1), np.arange(n1)) / n1
    eye = np.eye(n1)
    mb = np.concatenate([np.kron(np.cos(ang_b), eye), -np.kron(np.sin(ang_b), eye)], axis=1)
    mb = mb / math.sqrt(seq * F_GROUP_DIM)
    return tuple(jnp.asarray(t, F32).astype(BF16) for t in (fc, ma, mb))


def _fourier_kernel(h_ref, fc_ref, ma_ref, mb_ref, o_ref, z_ref, yre, yim):
    n1, n2, gd = DFT_N1, DFT_N2, F_GROUP_DIM
    rows = FOURIER_ROWS
    n_tiles = 2 * gd // LANES
    for r in range(0, h_ref.shape[1], rows):
        z = jnp.dot(h_ref[0, r:r + rows, :], fc_ref[...], preferred_element_type=F32)
        for t in range(n_tiles):
            z_ref[t, r:r + rows, :] = z[:, t * LANES:(t + 1) * LANES]
    for a in range(n1):
        zs = [z_ref[t, pl.ds(a, n2, stride=n1), :].astype(BF16) for t in range(n_tiles)]
        rhs = jnp.concatenate([jnp.concatenate(zs[:n_tiles // 2], axis=1),
                               jnp.concatenate(zs[n_tiles // 2:], axis=1)], axis=0)
        y = jnp.dot(ma_ref[a], rhs, preferred_element_type=F32).astype(BF16)
        yre[a] = y[:n2]
        yim[a] = y[n2:]
    for hi in range(n1):
        rhs = jnp.concatenate([yre[a, hi * n1:(hi + 1) * n1, :] for a in range(n1)]
                              + [yim[a, hi * n1:(hi + 1) * n1, :] for a in range(n1)], axis=0)
        out = jnp.dot(mb_ref[...], rhs, preferred_element_type=F32).astype(BF16)
        for k1 in range(n1):
            o_ref[0, k1 * n2 + hi * n1:k1 * n2 + (hi + 1) * n1, :] = out[k1 * n1:(k1 + 1) * n1]


def _fourier(hn, consts):
    b, s, d = hn.shape
    fc, ma, mb = consts
    gd = F_GROUP_DIM
    return pl.pallas_call(
        _fourier_kernel,
        grid=(b, F_GROUPS),
        in_specs=[pl.BlockSpec((1, s, gd), lambda i, g: (i, 0, g)),
                  _resident(fc.shape), _resident(ma.shape), _resident(mb.shape)],
        out_specs=pl.BlockSpec((1, s, gd), lambda i, g: (i, 0, g)),
        out_shape=jax.ShapeDtypeStruct((b, s, d), BF16),
        scratch_shapes=[pltpu.VMEM((2 * gd // LANES, s, LANES), F32),
                        pltpu.VMEM((DFT_N1, DFT_N2, gd), BF16), pltpu.VMEM((DFT_N1, DFT_N2, gd), BF16)],
        compiler_params=_cparams(("arbitrary", "arbitrary")),
        name="fourier",
    )(hn, fc, ma, mb)


def _rope_tables(seq):
    pos = np.arange(seq)
    row, col = (pos // GRID_W).astype(np.float64), (pos % GRID_W).astype(np.float64)

    def tables(width):
        half = width // 2
        quarter = half // 2
        freqs = ROPE_BASE ** (-np.arange(0, half, 2, dtype=np.float64) / half)
        lane = np.arange(LANES)
        ang = np.where((lane < half)[None, :], row[:, None], col[:, None]) * freqs[lane % quarter][None, :]
        live = (lane < width)[None, :]
        first = ((lane % half) < quarter)[None, :]
        cos = np.where(live, np.cos(ang), 1.0)
        sneg = np.where(live & first, -np.sin(ang), 0.0)
        spos = np.where(live & ~first, np.sin(ang), 0.0)
        return [jnp.asarray(t, F32) for t in (cos, sneg, spos)]

    return tables(HEAD_DIM) + tables(ROPE_DIM)


def _identity_tables(n):
    one, zero = jnp.ones((n, LANES), F32), jnp.zeros((n, LANES), F32)
    return [one, zero, zero, one, zero, zero]


def _pad_heads(w, lead):
    w = w.reshape(lead, B_HEADS, QK_DIM)
    return jnp.pad(w, ((0, 0), (0, 0), (0, QK_PAD - QK_DIM))).reshape(lead, B_HEADS * QK_PAD)


def kernel(x, c, ctx, c_ctx, w_ada, b_ada, g_norm, w_in, g_aqn, g_akn, g_bq_lat, w_bq_up, g_bkv_lat, w_bkv_up,
           g_bqn, g_bkn, sink, w_o_ab, w_fo, b_fo, w_router, b_router, w_gate, w_up, w_down):
    b, s, d = x.shape
    n_ctx = ctx.shape[1]

    crows = jnp.concatenate([c, c_ctx[None, :], jnp.zeros((8 - b - 1, d), F32)], axis=0)
    mods, (win, wo, wfo) = _ada(crows, w_ada, b_ada, (w_in, w_o_ab, w_fo), (IN_PAD, d, d))
    mods = mods.reshape(DEPTH, 8, 6, d)
    mod_lat = [mods[l, :b] for l in range(DEPTH)]
    mod_ctx = jnp.broadcast_to(mods[0, b][None], (b, 6, d))

    wr = jnp.pad(w_router, ((0, 0), (0, LANES - N_EXPERTS)))
    br = b_router.reshape(N_EXPERTS, 1)

    wbq = _pad_heads(w_bq_up[0], Q_LORA).astype(BF16)
    wkv = w_bkv_up[0].reshape(KV_LORA, B_HEADS, NOPE_DIM + V_DIM)
    wbkv = jnp.concatenate([wkv[:, :, :NOPE_DIM].reshape(KV_LORA, -1), wkv[:, :, NOPE_DIM:].reshape(KV_LORA, -1)],
                           axis=1).astype(BF16)
    gains = (g_aqn[0][None], g_akn[0][None], g_bq_lat[0][None], g_bkv_lat[0][None],
             jnp.pad(g_bqn[0][None], ((0, 0), (0, QK_PAD - QK_DIM))),
             jnp.pad(g_bkn[0][None], ((0, 0), (0, QK_PAD - QK_DIM))))
    gn0 = g_norm[0, 0][None]
    aq, ak, av, bq, bk, bv = _proj(x, mod_lat[0], gn0, win, wbq, wbkv, *gains, _rope_tables(s), PROJ_TM)
    _, akc, avc, _, bkc, bvc = _proj(ctx, mod_ctx, gn0, win, wbq, wbkv, *gains, _identity_tables(n_ctx), n_ctx)

    ya = _win_attn(sink[0], aq, ak, av, akc, avc)
    yb, (wg, wu, wd) = _mla_attn(bq, bk, bv, bkc, bvc, (w_gate, w_up, w_down), MLA_TQ, MLA_TK)

    n_a = A_HEADS * HEAD_DIM
    x1, h2, idx, wts = _out_proj([ya, yb], [wo[:n_a], wo[n_a:]], jnp.zeros((1, d), F32), x, mod_lat[0],
                                 g_norm[0, 1][None], wr, br, OUT_TM)
    dest, *plan = _dispatch(idx, MOE_BLK)
    yp = _moe(h2.reshape(b * s, d), *plan, wg, wu, wd, 0, MOE_BLK)
    x2, hn = _combine(dest, yp, x1, wts, mod_lat[0], COMBINE_TM, norm=(g_norm[1, 0][None], mod_lat[1]))

    f = _fourier(hn, _dft_constants(s))
    x3, h2, idx, wts = _out_proj([f], [wfo], b_fo[0][None], x2, mod_lat[1],
                                 g_norm[1, 1][None], wr, br, OUT_TM)
    dest, *plan = _dispatch(idx, MOE_BLK)
    yp = _moe(h2.reshape(b * s, d), *plan, wg, wu, wd, 1, MOE_BLK)
    (x4,) = _combine(dest, yp, x3, wts, mod_lat[1], COMBINE_TM)
    return x4
```

```python
import functools
import math

import numpy as np
import jax
import jax.numpy as jnp
from jax import lax
from jax.experimental import pallas as pl
from jax.experimental.pallas import tpu as pltpu

F32 = jnp.float32
BF16 = jnp.bfloat16
I32 = jnp.int32

D_MODEL = 2048
DEPTH = 2
GRID_W = 64
HEAD_DIM = 128
A_HEADS = 8
A_KV_HEADS = 2
A_GROUP = A_HEADS // A_KV_HEADS
WINDOW = 128
WBLK = 128
B_HEADS = 8
Q_LORA = 512
KV_LORA = 256
NOPE_DIM = 128
ROPE_DIM = 64
V_DIM = 128
QK_DIM = NOPE_DIM + ROPE_DIM
QK_PAD = 256
V_PAD = 256
IN_SPLITS = (A_HEADS * HEAD_DIM, A_KV_HEADS * HEAD_DIM, A_KV_HEADS * HEAD_DIM, Q_LORA, KV_LORA, ROPE_DIM)
IN_WIDTH = sum(IN_SPLITS)
IN_PAD = 2432
F_GROUPS = 8
F_GROUP_DIM = D_MODEL // F_GROUPS
N_EXPERTS = 16
N_GROUPS = 4
EXP_PER_GROUP = N_EXPERTS // N_GROUPS
TOP_K = 2
D_EXPERT = 1024
ROPE_BASE = 10000.0
EPS = 1e-6
LOG2E = math.log2(math.e)
LANES = 128

MOE_BLK = 512
WIN_Q = 8
PROJ_TM = 256
OUT_TM = 512
COMBINE_TM = 256
MLA_TQ = 1024
MLA_TK = 2048
ADA_TN = 1024
FOURIER_ROWS = 512
DFT_N1 = 16
DFT_N2 = 256
VMEM_LIMIT = 56 * 1024 * 1024


def _cparams(sem, **kw):
    return pltpu.CompilerParams(dimension_semantics=sem, vmem_limit_bytes=VMEM_LIMIT, **kw)


def _resident(shape):
    nd = len(shape)
    return pl.BlockSpec(shape, lambda *_: (0,) * nd, pipeline_mode=pl.Buffered(1))


def _rms(t, width=None):
    n = t.shape[-1] if width is None else width
    ss = jnp.sum(t * t, axis=-1, keepdims=True)
    return t * lax.rsqrt(ss * (1.0 / n) + EPS)


def _rope(t, cos, sneg, spos, dist):
    n = t.shape[-1]
    return t * cos + pltpu.roll(t, n - dist, 1) * sneg + pltpu.roll(t, dist, 1) * spos


ADA_SIDE_STEPS = 16


def _ada_kernel(c_ref, w_ref, b_ref, *rest, n_side, nj):
    side_in, o_ref, side_out = rest[:n_side], rest[n_side], rest[n_side + 1:]
    c = c_ref[...]
    s = c * jax.nn.sigmoid(c)
    w = w_ref[0]
    w_head = w.astype(BF16)
    w_tail = (w - w_head.astype(F32)).astype(BF16)
    s_head = s.astype(BF16)
    s_tail = (s - s_head.astype(F32)).astype(BF16)
    r = jnp.dot(jnp.concatenate([s_head, s_tail], axis=0), w_head, preferred_element_type=F32)
    rows = s.shape[0]
    o_ref[0] = (r[:rows] + r[rows:]) + jnp.dot(s_head, w_tail, preferred_element_type=F32) + b_ref[0]

    @pl.when(pl.program_id(0) * nj + pl.program_id(1) < ADA_SIDE_STEPS)
    def _():
        for w_in, w_out in zip(side_in, side_out):
            width = w_in.shape[1]
            w_out[:, :width] = w_in[...].astype(BF16)
            if w_out.shape[1] > width:
                w_out[:, width:] = jnp.zeros((w_out.shape[0], w_out.shape[1] - width), BF16)


def _ada(crows, w_ada, b_ada, side, side_widths):
    depth, d, n = w_ada.shape
    tn = ADA_TN
    nj = n // tn
    assert depth * nj >= ADA_SIDE_STEPS
    slab = lambda l, j: (jnp.minimum(l * nj + j, ADA_SIDE_STEPS - 1), 0)
    slab3 = lambda l, j: (0,) + slab(l, j)
    assert all(w.ndim == 3 and w.shape[0] == 1 for w in side)
    side_in = [pl.BlockSpec((None, w.shape[1] // ADA_SIDE_STEPS, w.shape[2]), slab3) for w in side]
    side_out = [pl.BlockSpec((w.shape[1] // ADA_SIDE_STEPS, wd), slab) for w, wd in zip(side, side_widths)]
    res = pl.pallas_call(
        functools.partial(_ada_kernel, n_side=len(side), nj=nj),
        grid=(depth, nj),
        in_specs=[
            pl.BlockSpec((8, d), lambda l, j: (0, 0)),
            pl.BlockSpec((1, d, tn), lambda l, j: (l, 0, j)),
            pl.BlockSpec((1, 1, tn), lambda l, j: (l, 0, j)),
        ] + side_in,
        out_specs=[pl.BlockSpec((1, 8, tn), lambda l, j: (l, 0, j))] + side_out,
        out_shape=[jax.ShapeDtypeStruct((depth, 8, n), F32)]
        + [jax.ShapeDtypeStruct((w.shape[1], wd), BF16) for w, wd in zip(side, side_widths)],
        compiler_params=_cparams(("arbitrary", "arbitrary")),
        name="ada",
    )(crows, w_ada, b_ada.reshape(depth, 1, n), *side)
    return res[0], res[1:]


def _proj_kernel(x_ref, mod_ref, gn_ref, win_ref, wbq_ref, wbkv_ref, gaq_ref, gak_ref, gbql_ref, gbkvl_ref,
                 gbq_ref, gbk_ref, ca_ref, sna_ref, spa_ref, cb_ref, snb_ref, spb_ref,
                 aq_ref, ak_ref, av_ref, bq_ref, bk_ref, bv_ref):
    x = x_ref[0]
    shift = mod_ref[0, 0:1, :]
    scale = mod_ref[0, 1:2, :]
    hb = ((_rms(x) * gn_ref[...]) * (1.0 + scale) + shift).astype(BF16)

    def cols(w_ref, lhs, lo, width):
        return jnp.dot(lhs, w_ref[:, lo:lo + width], preferred_element_type=F32)

    ca, sna, spa = ca_ref[...], sna_ref[...], spa_ref[...]
    cb, snb, spb = cb_ref[...], snb_ref[...], spb_ref[...]
    a_scale = HEAD_DIM ** -0.5 * LOG2E
    b_scale = QK_DIM ** -0.5 * LOG2E
    pair = 2 * HEAD_DIM

    for hp in range(A_HEADS // 2):
        pp = cols(win_ref, hb, hp * pair, pair)
        for j in range(2):
            hd = 2 * hp + j
            t = _rms(pp[:, j * HEAD_DIM:(j + 1) * HEAD_DIM]) * gaq_ref[...]
            aq_ref[0, :, hd * HEAD_DIM:(hd + 1) * HEAD_DIM] = (_rope(t, ca, sna, spa, HEAD_DIM // 4) * a_scale).astype(BF16)
    off = A_HEADS * HEAD_DIM
    pp = cols(win_ref, hb, off, A_KV_HEADS * HEAD_DIM)
    for kh in range(A_KV_HEADS):
        t = _rms(pp[:, kh * HEAD_DIM:(kh + 1) * HEAD_DIM]) * gak_ref[...]
        ak_ref[0, :, kh * HEAD_DIM:(kh + 1) * HEAD_DIM] = _rope(t, ca, sna, spa, HEAD_DIM // 4).astype(BF16)
    off += A_KV_HEADS * HEAD_DIM
    ones_col = (lax.broadcasted_iota(I32, (hb.shape[0], V_PAD - V_DIM), 1) == 0).astype(BF16)
    pp = cols(win_ref, hb, off, A_KV_HEADS * HEAD_DIM)
    for kh in range(A_KV_HEADS):
        av_ref[0, :, kh * V_PAD:kh * V_PAD + HEAD_DIM] = pp[:, kh * HEAD_DIM:(kh + 1) * HEAD_DIM].astype(BF16)
        av_ref[0, :, kh * V_PAD + HEAD_DIM:(kh + 1) * V_PAD] = ones_col
    off += A_KV_HEADS * HEAD_DIM

    ql = (_rms(cols(win_ref, hb, off, Q_LORA)) * gbql_ref[...]).astype(BF16)
    off += Q_LORA
    for hd in range(B_HEADS):
        t = _rms(cols(wbq_ref, ql, hd * QK_PAD, QK_PAD), QK_DIM) * gbq_ref[...]
        bq_ref[0, :, hd * QK_PAD:hd * QK_PAD + NOPE_DIM] = (t[:, :NOPE_DIM] * b_scale).astype(BF16)
        bq_ref[0, :, hd * QK_PAD + NOPE_DIM:(hd + 1) * QK_PAD] = (
            _rope(t[:, NOPE_DIM:], cb, snb, spb, ROPE_DIM // 4) * b_scale).astype(BF16)

    kvl = (_rms(cols(win_ref, hb, off, KV_LORA)) * gbkvl_ref[...]).astype(BF16)
    off += KV_LORA
    kr = cols(win_ref, hb, off, LANES)
    kr_ss = jnp.sum(kr * kr, axis=-1, keepdims=True)
    kr_rot = _rope(kr * gbk_ref[:, NOPE_DIM:], cb, snb, spb, ROPE_DIM // 4)
    for hp in range(B_HEADS // 2):
        kn2 = cols(wbkv_ref, kvl, hp * pair, pair)
        v2 = cols(wbkv_ref, kvl, B_HEADS * NOPE_DIM + hp * pair, pair)
        for j in range(2):
            hd = 2 * hp + j
            kn = kn2[:, j * NOPE_DIM:(j + 1) * NOPE_DIM]
            ss = jnp.sum(kn * kn, axis=-1, keepdims=True) + kr_ss
            r = lax.rsqrt(ss * (1.0 / QK_DIM) + EPS)
            bk_ref[0, :, hd * QK_PAD:hd * QK_PAD + NOPE_DIM] = (kn * r * gbk_ref[:, :NOPE_DIM]).astype(BF16)
            bk_ref[0, :, hd * QK_PAD + NOPE_DIM:(hd + 1) * QK_PAD] = (kr_rot * r).astype(BF16)
            bv_ref[0, :, hd * V_PAD:hd * V_PAD + V_DIM] = v2[:, j * V_DIM:(j + 1) * V_DIM].astype(BF16)
            bv_ref[0, :, hd * V_PAD + V_DIM:(hd + 1) * V_PAD] = ones_col


def _proj(x, mod, gn, win, wbq, wbkv, gaq, gak, gbql, gbkvl, gbq, gbk, tabs, tm):
    b, s, d = x.shape
    row = lambda w: pl.BlockSpec((1, tm, w), lambda i, j: (i, j, 0))
    tab = pl.BlockSpec((tm, LANES), lambda i, j: (j, 0))
    widths = (A_HEADS * HEAD_DIM, A_KV_HEADS * HEAD_DIM, A_KV_HEADS * V_PAD,
              B_HEADS * QK_PAD, B_HEADS * QK_PAD, B_HEADS * V_PAD)
    return pl.pallas_call(
        _proj_kernel,
        grid=(b, s // tm),
        in_specs=[row(d), pl.BlockSpec((1, 6, d), lambda i, j: (i, 0, 0)), _resident(gn.shape),
                  _resident(win.shape), _resident(wbq.shape), _resident(wbkv.shape),
                  _resident(gaq.shape), _resident(gak.shape), _resident(gbql.shape), _resident(gbkvl.shape),
                  _resident(gbq.shape), _resident(gbk.shape)] + [tab] * 6,
        out_specs=[row(w) for w in widths],
        out_shape=[jax.ShapeDtypeStruct((b, s, w), BF16) for w in widths],
        compiler_params=_cparams(("arbitrary", "arbitrary")),
        name="proj",
    )(x, mod, gn, win, wbq, wbkv, gaq, gak, gbql, gbkvl, gbq, gbk, *tabs)


def _win_kernel(sink_ref, q_ref, kp_ref, kc_ref, kn_ref, vp_ref, vc_ref, vn_ref, kx_ref, vx_ref, o_ref, *, seq):
    j = pl.program_id(1)
    rows = A_GROUP * WBLK
    band = 3 * WBLK
    keys = band + kx_ref.shape[1]
    r_iota = lax.broadcasted_iota(I32, (rows, keys), 0)
    c_iota = lax.broadcasted_iota(I32, (rows, keys), 1)
    head_of_row = lax.broadcasted_iota(I32, (rows, 1), 0) // WBLK
    dn = (((1,), (1,)), ((), ()))

    def key_block(refs, t, cols):
        p_ref, c_ref, n_ref = refs
        if t == 0:
            return p_ref[0, :, cols]
        if t == WIN_Q + 1:
            return n_ref[0, :, cols]
        return c_ref[0, (t - 1) * WBLK:t * WBLK, cols]

    for sub in range(WIN_Q):
        n = j * WIN_Q + sub
        qpos = n * WBLK + (r_iota & (WBLK - 1))
        kpos = (n - 1) * WBLK + c_iota
        valid = ((jnp.abs(qpos - kpos) <= WINDOW) & (kpos >= 0) & (kpos < seq)) | (c_iota >= band)
        for kh in range(A_KV_HEADS):
            cs = slice(kh * HEAD_DIM, (kh + 1) * HEAD_DIM)
            vs = slice(kh * V_PAD, (kh + 1) * V_PAD)
            q = jnp.concatenate(
                [q_ref[0, sub * WBLK:(sub + 1) * WBLK, (kh * A_GROUP + g) * HEAD_DIM:(kh * A_GROUP + g + 1) * HEAD_DIM]
                 for g in range(A_GROUP)], axis=0)
            kb = jnp.concatenate([key_block((kp_ref, kc_ref, kn_ref), sub + t, cs) for t in range(3)]
                                 + [kx_ref[0, :, cs]], axis=0)
            vb = jnp.concatenate([key_block((vp_ref, vc_ref, vn_ref), sub + t, vs) for t in range(3)]
                                 + [vx_ref[0, :, vs]], axis=0)
            s = jnp.where(valid, lax.dot_general(q, kb, dn, preferred_element_type=F32), -jnp.inf)
            sink = jnp.zeros((rows, 1), F32)
            for g in range(A_GROUP):
                sink = jnp.where(head_of_row == g, sink_ref[kh * A_GROUP + g] * LOG2E, sink)
            m = jnp.maximum(jnp.max(s, axis=-1, keepdims=True), sink)
            acc = jnp.dot(jnp.exp2(s - m).astype(BF16), vb, preferred_element_type=F32)
            o = acc[:, :HEAD_DIM] / (acc[:, HEAD_DIM:HEAD_DIM + 1] + jnp.exp2(sink - m))
            for g in range(A_GROUP):
                hd = kh * A_GROUP + g
                o_ref[0, sub * WBLK:(sub + 1) * WBLK, hd * HEAD_DIM:(hd + 1) * HEAD_DIM] = (
                    o[g * WBLK:(g + 1) * WBLK].astype(BF16))


def _win_attn(sink, aq, ak, av, akc, avc):
    b, s, _ = aq.shape
    nb = s // WBLK
    c = akc.shape[1]
    tq = WIN_Q * WBLK
    prev = lambda w: pl.BlockSpec((1, WBLK, w), lambda i, j: (i, jnp.maximum(j * WIN_Q - 1, 0), 0))
    cur = lambda w: pl.BlockSpec((1, tq, w), lambda i, j: (i, j, 0))
    nxt = lambda w: pl.BlockSpec((1, WBLK, w), lambda i, j: (i, jnp.minimum((j + 1) * WIN_Q, nb - 1), 0))
    cx = lambda w: pl.BlockSpec((1, c, w), lambda i, j: (i, 0, 0))
    kw, vw = A_KV_HEADS * HEAD_DIM, A_KV_HEADS * V_PAD
    qo = pl.BlockSpec((1, tq, A_HEADS * HEAD_DIM), lambda i, j: (i, j, 0))
    return pl.pallas_call(
        functools.partial(_win_kernel, seq=s),
        grid=(b, nb // WIN_Q),
        in_specs=[pl.BlockSpec(memory_space=pltpu.SMEM), qo, prev(kw), cur(kw), nxt(kw), prev(vw), cur(vw), nxt(vw),
                  cx(kw), cx(vw)],
        out_specs=qo,
        out_shape=jax.ShapeDtypeStruct(aq.shape, BF16),
        compiler_params=_cparams(("arbitrary", "arbitrary")),
        name="win_attn",
    )(sink, aq, ak, ak, ak, av, av, av, akc, avc)


def _mla_kernel(q_ref, k_ref, v_ref, kx_ref, vx_ref, *rest, tk, n_side):
    side_in, o_ref, side_out = rest[:n_side], rest[n_side], rest[n_side + 1:2 * n_side + 1]
    s_a, s_b, acc_ref = rest[2 * n_side + 1:]
    for w_in, w_out in zip(side_in, side_out):
        w_out[...] = w_in[...].astype(BF16)
    q = q_ref[0]
    dn = (((1,), (1,)), ((), ()))
    n_chunks = k_ref.shape[1] // tk
    s_bufs = (s_a, s_b)

    def scores_into(buf, c):
        s = lax.dot_general(q, k_ref[0, c * tk:(c + 1) * tk, :], dn, preferred_element_type=F32)
        buf[...] = s
        return jnp.max(s, axis=-1, keepdims=True)

    s0 = lax.dot_general(q, kx_ref[0], dn, preferred_element_type=F32)
    m = jnp.max(s0, axis=-1, keepdims=True)
    acc_ref[...] = jnp.dot(jnp.exp2(s0 - m).astype(BF16), vx_ref[0], preferred_element_type=F32)
    mx = scores_into(s_bufs[0], 0)
    for c in range(n_chunks):
        if c + 1 < n_chunks:
            mx_next = scores_into(s_bufs[(c + 1) % 2], c + 1)
        m_new = jnp.maximum(m, mx)
        p = jnp.exp2(s_bufs[c % 2][...] - m_new).astype(BF16)
        acc_ref[...] = (jnp.exp2(m - m_new) * acc_ref[...]
                        + jnp.dot(p, v_ref[0, c * tk:(c + 1) * tk, :], preferred_element_type=F32))
        m, mx = m_new, mx_next
    acc = acc_ref[...]
    o_ref[0] = (acc[:, :V_DIM] / acc[:, V_DIM:V_DIM + 1]).astype(BF16)


def _mla_attn(bq, bk, bv, bkc, bvc, side, tq, tk):
    b, s, _ = bq.shape
    c = bkc.shape[1]
    nq = s // tq
    steps = b * B_HEADS * nq
    side2d = [w.reshape(-1, w.shape[-1]) for w in side]
    slab = lambda i, h, j: ((i * B_HEADS + h) * nq + j, 0)
    side_specs = [pl.BlockSpec((w.shape[0] // steps, w.shape[1]), slab) for w in side2d]
    res = pl.pallas_call(
        functools.partial(_mla_kernel, tk=tk, n_side=len(side)),
        grid=(b, B_HEADS, nq),
        in_specs=[
            pl.BlockSpec((1, tq, QK_PAD), lambda i, h, j: (i, j, h)),
            pl.BlockSpec((1, s, QK_PAD), lambda i, h, j: (i, 0, h)),
            pl.BlockSpec((1, s, V_PAD), lambda i, h, j: (i, 0, h)),
            pl.BlockSpec((1, c, QK_PAD), lambda i, h, j: (i, 0, h)),
            pl.BlockSpec((1, c, V_PAD), lambda i, h, j: (i, 0, h)),
        ] + side_specs,
        out_specs=[pl.BlockSpec((1, tq, V_DIM), lambda i, h, j: (i, j, h))] + side_specs,
        out_shape=[jax.ShapeDtypeStruct((b, s, B_HEADS * V_DIM), BF16)]
        + [jax.ShapeDtypeStruct(w.shape, BF16) for w in side2d],
        scratch_shapes=[pltpu.VMEM((tq, tk), F32), pltpu.VMEM((tq, tk), F32), pltpu.VMEM((tq, V_PAD), F32)],
        compiler_params=_cparams(("arbitrary", "arbitrary", "arbitrary")),
        name="mla_attn",
    )(bq, bk, bv, bkc, bvc, *side2d)
    return res[0], [o.reshape(w.shape) for o, w in zip(res[1:], side)]


def _route(sel, aff):
    scores = []
    for g in range(N_GROUPS):
        r = sel[g * EXP_PER_GROUP:(g + 1) * EXP_PER_GROUP]
        best = None
        for a in range(EXP_PER_GROUP):
            for b in range(a + 1, EXP_PER_GROUP):
                pair = r[a] + r[b]
                best = pair if best is None else jnp.maximum(best, pair)
        scores.append(best)
    top, grp = scores[0], jnp.zeros_like(scores[0], dtype=I32)
    for g in range(1, N_GROUPS):
        take = scores[g] > top
        grp = jnp.where(take, g, grp)
        top = jnp.where(take, scores[g], top)
    masked = [jnp.where(grp == e // EXP_PER_GROUP, sel[e], -jnp.inf) for e in range(N_EXPERTS)]

    def argmax_first(vals, skip=None):
        bv = jnp.full_like(vals[0], -jnp.inf)
        bi = jnp.full_like(grp, -1)
        for e in range(N_EXPERTS):
            take = vals[e] > bv
            if skip is not None:
                take = take & (skip != e)
            bi = jnp.where(take, e, bi)
            bv = jnp.where(take, vals[e], bv)
        return bi

    i0 = argmax_first(masked)
    i1 = argmax_first(masked, skip=i0)
    a0 = jnp.zeros_like(aff[0])
    a1 = jnp.zeros_like(aff[0])
    for e in range(N_EXPERTS):
        a0 = jnp.where(i0 == e, aff[e], a0)
        a1 = jnp.where(i1 == e, aff[e], a1)
    tot = a0 + a1
    return i0, i1, a0 / tot, a1 / tot


def _out_kernel(*refs, n_lhs):
    lhs = refs[:n_lhs]
    ws = refs[n_lhs:2 * n_lhs]
    bias_ref, x_ref, mod_ref, gn_ref, wr_ref, br_ref, x1_ref, hp_ref, idx_ref, wts_ref = refs[2 * n_lhs:]
    y = bias_ref[...]
    for a, w in zip(lhs, ws):
        y = y + jnp.dot(a[0], w[...], preferred_element_type=F32)
    x1 = x_ref[0] + mod_ref[0, 2:3, :] * y
    x1_ref[0] = x1
    h2 = (_rms(x1) * gn_ref[...]) * (1.0 + mod_ref[0, 4:5, :]) + mod_ref[0, 3:4, :]
    hp_ref[0] = h2
    w = wr_ref[...]
    w_head = w.astype(BF16)
    w_tail = (w - w_head.astype(F32)).astype(BF16)
    h_head = h2.astype(BF16)
    h_tail = (h2 - h_head.astype(F32)).astype(BF16)
    t = jnp.dot(h_head, jnp.concatenate([w_head, w_tail], axis=1), preferred_element_type=F32)
    logits = (t[:, :LANES] + t[:, LANES:]) + jnp.dot(h_tail, w_head, preferred_element_type=F32)
    lt = logits.T[:N_EXPERTS]
    aff_t = jax.nn.sigmoid(lt)
    sel_t = aff_t + br_ref[...]
    sel = [sel_t[e:e + 1] for e in range(N_EXPERTS)]
    aff = [aff_t[e:e + 1] for e in range(N_EXPERTS)]
    i0, i1, w0, w1 = _route(sel, aff)
    idx_ref[0] = jnp.concatenate([i0, i1], axis=0)
    w_rows = jnp.concatenate([w0, w1, jnp.zeros((LANES - TOP_K, w0.shape[1]), F32)], axis=0)
    wts_ref[0] = w_rows.T[:, :TOP_K]


def _out_proj(lhs, ws, bias, x, mod, gn, wr, br, tm):
    b, s, d = x.shape
    n_lhs = len(lhs)
    row = pl.BlockSpec((1, tm, d), lambda i, j: (i, j, 0))
    in_specs = ([pl.BlockSpec((1, tm, a.shape[-1]), lambda i, j: (i, j, 0)) for a in lhs]
                + [_resident(w.shape) for w in ws]
                + [_resident(bias.shape), row, pl.BlockSpec((1, 6, d), lambda i, j: (i, 0, 0)),
                   _resident(gn.shape), _resident(wr.shape), _resident(br.shape)])
    return pl.pallas_call(
        functools.partial(_out_kernel, n_lhs=n_lhs),
        grid=(b, s // tm),
        in_specs=in_specs,
        out_specs=[row, row,
                   pl.BlockSpec((1, TOP_K, tm), lambda i, j: (i, 0, j)),
                   pl.BlockSpec((1, tm, TOP_K), lambda i, j: (i, j, 0))],
        out_shape=[jax.ShapeDtypeStruct((b, s, d), F32), jax.ShapeDtypeStruct((b, s, d), F32),
                   jax.ShapeDtypeStruct((b, TOP_K, s), I32), jax.ShapeDtypeStruct((b, s, TOP_K), F32)],
        compiler_params=_cparams(("arbitrary", "arbitrary")),
        name="out_proj",
    )(*lhs, *ws, bias, x, mod, gn, wr, br)


def _dispatch(idx, blk):
    b, _, s = idx.shape
    n_asg = b * TOP_K * s
    assert N_EXPERTS * n_asg < 2 ** 31
    e = idx.reshape(n_asg)
    onehot = (e[None, :] == jnp.arange(N_EXPERTS, dtype=I32)[:, None]).astype(I32)
    csum = jnp.cumsum(onehot, axis=1)
    counts = csum[:, -1]
    rank = jnp.sum((csum - onehot) * onehot, axis=0)
    padded = (counts + blk - 1) // blk * blk
    pad_end = jnp.cumsum(padded)
    pad_start = pad_end - padded
    start = jnp.cumsum(counts) - counts
    dest = pad_start[e] + rank
    src_asg = jnp.sort(e * n_asg + jnp.arange(n_asg, dtype=I32)) % n_asg
    src_tok = src_asg // (TOP_K * s) * s + src_asg % s
    src_tok = jnp.concatenate([src_tok, jnp.zeros((blk,), src_tok.dtype)])
    n_blocks = -(-n_asg // blk) + N_EXPERTS
    n_used = pad_end[-1] // blk
    blk_ids = jnp.minimum(jnp.arange(n_blocks, dtype=I32), n_used - 1)
    blk_e = jnp.sum((blk_ids[:, None] * blk >= pad_end[None, :]).astype(I32), axis=1)
    blk_e = jnp.minimum(blk_e, N_EXPERTS - 1)
    blk_lo = start[blk_e] + blk_ids * blk - pad_start[blk_e]
    blk_rows = jnp.clip(start[blk_e] + counts[blk_e] - blk_lo, 0, blk)
    as_i32 = lambda a: a.astype(I32)
    return (as_i32(dest), as_i32(src_tok), as_i32(blk_e), as_i32(blk_lo), as_i32(blk_rows),
            as_i32(n_used.reshape(1)))


def _moe_kernel(blk_e_ref, lo_ref, rows_ref, n_used_ref, tok_ref, h_ref, wg_ref, wu_ref, wd_ref, o_ref, xbuf, sem,
                *, blk):
    i = pl.program_id(0)
    n_used = n_used_ref[0]
    slot = i % 2
    half = blk // 2

    def gather(block, slot_):
        lo = lo_ref[block]

        def row_copy(r):
            t = tok_ref[lo + r]
            pltpu.make_async_copy(h_ref.at[pl.ds(t, 1)], xbuf.at[slot_, pl.ds(r, 1)], sem.at[slot_]).start()

        for r in range(half):
            row_copy(r)

        @pl.when(rows_ref[block] > half)
        def _():
            for r in range(half, blk):
                row_copy(r)

    def ffn(rows):
        pltpu.make_async_copy(xbuf.at[slot, pl.ds(0, rows)], xbuf.at[slot, pl.ds(0, rows)], sem.at[slot]).wait()
        xb = xbuf[slot, :rows, :].astype(BF16)
        g = jnp.dot(xb, wg_ref[0, 0], preferred_element_type=F32)
        u = jnp.dot(xb, wu_ref[0, 0], preferred_element_type=F32)
        a = (g * jax.nn.sigmoid(g) * u).astype(BF16)
        o_ref[:rows, :] = jnp.dot(a, wd_ref[0, 0], preferred_element_type=F32)
        if rows < blk:
            o_ref[rows:, :] = jnp.zeros((blk - rows, o_ref.shape[1]), F32)

    @pl.when(i == 0)
    def _():
        gather(0, 0)

    @pl.when(i + 1 < n_used)
    def _():
        gather(i + 1, 1 - slot)

    used = i < n_used
    short = rows_ref[i] <= half

    @pl.when(used & jnp.logical_not(short))
    def _():
        ffn(blk)

    @pl.when(used & short)
    def _():
        ffn(half)

    @pl.when(jnp.logical_not(used))
    def _():
        o_ref[...] = jnp.zeros_like(o_ref)


def _moe(h2, src_tok, blk_e, blk_lo, blk_rows, n_used, wg, wu, wd, layer, blk):
    n_blocks = blk_e.shape[0]
    _, _, d, de = wg.shape
    grid_spec = pltpu.PrefetchScalarGridSpec(
        num_scalar_prefetch=5,
        grid=(n_blocks,),
        in_specs=[
            pl.BlockSpec(memory_space=pl.ANY),
            pl.BlockSpec((1, 1, d, de), lambda i, be, *_: (layer, be[i], 0, 0)),
            pl.BlockSpec((1, 1, d, de), lambda i, be, *_: (layer, be[i], 0, 0)),
            pl.BlockSpec((1, 1, de, d), lambda i, be, *_: (layer, be[i], 0, 0)),
        ],
        out_specs=pl.BlockSpec((blk, d), lambda i, *_: (i, 0)),
        scratch_shapes=[pltpu.VMEM((2, blk, d), F32), pltpu.SemaphoreType.DMA((2,))],
    )
    return pl.pallas_call(
        functools.partial(_moe_kernel, blk=blk),
        grid_spec=grid_spec,
        out_shape=jax.ShapeDtypeStruct((n_blocks * blk, d), F32),
        compiler_params=_cparams(("arbitrary",)),
        name="moe_ffn",
    )(blk_e, blk_lo, blk_rows, n_used, src_tok, h2, wg, wu, wd)


def _combine_kernel(dest_ref, yp_ref, x_ref, wt_ref, mod_ref, *rest, tm, nt, with_norm):
    if with_norm:
        gn_ref, modn_ref, o_ref, hn_ref, buf, sem = rest
    else:
        o_ref, buf, sem = rest
    i = pl.program_id(0)
    j = pl.program_id(1)
    step = i * nt + j
    slot = step % 2

    def gather(step_, slot_):
        bases = [((step_ // nt) * TOP_K + k) * (nt * tm) + (step_ % nt) * tm for k in range(TOP_K)]
        for r in range(tm):
            for k in range(TOP_K):
                pltpu.make_async_copy(yp_ref.at[pl.ds(dest_ref[bases[k] + r], 1)],
                                      buf.at[slot_, k, pl.ds(r, 1)], sem.at[slot_]).start()

    @pl.when(step == 0)
    def _():
        gather(0, 0)

    @pl.when(step + 1 < pl.num_programs(0) * nt)
    def _():
        gather(step + 1, 1 - slot)

    pltpu.make_async_copy(buf.at[slot], buf.at[slot], sem.at[slot]).wait()
    w = wt_ref[0]
    y = buf[slot, 0] * w[:, 0:1] + buf[slot, 1] * w[:, 1:2]
    out = x_ref[0] + mod_ref[0, 5:6, :] * y
    o_ref[0] = out
    if with_norm:
        hn_ref[0] = ((_rms(out) * gn_ref[...]) * (1.0 + modn_ref[0, 1:2, :]) + modn_ref[0, 0:1, :]).astype(BF16)


def _combine(dest, yp, x1, wts, mod, tm, norm=None):
    b, s, d = x1.shape
    nt = s // tm
    with_norm = norm is not None
    row = pl.BlockSpec((1, tm, d), lambda i, j, ds: (i, j, 0))
    modspec = pl.BlockSpec((1, 6, d), lambda i, j, ds: (i, 0, 0))
    in_specs = [pl.BlockSpec(memory_space=pl.ANY), row,
                pl.BlockSpec((1, tm, TOP_K), lambda i, j, ds: (i, j, 0)), modspec]
    args = [yp, x1, wts, mod]
    out_shape = [jax.ShapeDtypeStruct((b, s, d), F32)]
    out_specs = [row]
    if with_norm:
        gn, modn = norm
        in_specs += [pl.BlockSpec(gn.shape, lambda i, j, ds: (0, 0)), modspec]
        args += [gn, modn]
        out_shape.append(jax.ShapeDtypeStruct((b, s, d), BF16))
        out_specs.append(row)
    grid_spec = pltpu.PrefetchScalarGridSpec(
        num_scalar_prefetch=1, grid=(b, nt), in_specs=in_specs, out_specs=out_specs,
        scratch_shapes=[pltpu.VMEM((2, TOP_K, tm, d), F32), pltpu.SemaphoreType.DMA((2,))])
    return pl.pallas_call(
        functools.partial(_combine_kernel, tm=tm, nt=nt, with_norm=with_norm),
        grid_spec=grid_spec,
        out_shape=out_shape,
        compiler_params=_cparams(("arbitrary", "arbitrary")),
        name="combine",
    )(dest, *args)


def _dft_constants(seq):
    n1, n2 = DFT_N1, DFT_N2
    assert seq == n1 * n2 and n2 == n1 * n1
    c = np.arange(F_GROUP_DIM)
    ang = 2 * np.pi * np.outer(c, c) / F_GROUP_DIM
    fc = np.concatenate([np.cos(ang), -np.sin(ang)], axis=1)
    a, k2, m = np.meshgrid(np.arange(n1), np.arange(n2), np.arange(n2), indexing="ij")
    ang_a = -2 * np.pi * (k2 * (a + n1 * m) % seq) / seq
    tre, tim = np.cos(ang_a), np.sin(ang_a)
    ma = np.concatenate([np.concatenate([tre, -tim], axis=2), np.concatenate([tim, tre], axis=2)], axis=1)
    ang_b = -2 * np.pi * np.outer(np.arange(n1), np.arange(n1)) / n1
    eye = np.eye(n1)
    mb = np.concatenate([np.kron(np.cos(ang_b), eye), -np.kron(np.sin(ang_b), eye)], axis=1)
    mb = mb / math.sqrt(seq * F_GROUP_DIM)
    return tuple(jnp.asarray(t, F32).astype(BF16) for t in (fc, ma, mb))


def _fourier_kernel(h_ref, fc_ref, ma_ref, mb_ref, o_ref, z_ref, yre, yim):
    n1, n2, gd = DFT_N1, DFT_N2, F_GROUP_DIM
    rows = FOURIER_ROWS
    n_tiles = 2 * gd // LANES
    for r in range(0, h_ref.shape[1], rows):
        z = jnp.dot(h_ref[0, r:r + rows, :], fc_ref[...], preferred_element_type=F32)
        for t in range(n_tiles):
            z_ref[t, r:r + rows, :] = z[:, t * LANES:(t + 1) * LANES]
    for a in range(n1):
        zs = [z_ref[t, pl.ds(a, n2, stride=n1), :].astype(BF16) for t in range(n_tiles)]
        rhs = jnp.concatenate([jnp.concatenate(zs[:n_tiles // 2], axis=1),
                               jnp.concatenate(zs[n_tiles // 2:], axis=1)], axis=0)
        y = jnp.dot(ma_ref[a], rhs, preferred_element_type=F32).astype(BF16)
        yre[a] = y[:n2]
        yim[a] = y[n2:]
    for hi in range(n1):
        rhs = jnp.concatenate([yre[a, hi * n1:(hi + 1) * n1, :] for a in range(n1)]
                              + [yim[a, hi * n1:(hi + 1) * n1, :] for a in range(n1)], axis=0)
        out = jnp.dot(mb_ref[...], rhs, preferred_element_type=F32).astype(BF16)
        for k1 in range(n1):
            o_ref[0, k1 * n2 + hi * n1:k1 * n2 + (hi + 1) * n1, :] = out[k1 * n1:(k1 + 1) * n1]


def _fourier(hn, consts):
    b, s, d = hn.shape
    fc, ma, mb = consts
    gd = F_GROUP_DIM
    return pl.pallas_call(
        _fourier_kernel,
        grid=(b, F_GROUPS),
        in_specs=[pl.BlockSpec((1, s, gd), lambda i, g: (i, 0, g)),
                  _resident(fc.shape), _resident(ma.shape), _resident(mb.shape)],
        out_specs=pl.BlockSpec((1, s, gd), lambda i, g: (i, 0, g)),
        out_shape=jax.ShapeDtypeStruct((b, s, d), BF16),
        scratch_shapes=[pltpu.VMEM((2 * gd // LANES, s, LANES), F32),
                        pltpu.VMEM((DFT_N1, DFT_N2, gd), BF16), pltpu.VMEM((DFT_N1, DFT_N2, gd), BF16)],
        compiler_params=_cparams(("arbitrary", "arbitrary")),
        name="fourier",
    )(hn, fc, ma, mb)


def _rope_tables(seq):
    pos = np.arange(seq)
    row, col = (pos // GRID_W).astype(np.float64), (pos % GRID_W).astype(np.float64)

    def tables(width):
        half = width // 2
        quarter = half // 2
        freqs = ROPE_BASE ** (-np.arange(0, half, 2, dtype=np.float64) / half)
        lane = np.arange(LANES)
        ang = np.where((lane < half)[None, :], row[:, None], col[:, None]) * freqs[lane % quarter][None, :]
        live = (lane < width)[None, :]
        first = ((lane % half) < quarter)[None, :]
        cos = np.where(live, np.cos(ang), 1.0)
        sneg = np.where(live & first, -np.sin(ang), 0.0)
        spos = np.where(live & ~first, np.sin(ang), 0.0)
        return [jnp.asarray(t, F32) for t in (cos, sneg, spos)]

    return tables(HEAD_DIM) + tables(ROPE_DIM)


def _identity_tables(n):
    one, zero = jnp.ones((n, LANES), F32), jnp.zeros((n, LANES), F32)
    return [one, zero, zero, one, zero, zero]


def _pad_heads(w, lead):
    w = w.reshape(lead, B_HEADS, QK_DIM)
    return jnp.pad(w, ((0, 0), (0, 0), (0, QK_PAD - QK_DIM))).reshape(lead, B_HEADS * QK_PAD)


def kernel(x, c, ctx, c_ctx, w_ada, b_ada, g_norm, w_in, g_aqn, g_akn, g_bq_lat, w_bq_up, g_bkv_lat, w_bkv_up,
           g_bqn, g_bkn, sink, w_o_ab, w_fo, b_fo, w_router, b_router, w_gate, w_up, w_down):
    b, s, d = x.shape
    n_ctx = ctx.shape[1]

    crows = jnp.concatenate([c, c_ctx[None, :], jnp.zeros((8 - b - 1, d), F32)], axis=0)
    mods, (win, wo, wfo) = _ada(crows, w_ada, b_ada, (w_in, w_o_ab, w_fo), (IN_PAD, d, d))
    mods = mods.reshape(DEPTH, 8, 6, d)
    mod_lat = [mods[l, :b] for l in range(DEPTH)]
    mod_ctx = jnp.broadcast_to(mods[0, b][None], (b, 6, d))

    wr = jnp.pad(w_router, ((0, 0), (0, LANES - N_EXPERTS)))
    br = b_router.reshape(N_EXPERTS, 1)

    wbq = _pad_heads(w_bq_up[0], Q_LORA).astype(BF16)
    wkv = w_bkv_up[0].reshape(KV_LORA, B_HEADS, NOPE_DIM + V_DIM)
    wbkv = jnp.concatenate([wkv[:, :, :NOPE_DIM].reshape(KV_LORA, -1), wkv[:, :, NOPE_DIM:].reshape(KV_LORA, -1)],
                           axis=1).astype(BF16)
    gains = (g_aqn[0][None], g_akn[0][None], g_bq_lat[0][None], g_bkv_lat[0][None],
             jnp.pad(g_bqn[0][None], ((0, 0), (0, QK_PAD - QK_DIM))),
             jnp.pad(g_bkn[0][None], ((0, 0), (0, QK_PAD - QK_DIM))))
    gn0 = g_norm[0, 0][None]
    aq, ak, av, bq, bk, bv = _proj(x, mod_lat[0], gn0, win, wbq, wbkv, *gains, _rope_tables(s), PROJ_TM)
    _, akc, avc, _, bkc, bvc = _proj(ctx, mod_ctx, gn0, win, wbq, wbkv, *gains, _identity_tables(n_ctx), n_ctx)

    ya = _win_attn(sink[0], aq, ak, av, akc, avc)
    yb, (wg, wu, wd) = _mla_attn(bq, bk, bv, bkc, bvc, (w_gate, w_up, w_down), MLA_TQ, MLA_TK)

    n_a = A_HEADS * HEAD_DIM
    x1, h2, idx, wts = _out_proj([ya, yb], [wo[:n_a], wo[n_a:]], jnp.zeros((1, d), F32), x, mod_lat[0],
                                 g_norm[0, 1][None], wr, br, OUT_TM)
    dest, *plan = _dispatch(idx, MOE_BLK)
    yp = _moe(h2.reshape(b * s, d), *plan, wg, wu, wd, 0, MOE_BLK)
    x2, hn = _combine(dest, yp, x1, wts, mod_lat[0], COMBINE_TM, norm=(g_norm[1, 0][None], mod_lat[1]))

    f = _fourier(hn, _dft_constants(s))
    x3, h2, idx, wts = _out_proj([f], [wfo], b_fo[0][None], x2, mod_lat[1],
                                 g_norm[1, 1][None], wr, br, OUT_TM)
    dest, *plan = _dispatch(idx, MOE_BLK)
    yp = _moe(h2.reshape(b * s, d), *plan, wg, wu, wd, 1, MOE_BLK)
    (x4,) = _combine(dest, yp, x3, wts, mod_lat[1], COMBINE_TM)
    return x4
```

```python
import functools
import math

import numpy as np
import jax
import jax.numpy as jnp
from jax import lax
from jax.experimental import pallas as pl
from jax.experimental.pallas import tpu as pltpu

F32 = jnp.float32
BF16 = jnp.bfloat16
I32 = jnp.int32

D_MODEL = 2048
DEPTH = 2
GRID_W = 64
HEAD_DIM = 128
A_HEADS = 8
A_KV_HEADS = 2
A_GROUP = A_HEADS // A_KV_HEADS
WINDOW = 128
WBLK = 128
B_HEADS = 8
Q_LORA = 512
KV_LORA = 256
NOPE_DIM = 128
ROPE_DIM = 64
V_DIM = 128
QK_DIM = NOPE_DIM + ROPE_DIM
QK_PAD = 256
V_PAD = 256
IN_SPLITS = (A_HEADS * HEAD_DIM, A_KV_HEADS * HEAD_DIM, A_KV_HEADS * HEAD_DIM, Q_LORA, KV_LORA, ROPE_DIM)
IN_WIDTH = sum(IN_SPLITS)
IN_PAD = 2432
F_GROUPS = 8
F_GROUP_DIM = D_MODEL // F_GROUPS
N_EXPERTS = 16
N_GROUPS = 4
EXP_PER_GROUP = N_EXPERTS // N_GROUPS
TOP_K = 2
D_EXPERT = 1024
ROPE_BASE = 10000.0
EPS = 1e-6
LOG2E = math.log2(math.e)
LANES = 128

MOE_BLK = 512
WIN_Q = 8
PROJ_TM = 256
OUT_TM = 512
COMBINE_TM = 256
MLA_TQ = 1024
MLA_TK = 2048
ADA_TN = 1024
FOURIER_ROWS = 512
DFT_N1 = 16
DFT_N2 = 256
VMEM_LIMIT = 56 * 1024 * 1024


def _cparams(sem, **kw):
    return pltpu.CompilerParams(dimension_semantics=sem, vmem_limit_bytes=VMEM_LIMIT, **kw)


def _resident(shape):
    nd = len(shape)
    return pl.BlockSpec(shape, lambda *_: (0,) * nd, pipeline_mode=pl.Buffered(1))


def _rms(t, width=None):
    n = t.shape[-1] if width is None else width
    ss = jnp.sum(t * t, axis=-1, keepdims=True)
    return t * lax.rsqrt(ss * (1.0 / n) + EPS)


def _rope(t, cos, sneg, spos, dist):
    n = t.shape[-1]
    return t * cos + pltpu.roll(t, n - dist, 1) * sneg + pltpu.roll(t, dist, 1) * spos


ADA_SIDE_STEPS = 16


def _ada_kernel(c_ref, w_ref, b_ref, *rest, n_side, nj):
    side_in, o_ref, side_out = rest[:n_side], rest[n_side], rest[n_side + 1:]
    c = c_ref[...]
    s = c * jax.nn.sigmoid(c)
    w = w_ref[0]
    w_head = w.astype(BF16)
    w_tail = (w - w_head.astype(F32)).astype(BF16)
    s_head = s.astype(BF16)
    s_tail = (s - s_head.astype(F32)).astype(BF16)
    r = jnp.dot(jnp.concatenate([s_head, s_tail], axis=0), w_head, preferred_element_type=F32)
    rows = s.shape[0]
    o_ref[0] = (r[:rows] + r[rows:]) + jnp.dot(s_head, w_tail, preferred_element_type=F32) + b_ref[0]

    @pl.when(pl.program_id(0) * nj + pl.program_id(1) < ADA_SIDE_STEPS)
    def _():
        for w_in, w_out in zip(side_in, side_out):
            width = w_in.shape[1]
            w_out[:, :width] = w_in[...].astype(BF16)
            if w_out.shape[1] > width:
                w_out[:, width:] = jnp.zeros((w_out.shape[0], w_out.shape[1] - width), BF16)


def _ada(crows, w_ada, b_ada, side, side_widths):
    depth, d, n = w_ada.shape
    tn = ADA_TN
    nj = n // tn
    assert depth * nj >= ADA_SIDE_STEPS
    slab = lambda l, j: (jnp.minimum(l * nj + j, ADA_SIDE_STEPS - 1), 0)
    slab3 = lambda l, j: (0,) + slab(l, j)
    assert all(w.ndim == 3 and w.shape[0] == 1 for w in side)
    side_in = [pl.BlockSpec((None, w.shape[1] // ADA_SIDE_STEPS, w.shape[2]), slab3) for w in side]
    side_out = [pl.BlockSpec((w.shape[1] // ADA_SIDE_STEPS, wd), slab) for w, wd in zip(side, side_widths)]
    res = pl.pallas_call(
        functools.partial(_ada_kernel, n_side=len(side), nj=nj),
        grid=(depth, nj),
        in_specs=[
            pl.BlockSpec((8, d), lambda l, j: (0, 0)),
            pl.BlockSpec((1, d, tn), lambda l, j: (l, 0, j)),
            pl.BlockSpec((1, 1, tn), lambda l, j: (l, 0, j)),
        ] + side_in,
        out_specs=[pl.BlockSpec((1, 8, tn), lambda l, j: (l, 0, j))] + side_out,
        out_shape=[jax.ShapeDtypeStruct((depth, 8, n), F32)]
        + [jax.ShapeDtypeStruct((w.shape[1], wd), BF16) for w, wd in zip(side, side_widths)],
        compiler_params=_cparams(("arbitrary", "arbitrary")),
        name="ada",
    )(crows, w_ada, b_ada.reshape(depth, 1, n), *side)
    return res[0], res[1:]


def _proj_kernel(x_ref, mod_ref, gn_ref, win_ref, wbq_ref, wbkv_ref, gaq_ref, gak_ref, gbql_ref, gbkvl_ref,
                 gbq_ref, gbk_ref, ca_ref, sna_ref, spa_ref, cb_ref, snb_ref, spb_ref,
                 aq_ref, ak_ref, av_ref, bq_ref, bk_ref, bv_ref):
    x = x_ref[0]
    shift = mod_ref[0, 0:1, :]
    scale = mod_ref[0, 1:2, :]
    hb = ((_rms(x) * gn_ref[...]) * (1.0 + scale) + shift).astype(BF16)

    def cols(w_ref, lhs, lo, width):
        return jnp.dot(lhs, w_ref[:, lo:lo + width], preferred_element_type=F32)

    ca, sna, spa = ca_ref[...], sna_ref[...], spa_ref[...]
    cb, snb, spb = cb_ref[...], snb_ref[...], spb_ref[...]
    a_scale = HEAD_DIM ** -0.5 * LOG2E
    b_scale = QK_DIM ** -0.5 * LOG2E
    pair = 2 * HEAD_DIM

    for hp in range(A_HEADS // 2):
        pp = cols(win_ref, hb, hp * pair, pair)
        for j in range(2):
            hd = 2 * hp + j
            t = _rms(pp[:, j * HEAD_DIM:(j + 1) * HEAD_DIM]) * gaq_ref[...]
            aq_ref[0, :, hd * HEAD_DIM:(hd + 1) * HEAD_DIM] = (_rope(t, ca, sna, spa, HEAD_DIM // 4) * a_scale).astype(BF16)
    off = A_HEADS * HEAD_DIM
    pp = cols(win_ref, hb, off, A_KV_HEADS * HEAD_DIM)
    for kh in range(A_KV_HEADS):
        t = _rms(pp[:, kh * HEAD_DIM:(kh + 1) * HEAD_DIM]) * gak_ref[...]
        ak_ref[0, :, kh * HEAD_DIM:(kh + 1) * HEAD_DIM] = _rope(t, ca, sna, spa, HEAD_DIM // 4).astype(BF16)
    off += A_KV_HEADS * HEAD_DIM
    ones_col = (lax.broadcasted_iota(I32, (hb.shape[0], V_PAD - V_DIM), 1) == 0).astype(BF16)
    pp = cols(win_ref, hb, off, A_KV_HEADS * HEAD_DIM)
    for kh in range(A_KV_HEADS):
        av_ref[0, :, kh * V_PAD:kh * V_PAD + HEAD_DIM] = pp[:, kh * HEAD_DIM:(kh + 1) * HEAD_DIM].astype(BF16)
        av_ref[0, :, kh * V_PAD + HEAD_DIM:(kh + 1) * V_PAD] = ones_col
    off += A_KV_HEADS * HEAD_DIM

    ql = (_rms(cols(win_ref, hb, off, Q_LORA)) * gbql_ref[...]).astype(BF16)
    off += Q_LORA
    for hd in range(B_HEADS):
        t = _rms(cols(wbq_ref, ql, hd * QK_PAD, QK_PAD), QK_DIM) * gbq_ref[...]
        bq_ref[0, :, hd * QK_PAD:hd * QK_PAD + NOPE_DIM] = (t[:, :NOPE_DIM] * b_scale).astype(BF16)
        bq_ref[0, :, hd * QK_PAD + NOPE_DIM:(hd + 1) * QK_PAD] = (
            _rope(t[:, NOPE_DIM:], cb, snb, spb, ROPE_DIM // 4) * b_scale).astype(BF16)

    kvl = (_rms(cols(win_ref, hb, off, KV_LORA)) * gbkvl_ref[...]).astype(BF16)
    off += KV_LORA
    kr = cols(win_ref, hb, off, LANES)
    kr_ss = jnp.sum(kr * kr, axis=-1, keepdims=True)
    kr_rot = _rope(kr * gbk_ref[:, NOPE_DIM:], cb, snb, spb, ROPE_DIM // 4)
    for hp in range(B_HEADS // 2):
        kn2 = cols(wbkv_ref, kvl, hp * pair, pair)
        v2 = cols(wbkv_ref, kvl, B_HEADS * NOPE_DIM + hp * pair, pair)
        for j in range(2):
            hd = 2 * hp + j
            kn = kn2[:, j * NOPE_DIM:(j + 1) * NOPE_DIM]
            ss = jnp.sum(kn * kn, axis=-1, keepdims=True) + kr_ss
            r = lax.rsqrt(ss * (1.0 / QK_DIM) + EPS)
            bk_ref[0, :, hd * QK_PAD:hd * QK_PAD + NOPE_DIM] = (kn * r * gbk_ref[:, :NOPE_DIM]).astype(BF16)
            bk_ref[0, :, hd * QK_PAD + NOPE_DIM:(hd + 1) * QK_PAD] = (kr_rot * r).astype(BF16)
            bv_ref[0, :, hd * V_PAD:hd * V_PAD + V_DIM] = v2[:, j * V_DIM:(j + 1) * V_DIM].astype(BF16)
            bv_ref[0, :, hd * V_PAD + V_DIM:(hd + 1) * V_PAD] = ones_col


def _proj(x, mod, gn, win, wbq, wbkv, gaq, gak, gbql, gbkvl, gbq, gbk, tabs, tm):
    b, s, d = x.shape
    row = lambda w: pl.BlockSpec((1, tm, w), lambda i, j: (i, j, 0))
    tab = pl.BlockSpec((tm, LANES), lambda i, j: (j, 0))
    widths = (A_HEADS * HEAD_DIM, A_KV_HEADS * HEAD_DIM, A_KV_HEADS * V_PAD,
              B_HEADS * QK_PAD, B_HEADS * QK_PAD, B_HEADS * V_PAD)
    return pl.pallas_call(
        _proj_kernel,
        grid=(b, s // tm),
        in_specs=[row(d), pl.BlockSpec((1, 6, d), lambda i, j: (i, 0, 0)), _resident(gn.shape),
                  _resident(win.shape), _resident(wbq.shape), _resident(wbkv.shape),
                  _resident(gaq.shape), _resident(gak.shape), _resident(gbql.shape), _resident(gbkvl.shape),
                  _resident(gbq.shape), _resident(gbk.shape)] + [tab] * 6,
        out_specs=[row(w) for w in widths],
        out_shape=[jax.ShapeDtypeStruct((b, s, w), BF16) for w in widths],
        compiler_params=_cparams(("arbitrary", "arbitrary")),
        name="proj",
    )(x, mod, gn, win, wbq, wbkv, gaq, gak, gbql, gbkvl, gbq, gbk, *tabs)


def _win_kernel(sink_ref, q_ref, kp_ref, kc_ref, kn_ref, vp_ref, vc_ref, vn_ref, kx_ref, vx_ref, o_ref, *, seq):
    j = pl.program_id(1)
    rows = A_GROUP * WBLK
    band = 3 * WBLK
    keys = band + kx_ref.shape[1]
    r_iota = lax.broadcasted_iota(I32, (rows, keys), 0)
    c_iota = lax.broadcasted_iota(I32, (rows, keys), 1)
    head_of_row = lax.broadcasted_iota(I32, (rows, 1), 0) // WBLK
    dn = (((1,), (1,)), ((), ()))

    def key_block(refs, t, cols):
        p_ref, c_ref, n_ref = refs
        if t == 0:
            return p_ref[0, :, cols]
        if t == WIN_Q + 1:
            return n_ref[0, :, cols]
        return c_ref[0, (t - 1) * WBLK:t * WBLK, cols]

    for sub in range(WIN_Q):
        n = j * WIN_Q + sub
        qpos = n * WBLK + (r_iota & (WBLK - 1))
        kpos = (n - 1) * WBLK + c_iota
        valid = ((jnp.abs(qpos - kpos) <= WINDOW) & (kpos >= 0) & (kpos < seq)) | (c_iota >= band)
        for kh in range(A_KV_HEADS):
            cs = slice(kh * HEAD_DIM, (kh + 1) * HEAD_DIM)
            vs = slice(kh * V_PAD, (kh + 1) * V_PAD)
            q = jnp.concatenate(
                [q_ref[0, sub * WBLK:(sub + 1) * WBLK, (kh * A_GROUP + g) * HEAD_DIM:(kh * A_GROUP + g + 1) * HEAD_DIM]
                 for g in range(A_GROUP)], axis=0)
            kb = jnp.concatenate([key_block((kp_ref, kc_ref, kn_ref), sub + t, cs) for t in range(3)]
                                 + [kx_ref[0, :, cs]], axis=0)
            vb = jnp.concatenate([key_block((vp_ref, vc_ref, vn_ref), sub + t, vs) for t in range(3)]
                                 + [vx_ref[0, :, vs]], axis=0)
            s = jnp.where(valid, lax.dot_general(q, kb, dn, preferred_element_type=F32), -jnp.inf)
            sink = jnp.zeros((rows, 1), F32)
            for g in range(A_GROUP):
                sink = jnp.where(head_of_row == g, sink_ref[kh * A_GROUP + g] * LOG2E, sink)
            m = jnp.maximum(jnp.max(s, axis=-1, keepdims=True), sink)
            acc = jnp.dot(jnp.exp2(s - m).astype(BF16), vb, preferred_element_type=F32)
            o = acc[:, :HEAD_DIM] / (acc[:, HEAD_DIM:HEAD_DIM + 1] + jnp.exp2(sink - m))
            for g in range(A_GROUP):
                hd = kh * A_GROUP + g
                o_ref[0, sub * WBLK:(sub + 1) * WBLK, hd * HEAD_DIM:(hd + 1) * HEAD_DIM] = (
                    o[g * WBLK:(g + 1) * WBLK].astype(BF16))


def _win_attn(sink, aq, ak, av, akc, avc):
    b, s, _ = aq.shape
    nb = s // WBLK
    c = akc.shape[1]
    tq = WIN_Q * WBLK
    prev = lambda w: pl.BlockSpec((1, WBLK, w), lambda i, j: (i, jnp.maximum(j * WIN_Q - 1, 0), 0))
    cur = lambda w: pl.BlockSpec((1, tq, w), lambda i, j: (i, j, 0))
    nxt = lambda w: pl.BlockSpec((1, WBLK, w), lambda i, j: (i, jnp.minimum((j + 1) * WIN_Q, nb - 1), 0))
    cx = lambda w: pl.BlockSpec((1, c, w), lambda i, j: (i, 0, 0))
    kw, vw = A_KV_HEADS * HEAD_DIM, A_KV_HEADS * V_PAD
    qo = pl.BlockSpec((1, tq, A_HEADS * HEAD_DIM), lambda i, j: (i, j, 0))
    return pl.pallas_call(
        functools.partial(_win_kernel, seq=s),
        grid=(b, nb // WIN_Q),
        in_specs=[pl.BlockSpec(memory_space=pltpu.SMEM), qo, prev(kw), cur(kw), nxt(kw), prev(vw), cur(vw), nxt(vw),
                  cx(kw), cx(vw)],
        out_specs=qo,
        out_shape=jax.ShapeDtypeStruct(aq.shape, BF16),
        compiler_params=_cparams(("arbitrary", "arbitrary")),
        name="win_attn",
    )(sink, aq, ak, ak, ak, av, av, av, akc, avc)


def _mla_kernel(q_ref, k_ref, v_ref, kx_ref, vx_ref, *rest, tk, n_side):
    side_in, o_ref, side_out = rest[:n_side], rest[n_side], rest[n_side + 1:2 * n_side + 1]
    s_a, s_b, acc_ref = rest[2 * n_side + 1:]
    for w_in, w_out in zip(side_in, side_out):
        w_out[...] = w_in[...].astype(BF16)
    q = q_ref[0]
    dn = (((1,), (1,)), ((), ()))
    n_chunks = k_ref.shape[1] // tk
    s_bufs = (s_a, s_b)

    def scores_into(buf, c):
        s = lax.dot_general(q, k_ref[0, c * tk:(c + 1) * tk, :], dn, preferred_element_type=F32)
        buf[...] = s
        return jnp.max(s, axis=-1, keepdims=True)

    s0 = lax.dot_general(q, kx_ref[0], dn, preferred_element_type=F32)
    m = jnp.max(s0, axis=-1, keepdims=True)
    acc_ref[...] = jnp.dot(jnp.exp2(s0 - m).astype(BF16), vx_ref[0], preferred_element_type=F32)
    mx = scores_into(s_bufs[0], 0)
    for c in range(n_chunks):
        if c + 1 < n_chunks:
            mx_next = scores_into(s_bufs[(c + 1) % 2], c + 1)
        m_new = jnp.maximum(m, mx)
        p = jnp.exp2(s_bufs[c % 2][...] - m_new).astype(BF16)
        acc_ref[...] = (jnp.exp2(m - m_new) * acc_ref[...]
                        + jnp.dot(p, v_ref[0, c * tk:(c + 1) * tk, :], preferred_element_type=F32))
        m, mx = m_new, mx_next
    acc = acc_ref[...]
    o_ref[0] = (acc[:, :V_DIM] / acc[:, V_DIM:V_DIM + 1]).astype(BF16)


def _mla_attn(bq, bk, bv, bkc, bvc, side, tq, tk):
    b, s, _ = bq.shape
    c = bkc.shape[1]
    nq = s // tq
    steps = b * B_HEADS * nq
    side2d = [w.reshape(-1, w.shape[-1]) for w in side]
    slab = lambda i, h, j: ((i * B_HEADS + h) * nq + j, 0)
    side_specs = [pl.BlockSpec((w.shape[0] // steps, w.shape[1]), slab) for w in side2d]
    res = pl.pallas_call(
        functools.partial(_mla_kernel, tk=tk, n_side=len(side)),
        grid=(b, B_HEADS, nq),
        in_specs=[
            pl.BlockSpec((1, tq, QK_PAD), lambda i, h, j: (i, j, h)),
            pl.BlockSpec((1, s, QK_PAD), lambda i, h, j: (i, 0, h)),
            pl.BlockSpec((1, s, V_PAD), lambda i, h, j: (i, 0, h)),
            pl.BlockSpec((1, c, QK_PAD), lambda i, h, j: (i, 0, h)),
            pl.BlockSpec((1, c, V_PAD), lambda i, h, j: (i, 0, h)),
        ] + side_specs,
        out_specs=[pl.BlockSpec((1, tq, V_DIM), lambda i, h, j: (i, j, h))] + side_specs,
        out_shape=[jax.ShapeDtypeStruct((b, s, B_HEADS * V_DIM), BF16)]
        + [jax.ShapeDtypeStruct(w.shape, BF16) for w in side2d],
        scratch_shapes=[pltpu.VMEM((tq, tk), F32), pltpu.VMEM((tq, tk), F32), pltpu.VMEM((tq, V_PAD), F32)],
        compiler_params=_cparams(("arbitrary", "arbitrary", "arbitrary")),
        name="mla_attn",
    )(bq, bk, bv, bkc, bvc, *side2d)
    return res[0], [o.reshape(w.shape) for o, w in zip(res[1:], side)]


def _route(sel, aff):
    scores = []
    for g in range(N_GROUPS):
        r = sel[g * EXP_PER_GROUP:(g + 1) * EXP_PER_GROUP]
        best = None
        for a in range(EXP_PER_GROUP):
            for b in range(a + 1, EXP_PER_GROUP):
                pair = r[a] + r[b]
                best = pair if best is None else jnp.maximum(best, pair)
        scores.append(best)
    top, grp = scores[0], jnp.zeros_like(scores[0], dtype=I32)
    for g in range(1, N_GROUPS):
        take = scores[g] > top
        grp = jnp.where(take, g, grp)
        top = jnp.where(take, scores[g], top)
    masked = [jnp.where(grp == e // EXP_PER_GROUP, sel[e], -jnp.inf) for e in range(N_EXPERTS)]

    def argmax_first(vals, skip=None):
        bv = jnp.full_like(vals[0], -jnp.inf)
        bi = jnp.full_like(grp, -1)
        for e in range(N_EXPERTS):
            take = vals[e] > bv
            if skip is not None:
                take = take & (skip != e)
            bi = jnp.where(take, e, bi)
            bv = jnp.where(take, vals[e], bv)
        return bi

    i0 = argmax_first(masked)
    i1 = argmax_first(masked, skip=i0)
    a0 = jnp.zeros_like(aff[0])
    a1 = jnp.zeros_like(aff[0])
    for e in range(N_EXPERTS):
        a0 = jnp.where(i0 == e, aff[e], a0)
        a1 = jnp.where(i1 == e, aff[e], a1)
    tot = a0 + a1
    return i0, i1, a0 / tot, a1 / tot


def _out_kernel(*refs, n_lhs):
    lhs = refs[:n_lhs]
    ws = refs[n_lhs:2 * n_lhs]
    bias_ref, x_ref, mod_ref, gn_ref, wr_ref, br_ref, x1_ref, hp_ref, idx_ref, wts_ref = refs[2 * n_lhs:]
    y = bias_ref[...]
    for a, w in zip(lhs, ws):
        y = y + jnp.dot(a[0], w[...], preferred_element_type=F32)
    x1 = x_ref[0] + mod_ref[0, 2:3, :] * y
    x1_ref[0] = x1
    h2 = (_rms(x1) * gn_ref[...]) * (1.0 + mod_ref[0, 4:5, :]) + mod_ref[0, 3:4, :]
    hp_ref[0] = h2
    w = wr_ref[...]
    w_head = w.astype(BF16)
    w_tail = (w - w_head.astype(F32)).astype(BF16)
    h_head = h2.astype(BF16)
    h_tail = (h2 - h_head.astype(F32)).astype(BF16)
    t = jnp.dot(h_head, jnp.concatenate([w_head, w_tail], axis=1), preferred_element_type=F32)
    logits = (t[:, :LANES] + t[:, LANES:]) + jnp.dot(h_tail, w_head, preferred_element_type=F32)
    lt = logits.T[:N_EXPERTS]
    aff_t = jax.nn.sigmoid(lt)
    sel_t = aff_t + br_ref[...]
    sel = [sel_t[e:e + 1] for e in range(N_EXPERTS)]
    aff = [aff_t[e:e + 1] for e in range(N_EXPERTS)]
    i0, i1, w0, w1 = _route(sel, aff)
    idx_ref[0] = jnp.concatenate([i0, i1], axis=0)
    w_rows = jnp.concatenate([w0, w1, jnp.zeros((LANES - TOP_K, w0.shape[1]), F32)], axis=0)
    wts_ref[0] = w_rows.T[:, :TOP_K]


def _out_proj(lhs, ws, bias, x, mod, gn, wr, br, tm):
    b, s, d = x.shape
    n_lhs = len(lhs)
    row = pl.BlockSpec((1, tm, d), lambda i, j: (i, j, 0))
    in_specs = ([pl.BlockSpec((1, tm, a.shape[-1]), lambda i, j: (i, j, 0)) for a in lhs]
                + [_resident(w.shape) for w in ws]
                + [_resident(bias.shape), row, pl.BlockSpec((1, 6, d), lambda i, j: (i, 0, 0)),
                   _resident(gn.shape), _resident(wr.shape), _resident(br.shape)])
    return pl.pallas_call(
        functools.partial(_out_kernel, n_lhs=n_lhs),
        grid=(b, s // tm),
        in_specs=in_specs,
        out_specs=[row, row,
                   pl.BlockSpec((1, TOP_K, tm), lambda i, j: (i, 0, j)),
                   pl.BlockSpec((1, tm, TOP_K), lambda i, j: (i, j, 0))],
        out_shape=[jax.ShapeDtypeStruct((b, s, d), F32), jax.ShapeDtypeStruct((b, s, d), F32),
                   jax.ShapeDtypeStruct((b, TOP_K, s), I32), jax.ShapeDtypeStruct((b, s, TOP_K), F32)],
        compiler_params=_cparams(("arbitrary", "arbitrary")),
        name="out_proj",
    )(*lhs, *ws, bias, x, mod, gn, wr, br)


def _dispatch(idx, blk):
    b, _, s = idx.shape
    n_asg = b * TOP_K * s
    assert N_EXPERTS * n_asg < 2 ** 31
    e = idx.reshape(n_asg)
    onehot = (e[None, :] == jnp.arange(N_EXPERTS, dtype=I32)[:, None]).astype(I32)
    csum = jnp.cumsum(onehot, axis=1)
    counts = csum[:, -1]
    rank = jnp.sum((csum - onehot) * onehot, axis=0)
    padded = (counts + blk - 1) // blk * blk
    pad_end = jnp.cumsum(padded)
    pad_start = pad_end - padded
    start = jnp.cumsum(counts) - counts
    dest = pad_start[e] + rank
    src_asg = jnp.sort(e * n_asg + jnp.arange(n_asg, dtype=I32)) % n_asg
    src_tok = src_asg // (TOP_K * s) * s + src_asg % s
    src_tok = jnp.concatenate([src_tok, jnp.zeros((blk,), src_tok.dtype)])
    n_blocks = -(-n_asg // blk) + N_EXPERTS
    n_used = pad_end[-1] // blk
    blk_ids = jnp.minimum(jnp.arange(n_blocks, dtype=I32), n_used - 1)
    blk_e = jnp.sum((blk_ids[:, None] * blk >= pad_end[None, :]).astype(I32), axis=1)
    blk_e = jnp.minimum(blk_e, N_EXPERTS - 1)
    blk_lo = start[blk_e] + blk_ids * blk - pad_start[blk_e]
    blk_rows = jnp.clip(start[blk_e] + counts[blk_e] - blk_lo, 0, blk)
    as_i32 = lambda a: a.astype(I32)
    return (as_i32(dest), as_i32(src_tok), as_i32(blk_e), as_i32(blk_lo), as_i32(blk_rows),
            as_i32(n_used.reshape(1)))


def _moe_kernel(blk_e_ref, lo_ref, rows_ref, n_used_ref, tok_ref, h_ref, wg_ref, wu_ref, wd_ref, o_ref, xbuf, sem,
                *, blk):
    i = pl.program_id(0)
    n_used = n_used_ref[0]
    slot = i % 2
    sizes = (blk // 4, blk // 2, blk)

    def gather(block, slot_):
        lo = lo_ref[block]
        rows = rows_ref[block]

        def row_copies(first, last):
            for r in range(first, last):
                t = tok_ref[lo + r]
                pltpu.make_async_copy(h_ref.at[pl.ds(t, 1)], xbuf.at[slot_, pl.ds(r, 1)], sem.at[slot_]).start()

        row_copies(0, sizes[0])
        for below, size in zip(sizes[:-1], sizes[1:]):
            @pl.when(rows > below)
            def _():
                row_copies(below, size)

    def ffn(rows):
        pltpu.make_async_copy(xbuf.at[slot, pl.ds(0, rows)], xbuf.at[slot, pl.ds(0, rows)], sem.at[slot]).wait()
        xb = xbuf[slot, :rows, :].astype(BF16)
        g = jnp.dot(xb, wg_ref[0, 0], preferred_element_type=F32)
        u = jnp.dot(xb, wu_ref[0, 0], preferred_element_type=F32)
        a = (g * jax.nn.sigmoid(g) * u).astype(BF16)
        o_ref[:rows, :] = jnp.dot(a, wd_ref[0, 0], preferred_element_type=F32)
        if rows < blk:
            o_ref[rows:, :] = jnp.zeros((blk - rows, o_ref.shape[1]), F32)

    @pl.when(i == 0)
    def _():
        gather(0, 0)

    @pl.when(i + 1 < n_used)
    def _():
        gather(i + 1, 1 - slot)

    used = i < n_used
    rows_i = rows_ref[i]
    for below, size in zip((0,) + sizes[:-1], sizes):
        @pl.when(used & (rows_i > below) & (rows_i <= size))
        def _():
            ffn(size)

    @pl.when(jnp.logical_not(used))
    def _():
        o_ref[...] = jnp.zeros_like(o_ref)


def _moe(h2, src_tok, blk_e, blk_lo, blk_rows, n_used, wg, wu, wd, layer, blk):
    n_blocks = blk_e.shape[0]
    _, _, d, de = wg.shape
    grid_spec = pltpu.PrefetchScalarGridSpec(
        num_scalar_prefetch=5,
        grid=(n_blocks,),
        in_specs=[
            pl.BlockSpec(memory_space=pl.ANY),
            pl.BlockSpec((1, 1, d, de), lambda i, be, *_: (layer, be[i], 0, 0)),
            pl.BlockSpec((1, 1, d, de), lambda i, be, *_: (layer, be[i], 0, 0)),
            pl.BlockSpec((1, 1, de, d), lambda i, be, *_: (layer, be[i], 0, 0)),
        ],
        out_specs=pl.BlockSpec((blk, d), lambda i, *_: (i, 0)),
        scratch_shapes=[pltpu.VMEM((2, blk, d), F32), pltpu.SemaphoreType.DMA((2,))],
    )
    return pl.pallas_call(
        functools.partial(_moe_kernel, blk=blk),
        grid_spec=grid_spec,
        out_shape=jax.ShapeDtypeStruct((n_blocks * blk, d), F32),
        compiler_params=_cparams(("arbitrary",)),
        name="moe_ffn",
    )(blk_e, blk_lo, blk_rows, n_used, src_tok, h2, wg, wu, wd)


def _combine_kernel(dest_ref, yp_ref, x_ref, wt_ref, mod_ref, *rest, tm, nt, with_norm):
    if with_norm:
        gn_ref, modn_ref, o_ref, hn_ref, buf, sem = rest
    else:
        o_ref, buf, sem = rest
    i = pl.program_id(0)
    j = pl.program_id(1)
    step = i * nt + j
    slot = step % 2

    def gather(step_, slot_):
        bases = [((step_ // nt) * TOP_K + k) * (nt * tm) + (step_ % nt) * tm for k in range(TOP_K)]
        for r in range(tm):
            for k in range(TOP_K):
                pltpu.make_async_copy(yp_ref.at[pl.ds(dest_ref[bases[k] + r], 1)],
                                      buf.at[slot_, k, pl.ds(r, 1)], sem.at[slot_]).start()

    @pl.when(step == 0)
    def _():
        gather(0, 0)

    @pl.when(step + 1 < pl.num_programs(0) * nt)
    def _():
        gather(step + 1, 1 - slot)

    pltpu.make_async_copy(buf.at[slot], buf.at[slot], sem.at[slot]).wait()
    w = wt_ref[0]
    y = buf[slot, 0] * w[:, 0:1] + buf[slot, 1] * w[:, 1:2]
    out = x_ref[0] + mod_ref[0, 5:6, :] * y
    o_ref[0] = out
    if with_norm:
        hn_ref[0] = ((_rms(out) * gn_ref[...]) * (1.0 + modn_ref[0, 1:2, :]) + modn_ref[0, 0:1, :]).astype(BF16)


def _combine(dest, yp, x1, wts, mod, tm, norm=None):
    b, s, d = x1.shape
    nt = s // tm
    with_norm = norm is not None
    row = pl.BlockSpec((1, tm, d), lambda i, j, ds: (i, j, 0))
    modspec = pl.BlockSpec((1, 6, d), lambda i, j, ds: (i, 0, 0))
    in_specs = [pl.BlockSpec(memory_space=pl.ANY), row,
                pl.BlockSpec((1, tm, TOP_K), lambda i, j, ds: (i, j, 0)), modspec]
    args = [yp, x1, wts, mod]
    out_shape = [jax.ShapeDtypeStruct((b, s, d), F32)]
    out_specs = [row]
    if with_norm:
        gn, modn = norm
        in_specs += [pl.BlockSpec(gn.shape, lambda i, j, ds: (0, 0)), modspec]
        args += [gn, modn]
        out_shape.append(jax.ShapeDtypeStruct((b, s, d), BF16))
        out_specs.append(row)
    grid_spec = pltpu.PrefetchScalarGridSpec(
        num_scalar_prefetch=1, grid=(b, nt), in_specs=in_specs, out_specs=out_specs,
        scratch_shapes=[pltpu.VMEM((2, TOP_K, tm, d), F32), pltpu.SemaphoreType.DMA((2,))])
    return pl.pallas_call(
        functools.partial(_combine_kernel, tm=tm, nt=nt, with_norm=with_norm),
        grid_spec=grid_spec,
        out_shape=out_shape,
        compiler_params=_cparams(("arbitrary", "arbitrary")),
        name="combine",
    )(dest, *args)


def _dft_constants(seq):
    n1, n2 = DFT_N1, DFT_N2
    assert seq == n1 * n2 and n2 == n1 * n1
    c = np.arange(F_GROUP_DIM)
    ang = 2 * np.pi * np.outer(c, c) / F_GROUP_DIM
    fc = np.concatenate([np.cos(ang), -np.sin(ang)], axis=1)
    a, k2, m = np.meshgrid(np.arange(n1), np.arange(n2), np.arange(n2), indexing="ij")
    ang_a = -2 * np.pi * (k2 * (a + n1 * m) % seq) / seq
    tre, tim = np.cos(ang_a), np.sin(ang_a)
    ma = np.concatenate([np.concatenate([tre, -tim], axis=2), np.concatenate([tim, tre], axis=2)], axis=1)
    ang_b = -2 * np.pi * np.outer(np.arange(n1), np.arange(n1)) / n1
    eye = np.eye(n1)
    mb = np.concatenate([np.kron(np.cos(ang_b), eye), -np.kron(np.sin(ang_b), eye)], axis=1)
    mb = mb / math.sqrt(seq * F_GROUP_DIM)
    return tuple(jnp.asarray(t, F32).astype(BF16) for t in (fc, ma, mb))


def _fourier_kernel(h_ref, fc_ref, ma_ref, mb_ref, o_ref, z_ref, yre, yim):
    n1, n2, gd = DFT_N1, DFT_N2, F_GROUP_DIM
    rows = FOURIER_ROWS
    n_tiles = 2 * gd // LANES
    for r in range(0, h_ref.shape[1], rows):
        z = jnp.dot(h_ref[0, r:r + rows, :], fc_ref[...], preferred_element_type=F32)
        for t in range(n_tiles):
            z_ref[t, r:r + rows, :] = z[:, t * LANES:(t + 1) * LANES]
    for a in range(n1):
        zs = [z_ref[t, pl.ds(a, n2, stride=n1), :].astype(BF16) for t in range(n_tiles)]
        rhs = jnp.concatenate([jnp.concatenate(zs[:n_tiles // 2], axis=1),
                               jnp.concatenate(zs[n_tiles // 2:], axis=1)], axis=0)
        y = jnp.dot(ma_ref[a], rhs, preferred_element_type=F32).astype(BF16)
        yre[a] = y[:n2]
        yim[a] = y[n2:]
    for hi in range(n1):
        rhs = jnp.concatenate([yre[a, hi * n1:(hi + 1) * n1, :] for a in range(n1)]
                              + [yim[a, hi * n1:(hi + 1) * n1, :] for a in range(n1)], axis=0)
        out = jnp.dot(mb_ref[...], rhs, preferred_element_type=F32).astype(BF16)
        for k1 in range(n1):
            o_ref[0, k1 * n2 + hi * n1:k1 * n2 + (hi + 1) * n1, :] = out[k1 * n1:(k1 + 1) * n1]


def _fourier(hn, consts):
    b, s, d = hn.shape
    fc, ma, mb = consts
    gd = F_GROUP_DIM
    return pl.pallas_call(
        _fourier_kernel,
        grid=(b, F_GROUPS),
        in_specs=[pl.BlockSpec((1, s, gd), lambda i, g: (i, 0, g)),
                  _resident(fc.shape), _resident(ma.shape), _resident(mb.shape)],
        out_specs=pl.BlockSpec((1, s, gd), lambda i, g: (i, 0, g)),
        out_shape=jax.ShapeDtypeStruct((b, s, d), BF16),
        scratch_shapes=[pltpu.VMEM((2 * gd // LANES, s, LANES), F32),
                        pltpu.VMEM((DFT_N1, DFT_N2, gd), BF16), pltpu.VMEM((DFT_N1, DFT_N2, gd), BF16)],
        compiler_params=_cparams(("arbitrary", "arbitrary")),
        name="fourier",
    )(hn, fc, ma, mb)


def _rope_tables(seq):
    pos = np.arange(seq)
    row, col = (pos // GRID_W).astype(np.float64), (pos % GRID_W).astype(np.float64)

    def tables(width):
        half = width // 2
        quarter = half // 2
        freqs = ROPE_BASE ** (-np.arange(0, half, 2, dtype=np.float64) / half)
        lane = np.arange(LANES)
        ang = np.where((lane < half)[None, :], row[:, None], col[:, None]) * freqs[lane % quarter][None, :]
        live = (lane < width)[None, :]
        first = ((lane % half) < quarter)[None, :]
        cos = np.where(live, np.cos(ang), 1.0)
        sneg = np.where(live & first, -np.sin(ang), 0.0)
        spos = np.where(live & ~first, np.sin(ang), 0.0)
        return [jnp.asarray(t, F32) for t in (cos, sneg, spos)]

    return tables(HEAD_DIM) + tables(ROPE_DIM)


def _identity_tables(n):
    one, zero = jnp.ones((n, LANES), F32), jnp.zeros((n, LANES), F32)
    return [one, zero, zero, one, zero, zero]


def _pad_heads(w, lead):
    w = w.reshape(lead, B_HEADS, QK_DIM)
    return jnp.pad(w, ((0, 0), (0, 0), (0, QK_PAD - QK_DIM))).reshape(lead, B_HEADS * QK_PAD)


def kernel(x, c, ctx, c_ctx, w_ada, b_ada, g_norm, w_in, g_aqn, g_akn, g_bq_lat, w_bq_up, g_bkv_lat, w_bkv_up,
           g_bqn, g_bkn, sink, w_o_ab, w_fo, b_fo, w_router, b_router, w_gate, w_up, w_down):
    b, s, d = x.shape
    n_ctx = ctx.shape[1]

    crows = jnp.concatenate([c, c_ctx[None, :], jnp.zeros((8 - b - 1, d), F32)], axis=0)
    mods, (win, wo, wfo) = _ada(crows, w_ada, b_ada, (w_in, w_o_ab, w_fo), (IN_PAD, d, d))
    mods = mods.reshape(DEPTH, 8, 6, d)
    mod_lat = [mods[l, :b] for l in range(DEPTH)]
    mod_ctx = jnp.broadcast_to(mods[0, b][None], (b, 6, d))

    wr = jnp.pad(w_router, ((0, 0), (0, LANES - N_EXPERTS)))
    br = b_router.reshape(N_EXPERTS, 1)

    wbq = _pad_heads(w_bq_up[0], Q_LORA).astype(BF16)
    wkv = w_bkv_up[0].reshape(KV_LORA, B_HEADS, NOPE_DIM + V_DIM)
    wbkv = jnp.concatenate([wkv[:, :, :NOPE_DIM].reshape(KV_LORA, -1), wkv[:, :, NOPE_DIM:].reshape(KV_LORA, -1)],
                           axis=1).astype(BF16)
    gains = (g_aqn[0][None], g_akn[0][None], g_bq_lat[0][None], g_bkv_lat[0][None],
             jnp.pad(g_bqn[0][None], ((0, 0), (0, QK_PAD - QK_DIM))),
             jnp.pad(g_bkn[0][None], ((0, 0), (0, QK_PAD - QK_DIM))))
    gn0 = g_norm[0, 0][None]
    aq, ak, av, bq, bk, bv = _proj(x, mod_lat[0], gn0, win, wbq, wbkv, *gains, _rope_tables(s), PROJ_TM)
    _, akc, avc, _, bkc, bvc = _proj(ctx, mod_ctx, gn0, win, wbq, wbkv, *gains, _identity_tables(n_ctx), n_ctx)

    ya = _win_attn(sink[0], aq, ak, av, akc, avc)
    yb, (wg, wu, wd) = _mla_attn(bq, bk, bv, bkc, bvc, (w_gate, w_up, w_down), MLA_TQ, MLA_TK)

    n_a = A_HEADS * HEAD_DIM
    x1, h2, idx, wts = _out_proj([ya, yb], [wo[:n_a], wo[n_a:]], jnp.zeros((1, d), F32), x, mod_lat[0],
                                 g_norm[0, 1][None], wr, br, OUT_TM)
    dest, *plan = _dispatch(idx, MOE_BLK)
    yp = _moe(h2.reshape(b * s, d), *plan, wg, wu, wd, 0, MOE_BLK)
    x2, hn = _combine(dest, yp, x1, wts, mod_lat[0], COMBINE_TM, norm=(g_norm[1, 0][None], mod_lat[1]))

    f = _fourier(hn, _dft_constants(s))
    x3, h2, idx, wts = _out_proj([f], [wfo], b_fo[0][None], x2, mod_lat[1],
                                 g_norm[1, 1][None], wr, br, OUT_TM)
    dest, *plan = _dispatch(idx, MOE_BLK)
    yp = _moe(h2.reshape(b * s, d), *plan, wg, wu, wd, 1, MOE_BLK)
    (x4,) = _combine(dest, yp, x3, wts, mod_lat[1], COMBINE_TM)
    return x4
```

```python
import functools
import math

import numpy as np
import jax
import jax.numpy as jnp
from jax import lax
from jax.experimental import pallas as pl
from jax.experimental.pallas import tpu as pltpu

F32 = jnp.float32
BF16 = jnp.bfloat16
I32 = jnp.int32

D_MODEL = 2048
DEPTH = 2
GRID_W = 64
HEAD_DIM = 128
A_HEADS = 8
A_KV_HEADS = 2
A_GROUP = A_HEADS // A_KV_HEADS
WINDOW = 128
WBLK = 128
B_HEADS = 8
Q_LORA = 512
KV_LORA = 256
NOPE_DIM = 128
ROPE_DIM = 64
V_DIM = 128
QK_DIM = NOPE_DIM + ROPE_DIM
QK_PAD = 256
V_PAD = 256
IN_SPLITS = (A_HEADS * HEAD_DIM, A_KV_HEADS * HEAD_DIM, A_KV_HEADS * HEAD_DIM, Q_LORA, KV_LORA, ROPE_DIM)
IN_WIDTH = sum(IN_SPLITS)
IN_PAD = 2432
F_GROUPS = 8
F_GROUP_DIM = D_MODEL // F_GROUPS
N_EXPERTS = 16
N_GROUPS = 4
EXP_PER_GROUP = N_EXPERTS // N_GROUPS
TOP_K = 2
D_EXPERT = 1024
ROPE_BASE = 10000.0
EPS = 1e-6
LOG2E = math.log2(math.e)
LANES = 128

MOE_BLK = 512
WIN_Q = 8
PROJ_TM = 256
OUT_TM = 512
COMBINE_TM = 256
MLA_TQ = 1024
MLA_TK = 2048
ADA_TN = 1024
FOURIER_ROWS = 512
DFT_N1 = 16
DFT_N2 = 256
VMEM_LIMIT = 56 * 1024 * 1024


def _cparams(sem, **kw):
    return pltpu.CompilerParams(dimension_semantics=sem, vmem_limit_bytes=VMEM_LIMIT, **kw)


def _resident(shape):
    nd = len(shape)
    return pl.BlockSpec(shape, lambda *_: (0,) * nd, pipeline_mode=pl.Buffered(1))


def _rms(t, width=None):
    n = t.shape[-1] if width is None else width
    ss = jnp.sum(t * t, axis=-1, keepdims=True)
    return t * lax.rsqrt(ss * (1.0 / n) + EPS)


def _rope(t, cos, sneg, spos, dist):
    n = t.shape[-1]
    return t * cos + pltpu.roll(t, n - dist, 1) * sneg + pltpu.roll(t, dist, 1) * spos


ADA_SIDE_STEPS = 16


def _ada_kernel(c_ref, w_ref, b_ref, *rest, n_side, nj):
    side_in, o_ref, side_out = rest[:n_side], rest[n_side], rest[n_side + 1:]
    c = c_ref[...]
    s = c * jax.nn.sigmoid(c)
    w = w_ref[0]
    w_head = w.astype(BF16)
    w_tail = (w - w_head.astype(F32)).astype(BF16)
    s_head = s.astype(BF16)
    s_tail = (s - s_head.astype(F32)).astype(BF16)
    r = jnp.dot(jnp.concatenate([s_head, s_tail], axis=0), w_head, preferred_element_type=F32)
    rows = s.shape[0]
    o_ref[0] = (r[:rows] + r[rows:]) + jnp.dot(s_head, w_tail, preferred_element_type=F32) + b_ref[0]

    @pl.when(pl.program_id(0) * nj + pl.program_id(1) < ADA_SIDE_STEPS)
    def _():
        for w_in, w_out in zip(side_in, side_out):
            width = w_in.shape[1]
            w_out[:, :width] = w_in[...].astype(BF16)
            if w_out.shape[1] > width:
                w_out[:, width:] = jnp.zeros((w_out.shape[0], w_out.shape[1] - width), BF16)


def _ada(crows, w_ada, b_ada, side, side_widths):
    depth, d, n = w_ada.shape
    tn = ADA_TN
    nj = n // tn
    assert depth * nj >= ADA_SIDE_STEPS
    slab = lambda l, j: (jnp.minimum(l * nj + j, ADA_SIDE_STEPS - 1), 0)
    slab3 = lambda l, j: (0,) + slab(l, j)
    assert all(w.ndim == 3 and w.shape[0] == 1 for w in side)
    side_in = [pl.BlockSpec((None, w.shape[1] // ADA_SIDE_STEPS, w.shape[2]), slab3) for w in side]
    side_out = [pl.BlockSpec((w.shape[1] // ADA_SIDE_STEPS, wd), slab) for w, wd in zip(side, side_widths)]
    res = pl.pallas_call(
        functools.partial(_ada_kernel, n_side=len(side), nj=nj),
        grid=(depth, nj),
        in_specs=[
            pl.BlockSpec((8, d), lambda l, j: (0, 0)),
            pl.BlockSpec((1, d, tn), lambda l, j: (l, 0, j)),
            pl.BlockSpec((1, 1, tn), lambda l, j: (l, 0, j)),
        ] + side_in,
        out_specs=[pl.BlockSpec((1, 8, tn), lambda l, j: (l, 0, j))] + side_out,
        out_shape=[jax.ShapeDtypeStruct((depth, 8, n), F32)]
        + [jax.ShapeDtypeStruct((w.shape[1], wd), BF16) for w, wd in zip(side, side_widths)],
        compiler_params=_cparams(("arbitrary", "arbitrary")),
        name="ada",
    )(crows, w_ada, b_ada.reshape(depth, 1, n), *side)
    return res[0], res[1:]


def _proj_kernel(x_ref, mod_ref, gn_ref, win_ref, wbq_ref, wbkv_ref, gaq_ref, gak_ref, gbql_ref, gbkvl_ref,
                 gbq_ref, gbk_ref, ca_ref, sna_ref, spa_ref, cb_ref, snb_ref, spb_ref,
                 aq_ref, ak_ref, av_ref, bq_ref, bk_ref, bv_ref):
    x = x_ref[0]
    shift = mod_ref[0, 0:1, :]
    scale = mod_ref[0, 1:2, :]
    hb = ((_rms(x) * gn_ref[...]) * (1.0 + scale) + shift).astype(BF16)

    def cols(w_ref, lhs, lo, width):
        return jnp.dot(lhs, w_ref[:, lo:lo + width], preferred_element_type=F32)

    ca, sna, spa = ca_ref[...], sna_ref[...], spa_ref[...]
    cb, snb, spb = cb_ref[...], snb_ref[...], spb_ref[...]
    a_scale = HEAD_DIM ** -0.5 * LOG2E
    b_scale = QK_DIM ** -0.5 * LOG2E
    pair = 2 * HEAD_DIM

    for hp in range(A_HEADS // 2):
        pp = cols(win_ref, hb, hp * pair, pair)
        for j in range(2):
            hd = 2 * hp + j
            t = _rms(pp[:, j * HEAD_DIM:(j + 1) * HEAD_DIM]) * gaq_ref[...]
            aq_ref[0, :, hd * HEAD_DIM:(hd + 1) * HEAD_DIM] = (_rope(t, ca, sna, spa, HEAD_DIM // 4) * a_scale).astype(BF16)
    off = A_HEADS * HEAD_DIM
    pp = cols(win_ref, hb, off, A_KV_HEADS * HEAD_DIM)
    for kh in range(A_KV_HEADS):
        t = _rms(pp[:, kh * HEAD_DIM:(kh + 1) * HEAD_DIM]) * gak_ref[...]
        ak_ref[0, :, kh * HEAD_DIM:(kh + 1) * HEAD_DIM] = _rope(t, ca, sna, spa, HEAD_DIM // 4).astype(BF16)
    off += A_KV_HEADS * HEAD_DIM
    ones_col = (lax.broadcasted_iota(I32, (hb.shape[0], V_PAD - V_DIM), 1) == 0).astype(BF16)
    pp = cols(win_ref, hb, off, A_KV_HEADS * HEAD_DIM)
    for kh in range(A_KV_HEADS):
        av_ref[0, :, kh * V_PAD:kh * V_PAD + HEAD_DIM] = pp[:, kh * HEAD_DIM:(kh + 1) * HEAD_DIM].astype(BF16)
        av_ref[0, :, kh * V_PAD + HEAD_DIM:(kh + 1) * V_PAD] = ones_col
    off += A_KV_HEADS * HEAD_DIM

    ql = (_rms(cols(win_ref, hb, off, Q_LORA)) * gbql_ref[...]).astype(BF16)
    off += Q_LORA
    for hd in range(B_HEADS):
        t = _rms(cols(wbq_ref, ql, hd * QK_PAD, QK_PAD), QK_DIM) * gbq_ref[...]
        bq_ref[0, :, hd * QK_PAD:hd * QK_PAD + NOPE_DIM] = (t[:, :NOPE_DIM] * b_scale).astype(BF16)
        bq_ref[0, :, hd * QK_PAD + NOPE_DIM:(hd + 1) * QK_PAD] = (
            _rope(t[:, NOPE_DIM:], cb, snb, spb, ROPE_DIM // 4) * b_scale).astype(BF16)

    kvl = (_rms(cols(win_ref, hb, off, KV_LORA)) * gbkvl_ref[...]).astype(BF16)
    off += KV_LORA
    kr = cols(win_ref, hb, off, LANES)
    kr_ss = jnp.sum(kr * kr, axis=-1, keepdims=True)
    kr_rot = _rope(kr * gbk_ref[:, NOPE_DIM:], cb, snb, spb, ROPE_DIM // 4)
    for hp in range(B_HEADS // 2):
        kn2 = cols(wbkv_ref, kvl, hp * pair, pair)
        v2 = cols(wbkv_ref, kvl, B_HEADS * NOPE_DIM + hp * pair, pair)
        for j in range(2):
            hd = 2 * hp + j
            kn = kn2[:, j * NOPE_DIM:(j + 1) * NOPE_DIM]
            ss = jnp.sum(kn * kn, axis=-1, keepdims=True) + kr_ss
            r = lax.rsqrt(ss * (1.0 / QK_DIM) + EPS)
            bk_ref[0, :, hd * QK_PAD:hd * QK_PAD + NOPE_DIM] = (kn * r * gbk_ref[:, :NOPE_DIM]).astype(BF16)
            bk_ref[0, :, hd * QK_PAD + NOPE_DIM:(hd + 1) * QK_PAD] = (kr_rot * r).astype(BF16)
            bv_ref[0, :, hd * V_PAD:hd * V_PAD + V_DIM] = v2[:, j * V_DIM:(j + 1) * V_DIM].astype(BF16)
            bv_ref[0, :, hd * V_PAD + V_DIM:(hd + 1) * V_PAD] = ones_col


def _proj(x, mod, gn, win, wbq, wbkv, gaq, gak, gbql, gbkvl, gbq, gbk, tabs, tm):
    b, s, d = x.shape
    row = lambda w: pl.BlockSpec((1, tm, w), lambda i, j: (i, j, 0))
    tab = pl.BlockSpec((tm, LANES), lambda i, j: (j, 0))
    widths = (A_HEADS * HEAD_DIM, A_KV_HEADS * HEAD_DIM, A_KV_HEADS * V_PAD,
              B_HEADS * QK_PAD, B_HEADS * QK_PAD, B_HEADS * V_PAD)
    return pl.pallas_call(
        _proj_kernel,
        grid=(b, s // tm),
        in_specs=[row(d), pl.BlockSpec((1, 6, d), lambda i, j: (i, 0, 0)), _resident(gn.shape),
                  _resident(win.shape), _resident(wbq.shape), _resident(wbkv.shape),
                  _resident(gaq.shape), _resident(gak.shape), _resident(gbql.shape), _resident(gbkvl.shape),
                  _resident(gbq.shape), _resident(gbk.shape)] + [tab] * 6,
        out_specs=[row(w) for w in widths],
        out_shape=[jax.ShapeDtypeStruct((b, s, w), BF16) for w in widths],
        compiler_params=_cparams(("arbitrary", "arbitrary")),
        name="proj",
    )(x, mod, gn, win, wbq, wbkv, gaq, gak, gbql, gbkvl, gbq, gbk, *tabs)


def _win_kernel(sink_ref, q_ref, kp_ref, kc_ref, kn_ref, vp_ref, vc_ref, vn_ref, kx_ref, vx_ref, o_ref, *, seq):
    j = pl.program_id(1)
    rows = A_GROUP * WBLK
    band = 3 * WBLK
    keys = band + kx_ref.shape[1]
    r_iota = lax.broadcasted_iota(I32, (rows, keys), 0)
    c_iota = lax.broadcasted_iota(I32, (rows, keys), 1)
    head_of_row = lax.broadcasted_iota(I32, (rows, 1), 0) // WBLK
    dn = (((1,), (1,)), ((), ()))

    def key_block(refs, t, cols):
        p_ref, c_ref, n_ref = refs
        if t == 0:
            return p_ref[0, :, cols]
        if t == WIN_Q + 1:
            return n_ref[0, :, cols]
        return c_ref[0, (t - 1) * WBLK:t * WBLK, cols]

    for sub in range(WIN_Q):
        n = j * WIN_Q + sub
        qpos = n * WBLK + (r_iota & (WBLK - 1))
        kpos = (n - 1) * WBLK + c_iota
        valid = ((jnp.abs(qpos - kpos) <= WINDOW) & (kpos >= 0) & (kpos < seq)) | (c_iota >= band)
        for kh in range(A_KV_HEADS):
            cs = slice(kh * HEAD_DIM, (kh + 1) * HEAD_DIM)
            vs = slice(kh * V_PAD, (kh + 1) * V_PAD)
            q = jnp.concatenate(
                [q_ref[0, sub * WBLK:(sub + 1) * WBLK, (kh * A_GROUP + g) * HEAD_DIM:(kh * A_GROUP + g + 1) * HEAD_DIM]
                 for g in range(A_GROUP)], axis=0)
            kb = jnp.concatenate([key_block((kp_ref, kc_ref, kn_ref), sub + t, cs) for t in range(3)]
                                 + [kx_ref[0, :, cs]], axis=0)
            vb = jnp.concatenate([key_block((vp_ref, vc_ref, vn_ref), sub + t, vs) for t in range(3)]
                                 + [vx_ref[0, :, vs]], axis=0)
            s = jnp.where(valid, lax.dot_general(q, kb, dn, preferred_element_type=F32), -jnp.inf)
            sink = jnp.zeros((rows, 1), F32)
            for g in range(A_GROUP):
                sink = jnp.where(head_of_row == g, sink_ref[kh * A_GROUP + g] * LOG2E, sink)
            m = jnp.maximum(jnp.max(s, axis=-1, keepdims=True), sink)
            acc = jnp.dot(jnp.exp2(s - m).astype(BF16), vb, preferred_element_type=F32)
            o = acc[:, :HEAD_DIM] / (acc[:, HEAD_DIM:HEAD_DIM + 1] + jnp.exp2(sink - m))
            for g in range(A_GROUP):
                hd = kh * A_GROUP + g
                o_ref[0, sub * WBLK:(sub + 1) * WBLK, hd * HEAD_DIM:(hd + 1) * HEAD_DIM] = (
                    o[g * WBLK:(g + 1) * WBLK].astype(BF16))


def _win_attn(sink, aq, ak, av, akc, avc):
    b, s, _ = aq.shape
    nb = s // WBLK
    c = akc.shape[1]
    tq = WIN_Q * WBLK
    prev = lambda w: pl.BlockSpec((1, WBLK, w), lambda i, j: (i, jnp.maximum(j * WIN_Q - 1, 0), 0))
    cur = lambda w: pl.BlockSpec((1, tq, w), lambda i, j: (i, j, 0))
    nxt = lambda w: pl.BlockSpec((1, WBLK, w), lambda i, j: (i, jnp.minimum((j + 1) * WIN_Q, nb - 1), 0))
    cx = lambda w: pl.BlockSpec((1, c, w), lambda i, j: (i, 0, 0))
    kw, vw = A_KV_HEADS * HEAD_DIM, A_KV_HEADS * V_PAD
    qo = pl.BlockSpec((1, tq, A_HEADS * HEAD_DIM), lambda i, j: (i, j, 0))
    return pl.pallas_call(
        functools.partial(_win_kernel, seq=s),
        grid=(b, nb // WIN_Q),
        in_specs=[pl.BlockSpec(memory_space=pltpu.SMEM), qo, prev(kw), cur(kw), nxt(kw), prev(vw), cur(vw), nxt(vw),
                  cx(kw), cx(vw)],
        out_specs=qo,
        out_shape=jax.ShapeDtypeStruct(aq.shape, BF16),
        compiler_params=_cparams(("arbitrary", "arbitrary")),
        name="win_attn",
    )(sink, aq, ak, ak, ak, av, av, av, akc, avc)


def _mla_kernel(q_ref, k_ref, v_ref, kx_ref, vx_ref, *rest, tk, n_side):
    side_in, o_ref, side_out = rest[:n_side], rest[n_side], rest[n_side + 1:2 * n_side + 1]
    s_a, s_b, acc_ref = rest[2 * n_side + 1:]
    for w_in, w_out in zip(side_in, side_out):
        w_out[...] = w_in[...].astype(BF16)
    q = q_ref[0]
    dn = (((1,), (1,)), ((), ()))
    n_chunks = k_ref.shape[1] // tk
    s_bufs = (s_a, s_b)

    def scores_into(buf, c):
        s = lax.dot_general(q, k_ref[0, c * tk:(c + 1) * tk, :], dn, preferred_element_type=F32)
        buf[...] = s
        return jnp.max(s, axis=-1, keepdims=True)

    s0 = lax.dot_general(q, kx_ref[0], dn, preferred_element_type=F32)
    m = jnp.max(s0, axis=-1, keepdims=True)
    acc_ref[...] = jnp.dot(jnp.exp2(s0 - m).astype(BF16), vx_ref[0], preferred_element_type=F32)
    mx = scores_into(s_bufs[0], 0)
    for c in range(n_chunks):
        if c + 1 < n_chunks:
            mx_next = scores_into(s_bufs[(c + 1) % 2], c + 1)
        m_new = jnp.maximum(m, mx)
        p = jnp.exp2(s_bufs[c % 2][...] - m_new).astype(BF16)
        acc_ref[...] = (jnp.exp2(m - m_new) * acc_ref[...]
                        + jnp.dot(p, v_ref[0, c * tk:(c + 1) * tk, :], preferred_element_type=F32))
        m, mx = m_new, mx_next
    acc = acc_ref[...]
    o_ref[0] = (acc[:, :V_DIM] / acc[:, V_DIM:V_DIM + 1]).astype(BF16)


def _mla_attn(bq, bk, bv, bkc, bvc, side, tq, tk):
    b, s, _ = bq.shape
    c = bkc.shape[1]
    nq = s // tq
    steps = b * B_HEADS * nq
    side2d = [w.reshape(-1, w.shape[-1]) for w in side]
    slab = lambda i, h, j: ((i * B_HEADS + h) * nq + j, 0)
    side_specs = [pl.BlockSpec((w.shape[0] // steps, w.shape[1]), slab) for w in side2d]
    res = pl.pallas_call(
        functools.partial(_mla_kernel, tk=tk, n_side=len(side)),
        grid=(b, B_HEADS, nq),
        in_specs=[
            pl.BlockSpec((1, tq, QK_PAD), lambda i, h, j: (i, j, h)),
            pl.BlockSpec((1, s, QK_PAD), lambda i, h, j: (i, 0, h)),
            pl.BlockSpec((1, s, V_PAD), lambda i, h, j: (i, 0, h)),
            pl.BlockSpec((1, c, QK_PAD), lambda i, h, j: (i, 0, h)),
            pl.BlockSpec((1, c, V_PAD), lambda i, h, j: (i, 0, h)),
        ] + side_specs,
        out_specs=[pl.BlockSpec((1, tq, V_DIM), lambda i, h, j: (i, j, h))] + side_specs,
        out_shape=[jax.ShapeDtypeStruct((b, s, B_HEADS * V_DIM), BF16)]
        + [jax.ShapeDtypeStruct(w.shape, BF16) for w in side2d],
        scratch_shapes=[pltpu.VMEM((tq, tk), F32), pltpu.VMEM((tq, tk), F32), pltpu.VMEM((tq, V_PAD), F32)],
        compiler_params=_cparams(("arbitrary", "arbitrary", "arbitrary")),
        name="mla_attn",
    )(bq, bk, bv, bkc, bvc, *side2d)
    return res[0], [o.reshape(w.shape) for o, w in zip(res[1:], side)]


def _route(sel, aff):
    scores = []
    for g in range(N_GROUPS):
        r = sel[g * EXP_PER_GROUP:(g + 1) * EXP_PER_GROUP]
        best = None
        for a in range(EXP_PER_GROUP):
            for b in range(a + 1, EXP_PER_GROUP):
                pair = r[a] + r[b]
                best = pair if best is None else jnp.maximum(best, pair)
        scores.append(best)
    top, grp = scores[0], jnp.zeros_like(scores[0], dtype=I32)
    for g in range(1, N_GROUPS):
        take = scores[g] > top
        grp = jnp.where(take, g, grp)
        top = jnp.where(take, scores[g], top)
    masked = [jnp.where(grp == e // EXP_PER_GROUP, sel[e], -jnp.inf) for e in range(N_EXPERTS)]

    def argmax_first(vals, skip=None):
        bv = jnp.full_like(vals[0], -jnp.inf)
        bi = jnp.full_like(grp, -1)
        for e in range(N_EXPERTS):
            take = vals[e] > bv
            if skip is not None:
                take = take & (skip != e)
            bi = jnp.where(take, e, bi)
            bv = jnp.where(take, vals[e], bv)
        return bi

    i0 = argmax_first(masked)
    i1 = argmax_first(masked, skip=i0)
    a0 = jnp.zeros_like(aff[0])
    a1 = jnp.zeros_like(aff[0])
    for e in range(N_EXPERTS):
        a0 = jnp.where(i0 == e, aff[e], a0)
        a1 = jnp.where(i1 == e, aff[e], a1)
    tot = a0 + a1
    return i0, i1, a0 / tot, a1 / tot


def _out_kernel(*refs, n_lhs):
    lhs = refs[:n_lhs]
    ws = refs[n_lhs:2 * n_lhs]
    bias_ref, x_ref, mod_ref, gn_ref, wr_ref, br_ref, x1_ref, hp_ref, idx_ref, wts_ref = refs[2 * n_lhs:]
    y = bias_ref[...]
    for a, w in zip(lhs, ws):
        y = y + jnp.dot(a[0], w[...], preferred_element_type=F32)
    x1 = x_ref[0] + mod_ref[0, 2:3, :] * y
    x1_ref[0] = x1
    h2 = (_rms(x1) * gn_ref[...]) * (1.0 + mod_ref[0, 4:5, :]) + mod_ref[0, 3:4, :]
    hp_ref[0] = h2
    w = wr_ref[...]
    w_head = w.astype(BF16)
    w_tail = (w - w_head.astype(F32)).astype(BF16)
    h_head = h2.astype(BF16)
    h_tail = (h2 - h_head.astype(F32)).astype(BF16)
    t = jnp.dot(h_head, jnp.concatenate([w_head, w_tail], axis=1), preferred_element_type=F32)
    logits = (t[:, :LANES] + t[:, LANES:]) + jnp.dot(h_tail, w_head, preferred_element_type=F32)
    lt = logits.T[:N_EXPERTS]
    aff_t = jax.nn.sigmoid(lt)
    sel_t = aff_t + br_ref[...]
    sel = [sel_t[e:e + 1] for e in range(N_EXPERTS)]
    aff = [aff_t[e:e + 1] for e in range(N_EXPERTS)]
    i0, i1, w0, w1 = _route(sel, aff)
    idx_ref[0] = jnp.concatenate([i0, i1], axis=0)
    w_rows = jnp.concatenate([w0, w1, jnp.zeros((LANES - TOP_K, w0.shape[1]), F32)], axis=0)
    wts_ref[0] = w_rows.T[:, :TOP_K]


def _out_proj(lhs, ws, bias, x, mod, gn, wr, br, tm):
    b, s, d = x.shape
    n_lhs = len(lhs)
    row = pl.BlockSpec((1, tm, d), lambda i, j: (i, j, 0))
    in_specs = ([pl.BlockSpec((1, tm, a.shape[-1]), lambda i, j: (i, j, 0)) for a in lhs]
                + [_resident(w.shape) for w in ws]
                + [_resident(bias.shape), row, pl.BlockSpec((1, 6, d), lambda i, j: (i, 0, 0)),
                   _resident(gn.shape), _resident(wr.shape), _resident(br.shape)])
    return pl.pallas_call(
        functools.partial(_out_kernel, n_lhs=n_lhs),
        grid=(b, s // tm),
        in_specs=in_specs,
        out_specs=[row, row,
                   pl.BlockSpec((1, TOP_K, tm), lambda i, j: (i, 0, j)),
                   pl.BlockSpec((1, tm, TOP_K), lambda i, j: (i, j, 0))],
        out_shape=[jax.ShapeDtypeStruct((b, s, d), F32), jax.ShapeDtypeStruct((b, s, d), F32),
                   jax.ShapeDtypeStruct((b, TOP_K, s), I32), jax.ShapeDtypeStruct((b, s, TOP_K), F32)],
        compiler_params=_cparams(("arbitrary", "arbitrary")),
        name="out_proj",
    )(*lhs, *ws, bias, x, mod, gn, wr, br)


def _dispatch(idx, blk):
    b, _, s = idx.shape
    n_asg = b * TOP_K * s
    assert N_EXPERTS * n_asg < 2 ** 31
    e = idx.reshape(n_asg)
    onehot = (e[None, :] == jnp.arange(N_EXPERTS, dtype=I32)[:, None]).astype(I32)
    csum = jnp.cumsum(onehot, axis=1)
    counts = csum[:, -1]
    rank = jnp.sum((csum - onehot) * onehot, axis=0)
    padded = (counts + blk - 1) // blk * blk
    pad_end = jnp.cumsum(padded)
    pad_start = pad_end - padded
    start = jnp.cumsum(counts) - counts
    dest = pad_start[e] + rank
    src_asg = jnp.sort(e * n_asg + jnp.arange(n_asg, dtype=I32)) % n_asg
    src_tok = src_asg // (TOP_K * s) * s + src_asg % s
    src_tok = jnp.concatenate([src_tok, jnp.zeros((blk,), src_tok.dtype)])
    n_blocks = -(-n_asg // blk) + N_EXPERTS
    n_used = pad_end[-1] // blk
    blk_ids = jnp.minimum(jnp.arange(n_blocks, dtype=I32), n_used - 1)
    blk_e = jnp.sum((blk_ids[:, None] * blk >= pad_end[None, :]).astype(I32), axis=1)
    blk_e = jnp.minimum(blk_e, N_EXPERTS - 1)
    blk_lo = start[blk_e] + blk_ids * blk - pad_start[blk_e]
    blk_rows = jnp.clip(start[blk_e] + counts[blk_e] - blk_lo, 0, blk)
    as_i32 = lambda a: a.astype(I32)
    return (as_i32(dest), as_i32(src_tok), as_i32(blk_e), as_i32(blk_lo), as_i32(blk_rows),
            as_i32(n_used.reshape(1)))


def _moe_kernel(blk_e_ref, lo_ref, rows_ref, n_used_ref, tok_ref, h_ref, wg_ref, wu_ref, wd_ref, o_ref, xbuf, sem,
                *, blk):
    i = pl.program_id(0)
    n_used = n_used_ref[0]
    slot = i % 2
    sizes = (blk // 4, blk // 2, 3 * blk // 4, blk)

    def gather(block, slot_):
        lo = lo_ref[block]
        rows = rows_ref[block]

        def row_copies(first, last):
            for r in range(first, last):
                t = tok_ref[lo + r]
                pltpu.make_async_copy(h_ref.at[pl.ds(t, 1)], xbuf.at[slot_, pl.ds(r, 1)], sem.at[slot_]).start()

        row_copies(0, sizes[0])
        for below, size in zip(sizes[:-1], sizes[1:]):
            @pl.when(rows > below)
            def _():
                row_copies(below, size)

    def ffn(rows):
        pltpu.make_async_copy(xbuf.at[slot, pl.ds(0, rows)], xbuf.at[slot, pl.ds(0, rows)], sem.at[slot]).wait()
        xb = xbuf[slot, :rows, :].astype(BF16)
        g = jnp.dot(xb, wg_ref[0, 0], preferred_element_type=F32)
        u = jnp.dot(xb, wu_ref[0, 0], preferred_element_type=F32)
        a = (g * jax.nn.sigmoid(g) * u).astype(BF16)
        o_ref[:rows, :] = jnp.dot(a, wd_ref[0, 0], preferred_element_type=F32)
        if rows < blk:
            o_ref[rows:, :] = jnp.zeros((blk - rows, o_ref.shape[1]), F32)

    @pl.when(i == 0)
    def _():
        gather(0, 0)

    @pl.when(i + 1 < n_used)
    def _():
        gather(i + 1, 1 - slot)

    used = i < n_used
    rows_i = rows_ref[i]
    for below, size in zip((0,) + sizes[:-1], sizes):
        @pl.when(used & (rows_i > below) & (rows_i <= size))
        def _():
            ffn(size)

    @pl.when(jnp.logical_not(used))
    def _():
        o_ref[...] = jnp.zeros_like(o_ref)


def _moe(h2, src_tok, blk_e, blk_lo, blk_rows, n_used, wg, wu, wd, layer, blk):
    n_blocks = blk_e.shape[0]
    _, _, d, de = wg.shape
    grid_spec = pltpu.PrefetchScalarGridSpec(
        num_scalar_prefetch=5,
        grid=(n_blocks,),
        in_specs=[
            pl.BlockSpec(memory_space=pl.ANY),
            pl.BlockSpec((1, 1, d, de), lambda i, be, *_: (layer, be[i], 0, 0)),
            pl.BlockSpec((1, 1, d, de), lambda i, be, *_: (layer, be[i], 0, 0)),
            pl.BlockSpec((1, 1, de, d), lambda i, be, *_: (layer, be[i], 0, 0)),
        ],
        out_specs=pl.BlockSpec((blk, d), lambda i, *_: (i, 0)),
        scratch_shapes=[pltpu.VMEM((2, blk, d), F32), pltpu.SemaphoreType.DMA((2,))],
    )
    return pl.pallas_call(
        functools.partial(_moe_kernel, blk=blk),
        grid_spec=grid_spec,
        out_shape=jax.ShapeDtypeStruct((n_blocks * blk, d), F32),
        compiler_params=_cparams(("arbitrary",)),
        name="moe_ffn",
    )(blk_e, blk_lo, blk_rows, n_used, src_tok, h2, wg, wu, wd)


def _combine_kernel(dest_ref, yp_ref, x_ref, wt_ref, mod_ref, *rest, tm, nt, with_norm):
    if with_norm:
        gn_ref, modn_ref, o_ref, hn_ref, buf, sem = rest
    else:
        o_ref, buf, sem = rest
    i = pl.program_id(0)
    j = pl.program_id(1)
    step = i * nt + j
    slot = step % 2

    def gather(step_, slot_):
        bases = [((step_ // nt) * TOP_K + k) * (nt * tm) + (step_ % nt) * tm for k in range(TOP_K)]
        for r in range(tm):
            for k in range(TOP_K):
                pltpu.make_async_copy(yp_ref.at[pl.ds(dest_ref[bases[k] + r], 1)],
                                      buf.at[slot_, k, pl.ds(r, 1)], sem.at[slot_]).start()

    @pl.when(step == 0)
    def _():
        gather(0, 0)

    @pl.when(step + 1 < pl.num_programs(0) * nt)
    def _():
        gather(step + 1, 1 - slot)

    pltpu.make_async_copy(buf.at[slot], buf.at[slot], sem.at[slot]).wait()
    w = wt_ref[0]
    y = buf[slot, 0] * w[:, 0:1] + buf[slot, 1] * w[:, 1:2]
    out = x_ref[0] + mod_ref[0, 5:6, :] * y
    o_ref[0] = out
    if with_norm:
        hn_ref[0] = ((_rms(out) * gn_ref[...]) * (1.0 + modn_ref[0, 1:2, :]) + modn_ref[0, 0:1, :]).astype(BF16)


def _combine(dest, yp, x1, wts, mod, tm, norm=None):
    b, s, d = x1.shape
    nt = s // tm
    with_norm = norm is not None
    row = pl.BlockSpec((1, tm, d), lambda i, j, ds: (i, j, 0))
    modspec = pl.BlockSpec((1, 6, d), lambda i, j, ds: (i, 0, 0))
    in_specs = [pl.BlockSpec(memory_space=pl.ANY), row,
                pl.BlockSpec((1, tm, TOP_K), lambda i, j, ds: (i, j, 0)), modspec]
    args = [yp, x1, wts, mod]
    out_shape = [jax.ShapeDtypeStruct((b, s, d), F32)]
    out_specs = [row]
    if with_norm:
        gn, modn = norm
        in_specs += [pl.BlockSpec(gn.shape, lambda i, j, ds: (0, 0)), modspec]
        args += [gn, modn]
        out_shape.append(jax.ShapeDtypeStruct((b, s, d), BF16))
        out_specs.append(row)
    grid_spec = pltpu.PrefetchScalarGridSpec(
        num_scalar_prefetch=1, grid=(b, nt), in_specs=in_specs, out_specs=out_specs,
        scratch_shapes=[pltpu.VMEM((2, TOP_K, tm, d), F32), pltpu.SemaphoreType.DMA((2,))])
    return pl.pallas_call(
        functools.partial(_combine_kernel, tm=tm, nt=nt, with_norm=with_norm),
        grid_spec=grid_spec,
        out_shape=out_shape,
        compiler_params=_cparams(("arbitrary", "arbitrary")),
        name="combine",
    )(dest, *args)


def _dft_constants(seq):
    n1, n2 = DFT_N1, DFT_N2
    assert seq == n1 * n2 and n2 == n1 * n1
    c = np.arange(F_GROUP_DIM)
    ang = 2 * np.pi * np.outer(c, c) / F_GROUP_DIM
    fc = np.concatenate([np.cos(ang), -np.sin(ang)], axis=1)
    a, k2, m = np.meshgrid(np.arange(n1), np.arange(n2), np.arange(n2), indexing="ij")
    ang_a = -2 * np.pi * (k2 * (a + n1 * m) % seq) / seq
    tre, tim = np.cos(ang_a), np.sin(ang_a)
    ma = np.concatenate([np.concatenate([tre, -tim], axis=2), np.concatenate([tim, tre], axis=2)], axis=1)
    ang_b = -2 * np.pi * np.outer(np.arange(n1), np.arange(n1)) / n1
    eye = np.eye(n1)
    mb = np.concatenate([np.kron(np.cos(ang_b), eye), -np.kron(np.sin(ang_b), eye)], axis=1)
    mb = mb / math.sqrt(seq * F_GROUP_DIM)
    return tuple(jnp.asarray(t, F32).astype(BF16) for t in (fc, ma, mb))


def _fourier_kernel(h_ref, fc_ref, ma_ref, mb_ref, o_ref, z_ref, yre, yim):
    n1, n2, gd = DFT_N1, DFT_N2, F_GROUP_DIM
    rows = FOURIER_ROWS
    n_tiles = 2 * gd // LANES
    for r in range(0, h_ref.shape[1], rows):
        z = jnp.dot(h_ref[0, r:r + rows, :], fc_ref[...], preferred_element_type=F32)
        for t in range(n_tiles):
            z_ref[t, r:r + rows, :] = z[:, t * LANES:(t + 1) * LANES]
    for a in range(n1):
        zs = [z_ref[t, pl.ds(a, n2, stride=n1), :].astype(BF16) for t in range(n_tiles)]
        rhs = jnp.concatenate([jnp.concatenate(zs[:n_tiles // 2], axis=1),
                               jnp.concatenate(zs[n_tiles // 2:], axis=1)], axis=0)
        y = jnp.dot(ma_ref[a], rhs, preferred_element_type=F32).astype(BF16)
        yre[a] = y[:n2]
        yim[a] = y[n2:]
    for hi in range(n1):
        rhs = jnp.concatenate([yre[a, hi * n1:(hi + 1) * n1, :] for a in range(n1)]
                              + [yim[a, hi * n1:(hi + 1) * n1, :] for a in range(n1)], axis=0)
        out = jnp.dot(mb_ref[...], rhs, preferred_element_type=F32).astype(BF16)
        for k1 in range(n1):
            o_ref[0, k1 * n2 + hi * n1:k1 * n2 + (hi + 1) * n1, :] = out[k1 * n1:(k1 + 1) * n1]


def _fourier(hn, consts):
    b, s, d = hn.shape
    fc, ma, mb = consts
    gd = F_GROUP_DIM
    return pl.pallas_call(
        _fourier_kernel,
        grid=(b, F_GROUPS),
        in_specs=[pl.BlockSpec((1, s, gd), lambda i, g: (i, 0, g)),
                  _resident(fc.shape), _resident(ma.shape), _resident(mb.shape)],
        out_specs=pl.BlockSpec((1, s, gd), lambda i, g: (i, 0, g)),
        out_shape=jax.ShapeDtypeStruct((b, s, d), BF16),
        scratch_shapes=[pltpu.VMEM((2 * gd // LANES, s, LANES), F32),
                        pltpu.VMEM((DFT_N1, DFT_N2, gd), BF16), pltpu.VMEM((DFT_N1, DFT_N2, gd), BF16)],
        compiler_params=_cparams(("arbitrary", "arbitrary")),
        name="fourier",
    )(hn, fc, ma, mb)


def _rope_tables(seq):
    pos = np.arange(seq)
    row, col = (pos // GRID_W).astype(np.float64), (pos % GRID_W).astype(np.float64)

    def tables(width):
        half = width // 2
        quarter = half // 2
        freqs = ROPE_BASE ** (-np.arange(0, half, 2, dtype=np.float64) / half)
        lane = np.arange(LANES)
        ang = np.where((lane < half)[None, :], row[:, None], col[:, None]) * freqs[lane % quarter][None, :]
        live = (lane < width)[None, :]
        first = ((lane % half) < quarter)[None, :]
        cos = np.where(live, np.cos(ang), 1.0)
        sneg = np.where(live & first, -np.sin(ang), 0.0)
        spos = np.where(live & ~first, np.sin(ang), 0.0)
        return [jnp.asarray(t, F32) for t in (cos, sneg, spos)]

    return tables(HEAD_DIM) + tables(ROPE_DIM)


def _identity_tables(n):
    one, zero = jnp.ones((n, LANES), F32), jnp.zeros((n, LANES), F32)
    return [one, zero, zero, one, zero, zero]


def _pad_heads(w, lead):
    w = w.reshape(lead, B_HEADS, QK_DIM)
    return jnp.pad(w, ((0, 0), (0, 0), (0, QK_PAD - QK_DIM))).reshape(lead, B_HEADS * QK_PAD)


def kernel(x, c, ctx, c_ctx, w_ada, b_ada, g_norm, w_in, g_aqn, g_akn, g_bq_lat, w_bq_up, g_bkv_lat, w_bkv_up,
           g_bqn, g_bkn, sink, w_o_ab, w_fo, b_fo, w_router, b_router, w_gate, w_up, w_down):
    b, s, d = x.shape
    n_ctx = ctx.shape[1]

    crows = jnp.concatenate([c, c_ctx[None, :], jnp.zeros((8 - b - 1, d), F32)], axis=0)
    mods, (win, wo, wfo) = _ada(crows, w_ada, b_ada, (w_in, w_o_ab, w_fo), (IN_PAD, d, d))
    mods = mods.reshape(DEPTH, 8, 6, d)
    mod_lat = [mods[l, :b] for l in range(DEPTH)]
    mod_ctx = jnp.broadcast_to(mods[0, b][None], (b, 6, d))

    wr = jnp.pad(w_router, ((0, 0), (0, LANES - N_EXPERTS)))
    br = b_router.reshape(N_EXPERTS, 1)

    wbq = _pad_heads(w_bq_up[0], Q_LORA).astype(BF16)
    wkv = w_bkv_up[0].reshape(KV_LORA, B_HEADS, NOPE_DIM + V_DIM)
    wbkv = jnp.concatenate([wkv[:, :, :NOPE_DIM].reshape(KV_LORA, -1), wkv[:, :, NOPE_DIM:].reshape(KV_LORA, -1)],
                           axis=1).astype(BF16)
    gains = (g_aqn[0][None], g_akn[0][None], g_bq_lat[0][None], g_bkv_lat[0][None],
             jnp.pad(g_bqn[0][None], ((0, 0), (0, QK_PAD - QK_DIM))),
             jnp.pad(g_bkn[0][None], ((0, 0), (0, QK_PAD - QK_DIM))))
    gn0 = g_norm[0, 0][None]
    aq, ak, av, bq, bk, bv = _proj(x, mod_lat[0], gn0, win, wbq, wbkv, *gains, _rope_tables(s), PROJ_TM)
    _, akc, avc, _, bkc, bvc = _proj(ctx, mod_ctx, gn0, win, wbq, wbkv, *gains, _identity_tables(n_ctx), n_ctx)

    ya = _win_attn(sink[0], aq, ak, av, akc, avc)
    yb, (wg, wu, wd) = _mla_attn(bq, bk, bv, bkc, bvc, (w_gate, w_up, w_down), MLA_TQ, MLA_TK)

    n_a = A_HEADS * HEAD_DIM
    x1, h2, idx, wts = _out_proj([ya, yb], [wo[:n_a], wo[n_a:]], jnp.zeros((1, d), F32), x, mod_lat[0],
                                 g_norm[0, 1][None], wr, br, OUT_TM)
    dest, *plan = _dispatch(idx, MOE_BLK)
    yp = _moe(h2.reshape(b * s, d), *plan, wg, wu, wd, 0, MOE_BLK)
    x2, hn = _combine(dest, yp, x1, wts, mod_lat[0], COMBINE_TM, norm=(g_norm[1, 0][None], mod_lat[1]))

    f = _fourier(hn, _dft_constants(s))
    x3, h2, idx, wts = _out_proj([f], [wfo], b_fo[0][None], x2, mod_lat[1],
                                 g_norm[1, 1][None], wr, br, OUT_TM)
    dest, *plan = _dispatch(idx, MOE_BLK)
    yp = _moe(h2.reshape(b * s, d), *plan, wg, wu, wd, 1, MOE_BLK)
    (x4,) = _combine(dest, yp, x3, wts, mod_lat[1], COMBINE_TM)
    return x4
```
